```python
import jax, jax.numpy as jnp
from jax import lax
import numpy as np

D_MODEL = 1024
BATCH = 4
SEQ = 8192
DEPTH = 2
DEC_BATCH = 16
DEC_SEQ = 16
PAST_LEN = 1024

CHUNK = 64
D_FF = 2816
CONV_W = 4
LRU_W = D_MODEL
LRU_BLOCKS = 8
LRU_BD = LRU_W // LRU_BLOCKS
LRU_C = 8.0
M_HEADS = 4
M_W = D_MODEL
M_HD = M_W // M_HEADS
R_HD = 64
R_W = D_MODEL
R_HEADS = R_W // R_HD
R_LORA_W = 64
R_LORA_A = 64
R_LORA_G = 128
N_BRANCH = 3
BR_W = D_MODEL
CONV_CH = LRU_W + 2 * M_W
R_IN = 3 * R_W + R_LORA_W + R_LORA_A + R_LORA_G
IN_SPLITS = [CONV_CH, CONV_CH + M_W, CONV_CH + 2 * M_W, CONV_CH + 2 * M_W + 2 * M_HEADS,
             CONV_CH + 2 * M_W + 2 * M_HEADS + R_IN]
IN_W = CONV_CH + 2 * M_W + 2 * M_HEADS + R_IN + N_BRANCH * D_MODEL
R_SPLITS = [R_W, 2 * R_W, 3 * R_W, 3 * R_W + R_LORA_W, 3 * R_W + R_LORA_W + R_LORA_A]
RMS_EPS = 1e-6
MH_EPS = 1e-6
RWKV_GN_EPS = 64e-5
N_STATE = 7

kernel_name = "hybrid_lru_mlstm_rwkv7_stream_step"


def rms_norm(x, g):
    xf = x.astype(jnp.float32)
    y = xf * lax.rsqrt(jnp.mean(xf * xf, axis=-1, keepdims=True) + RMS_EPS)
    return (y * g.astype(jnp.float32)).astype(x.dtype)


def head_layer_norm(h, eps):
    d = h - jnp.mean(h, axis=-1, keepdims=True)
    return d * lax.rsqrt(jnp.mean(d * d, axis=-1, keepdims=True) + eps)


def swiglu(x, w_gate, w_up, w_down):
    return (jax.nn.silu(x @ w_gate) * (x @ w_up)) @ w_down


def causal_conv(x, prev, w, b):
    t = x.shape[1]
    xp = jnp.concatenate([prev.astype(x.dtype), x], axis=1)
    y = b + xp[:, 0:t] * w[0]
    for j in range(1, CONV_W):
        y = y + xp[:, j:j + t] * w[j]
    return y, xp[:, t:]


def rg_lru(x, h0, wa, ba, wx, bx, lam):
    bsz, t, _ = x.shape
    xb = x.reshape(bsz, t, LRU_BLOCKS, LRU_BD)
    r = jax.nn.sigmoid(jnp.einsum("btni,nij->btnj", xb, wa).reshape(bsz, t, LRU_W) + ba)
    i = jax.nn.sigmoid(jnp.einsum("btni,nij->btnj", xb, wx).reshape(bsz, t, LRU_W) + bx)
    log_a = -LRU_C * r * jax.nn.softplus(-lam)
    a = jnp.exp(log_a)
    b = jnp.sqrt(-jnp.expm1(2.0 * log_a)) * (i * x)
    b = b.at[:, 0].add(a[:, 0] * h0)

    def combine(l, rr):
        return (l[0] * rr[0], rr[0] * l[1] + rr[1])

    _, h = lax.associative_scan(combine, (a, b), axis=1)
    return h, h[:, -1]


def mlstm_block(carry, inp):
    c_prev, n_prev, m_prev = carry
    q, k, v, ig, lf = inp
    lb = q.shape[1]
    bcum = jnp.cumsum(lf, axis=1)
    dmat = bcum[:, :, None, :] - bcum[:, None, :, :] + ig[:, None, :, :]
    causal = jnp.tril(jnp.ones((lb, lb), bool))
    dmat = jnp.where(causal[None, :, :, None], dmat, -jnp.inf)
    inter = bcum + m_prev[:, None, :]
    m_t = jnp.maximum(inter, jnp.max(dmat, axis=2))
    w_intra = jnp.exp(dmat - m_t[:, :, None, :])
    w_inter = jnp.exp(inter - m_t)
    s = jnp.einsum("bthd,bshd->btsh", q, k) * w_intra
    num = jnp.einsum("btsh,bshv->bthv", s, v) + w_inter[..., None] * jnp.einsum("bhvd,bthd->bthv", c_prev, q)
    den = jnp.sum(s, axis=2) + w_inter * jnp.einsum("bhd,bthd->bth", n_prev, q)
    h = num / jnp.maximum(jnp.abs(den), jnp.exp(-m_t))[..., None]
    m_new = m_t[:, -1]
    g_state = jnp.exp(bcum[:, -1] + m_prev - m_new)
    g_src = jnp.exp(bcum[:, -1:, :] - bcum + ig - m_new[:, None, :])
    c_new = g_state[..., None, None] * c_prev + jnp.einsum("bsh,bshv,bshd->bhvd", g_src, v, k)
    n_new = g_state[..., None] * n_prev + jnp.einsum("bsh,bshd->bhd", g_src, k)
    return (c_new, n_new, m_new), h


def mlstm(q, k, v, ig, lf, c0, n0, m0):
    bsz, t, nh, hd = q.shape
    lb = min(CHUNK, t)
    nc = t // lb

    def to_blocks(a):
        return jnp.moveaxis(a.reshape((bsz, nc, lb) + a.shape[2:]), 1, 0)

    (c, n, m), h = lax.scan(mlstm_block, (c0, n0, m0),
                            (to_blocks(q), to_blocks(k), to_blocks(v), to_blocks(ig), to_blocks(lf)))
    h = jnp.moveaxis(h, 0, 1).reshape(bsz, t, nh, hd)
    return h, c, n, m


def rwkv7_scan(r, w, k, v, a_vec, b_vec, s0):
    def step(s, inp):
        r_t, w_t, k_t, v_t, a_t, b_t = inp
        sa = jnp.einsum("bhij,bhj->bhi", s, a_t)
        s = s * w_t[:, :, None, :] + sa[..., None] * b_t[:, :, None, :] + v_t[..., None] * k_t[:, :, None, :]
        return s, jnp.einsum("bhij,bhj->bhi", s, r_t)

    seq = tuple(jnp.moveaxis(a, 1, 0) for a in (r, w, k, v, a_vec, b_vec))
    s, y = lax.scan(step, s0, seq)
    return jnp.moveaxis(y, 0, 1), s


def rwkv7(xs, s0, lp):
    bsz, t, _ = xs.shape
    r, k, v, wd, ad, gd = jnp.split(xs, R_SPLITS, axis=-1)
    w_log = -jax.nn.softplus(-(lp["rwkv_w0"] + jnp.tanh(wd) @ lp["rwkv_w2"])) - 0.5
    decay = jnp.exp(-jnp.exp(w_log))
    a = jax.nn.sigmoid(lp["rwkv_a0"] + ad @ lp["rwkv_a2"])
    g = jax.nn.sigmoid(gd) @ lp["rwkv_g2"]

    def heads(y):
        return y.reshape(bsz, t, R_HEADS, R_HD)

    kk = heads(k * lp["rwkv_k_k"])
    kk = kk * lax.rsqrt(jnp.maximum(jnp.sum(kk * kk, axis=-1, keepdims=True), 1e-24))
    k = k * (1.0 + (a - 1.0) * lp["rwkv_k_a"])
    r_h, k_h, v_h, a_h = heads(r), heads(k), heads(v), heads(a)
    y, s = rwkv7_scan(r_h, heads(decay), k_h, v_h, -kk, kk * a_h, s0)
    y = (head_layer_norm(y, RWKV_GN_EPS) * lp["rwkv_ln_w"].reshape(R_HEADS, R_HD)
         + lp["rwkv_ln_b"].reshape(R_HEADS, R_HD))
    y = y + jnp.sum(r_h * k_h * lp["rwkv_r_k"], axis=-1, keepdims=True) * v_h
    return y.reshape(bsz, t, R_W) * g, s


def token_mix(u, st, lp):
    conv_prev, lru_h, m_c, m_n, m_m, shift_prev, rwkv_s = st
    f32 = jnp.float32
    bsz, t, _ = u.shape
    z = u @ lp["w_in"]
    z_conv, z_mv, z_mo, z_if, z_rw, z_gate = jnp.split(z, IN_SPLITS, axis=-1)

    c, conv_new = causal_conv(z_conv, conv_prev, lp["conv_w"], lp["conv_b"])
    c = c.astype(f32)

    h_lru, lru_new = rg_lru(c[..., :LRU_W], lru_h.astype(f32), lp["lru_wa"], lp["lru_ba"],
                            lp["lru_wx"], lp["lru_bx"], lp["lru_lambda"])

    qk = jax.nn.silu(c[..., LRU_W:])
    q = qk[..., :M_W].reshape(bsz, t, M_HEADS, M_HD)
    k = qk[..., M_W:].reshape(bsz, t, M_HEADS, M_HD) * (M_HD ** -0.5)
    v = z_mv.astype(f32).reshape(bsz, t, M_HEADS, M_HD)
    gif = z_if.astype(f32) + lp["mlstm_if_bias"]
    ig = gif[..., :M_HEADS]
    lf = jax.nn.log_sigmoid(gif[..., M_HEADS:])
    h_m, c_new, n_new, m_new = mlstm(q, k, v, ig, lf, m_c.astype(f32), m_n.astype(f32), m_m.astype(f32))
    h_m = head_layer_norm(h_m, MH_EPS).reshape(bsz, t, M_W) * lp["mlstm_norm"]
    o_m = jax.nn.sigmoid(z_mo.astype(f32)) * h_m

    zr = z_rw.astype(f32)
    zp = jnp.concatenate([shift_prev.astype(f32), zr], axis=1)
    xs = zr + (zp[:, :-1] - zr) * lp["rwkv_mu"]
    shift_new = zr[:, -1:]
    o_r, s_new = rwkv7(xs, rwkv_s.astype(f32), lp)

    outs = jnp.stack([h_lru, o_m, o_r], axis=2)
    proj = jnp.einsum("btgc,gcd->btgd", outs, lp["w_branch"])
    gates = jax.nn.sigmoid(z_gate.astype(f32).reshape(bsz, t, N_BRANCH, D_MODEL))
    y = jnp.sum(gates * proj, axis=2) @ lp["w_out"]
    return y.astype(u.dtype), (conv_new, lru_new, c_new, n_new, m_new, shift_new, s_new)


def layer(x, st, lp):
    x = x + 0.5 * swiglu(rms_norm(x, lp["ffn1_norm"]), lp["ffn1_w_gate"], lp["ffn1_w_up"], lp["ffn1_w_down"])
    mix, new_st = token_mix(rms_norm(x, lp["mix_norm"]), st, lp)
    x = x + mix
    x = x + 0.5 * swiglu(rms_norm(x, lp["ffn2_norm"]), lp["ffn2_w_gate"], lp["ffn2_w_up"], lp["ffn2_w_down"])
    return x, new_st


def run_trunk(x, states, params, final_norm):
    new_states = []
    for l in range(DEPTH):
        lp = {name: w[l] for name, w in params.items()}
        x, st = layer(x, states[l], lp)
        new_states.append(st)
    stacked = tuple(jnp.stack([st[i] for st in new_states]) for i in range(N_STATE))
    return rms_norm(x, final_norm), stacked


def zero_states(bsz):
    f32 = jnp.float32
    return (jnp.zeros((bsz, CONV_W - 1, CONV_CH), f32), jnp.zeros((bsz, LRU_W), f32),
            jnp.zeros((bsz, M_HEADS, M_HD, M_HD), f32), jnp.zeros((bsz, M_HEADS, M_HD), f32),
            jnp.zeros((bsz, M_HEADS), f32), jnp.zeros((bsz, 1, R_IN), f32),
            jnp.zeros((bsz, R_HEADS, R_HD, R_HD), f32))


def setup_inputs(seed: int = 0) -> dict:
    key = jax.random.key(seed)
    keys = iter(jax.random.split(key, 64))

    def nrm(shape, scale):
        return scale * jax.random.normal(next(keys), shape, jnp.float32)

    def unif(shape, lo, hi):
        return jax.random.uniform(next(keys), shape, jnp.float32, lo, hi)

    L = DEPTH
    lam_a = unif((L, LRU_W), 0.9, 0.999) ** (1.0 / LRU_C)
    if_bias = jnp.concatenate([nrm((L, M_HEADS), 0.1),
                               jnp.broadcast_to(jnp.linspace(3.0, 6.0, M_HEADS), (L, M_HEADS))
                               + nrm((L, M_HEADS), 0.01)], axis=-1)
    return {
        "x_prompt": nrm((BATCH, SEQ, D_MODEL), 1.0),
        "x_sample": nrm((DEC_BATCH, DEC_SEQ, D_MODEL), 1.0),
        "state_conv": nrm((L, DEC_BATCH, CONV_W - 1, CONV_CH), 1.0),
        "state_lru_h": nrm((L, DEC_BATCH, LRU_W), 0.5),
        "state_mlstm_C": nrm((L, DEC_BATCH, M_HEADS, M_HD, M_HD), 0.05),
        "state_mlstm_n": nrm((L, DEC_BATCH, M_HEADS, M_HD), 0.05),
        "state_mlstm_m": nrm((L, DEC_BATCH, M_HEADS), 0.5),
        "state_rwkv_shift": nrm((L, DEC_BATCH, 1, R_IN), 1.0),
        "state_rwkv_S": nrm((L, DEC_BATCH, R_HEADS, R_HD, R_HD), 0.1),
        "ffn1_norm": 1.0 + nrm((L, D_MODEL), 0.02),
        "ffn1_w_gate": nrm((L, D_MODEL, D_FF), D_MODEL ** -0.5),
        "ffn1_w_up": nrm((L, D_MODEL, D_FF), D_MODEL ** -0.5),
        "ffn1_w_down": nrm((L, D_FF, D_MODEL), D_FF ** -0.5),
        "mix_norm": 1.0 + nrm((L, D_MODEL), 0.02),
        "w_in": nrm((L, D_MODEL, IN_W), D_MODEL ** -0.5),
        "conv_w": nrm((L, CONV_W, CONV_CH), 0.5),
        "conv_b": nrm((L, CONV_CH), 0.01),
        "lru_wa": nrm((L, LRU_BLOCKS, LRU_BD, LRU_BD), LRU_BD ** -0.5),
        "lru_ba": nrm((L, LRU_W), 0.01),
        "lru_wx": nrm((L, LRU_BLOCKS, LRU_BD, LRU_BD), LRU_BD ** -0.5),
        "lru_bx": nrm((L, LRU_W), 0.01),
        "lru_lambda": jnp.log(lam_a) - jnp.log1p(-lam_a),
        "mlstm_if_bias": if_bias,
        "mlstm_norm": 1.0 + nrm((L, M_W), 0.02),
        "rwkv_mu": unif((L, R_IN), 0.0, 1.0),
        "rwkv_w0": unif((L, R_W), -6.0, 1.0),
        "rwkv_w2": nrm((L, R_LORA_W, R_W), 0.1),
        "rwkv_a0": nrm((L, R_W), 0.1),
        "rwkv_a2": nrm((L, R_LORA_A, R_W), 0.5 * R_LORA_A ** -0.5),
        "rwkv_g2": nrm((L, R_LORA_G, R_W), R_LORA_G ** -0.5),
        "rwkv_k_k": 0.85 + nrm((L, R_W), 0.02),
        "rwkv_k_a": 1.0 + nrm((L, R_W), 0.02),
        "rwkv_r_k": nrm((L, R_HEADS, R_HD), 0.1),
        "rwkv_ln_w": 1.0 + nrm((L, R_W), 0.02),
        "rwkv_ln_b": nrm((L, R_W), 0.01),
        "w_branch": nrm((L, N_BRANCH, BR_W, D_MODEL), BR_W ** -0.5),
        "w_out": nrm((L, D_MODEL, D_MODEL), D_MODEL ** -0.5),
        "ffn2_norm": 1.0 + nrm((L, D_MODEL), 0.02),
        "ffn2_w_gate": nrm((L, D_MODEL, D_FF), D_MODEL ** -0.5),
        "ffn2_w_up": nrm((L, D_MODEL, D_FF), D_MODEL ** -0.5),
        "ffn2_w_down": nrm((L, D_FF, D_MODEL), D_FF ** -0.5),
        "final_norm": 1.0 + nrm((D_MODEL,), 0.02),
    }


def reference(x_prompt, x_sample, state_conv, state_lru_h, state_mlstm_C, state_mlstm_n,
              state_mlstm_m, state_rwkv_shift, state_rwkv_S,
              ffn1_norm, ffn1_w_gate, ffn1_w_up, ffn1_w_down, mix_norm, w_in, conv_w, conv_b,
              lru_wa, lru_ba, lru_wx, lru_bx, lru_lambda, mlstm_if_bias, mlstm_norm,
              rwkv_mu, rwkv_w0, rwkv_w2, rwkv_a0, rwkv_a2, rwkv_g2, rwkv_k_k, rwkv_k_a, rwkv_r_k,
              rwkv_ln_w, rwkv_ln_b, w_branch, w_out,
              ffn2_norm, ffn2_w_gate, ffn2_w_up, ffn2_w_down, final_norm):
    params = dict(ffn1_norm=ffn1_norm, ffn1_w_gate=ffn1_w_gate, ffn1_w_up=ffn1_w_up, ffn1_w_down=ffn1_w_down,
                  mix_norm=mix_norm, w_in=w_in, conv_w=conv_w, conv_b=conv_b,
                  lru_wa=lru_wa, lru_ba=lru_ba, lru_wx=lru_wx, lru_bx=lru_bx, lru_lambda=lru_lambda,
                  mlstm_if_bias=mlstm_if_bias, mlstm_norm=mlstm_norm,
                  rwkv_mu=rwkv_mu, rwkv_w0=rwkv_w0, rwkv_w2=rwkv_w2, rwkv_a0=rwkv_a0, rwkv_a2=rwkv_a2,
                  rwkv_g2=rwkv_g2, rwkv_k_k=rwkv_k_k, rwkv_k_a=rwkv_k_a, rwkv_r_k=rwkv_r_k,
                  rwkv_ln_w=rwkv_ln_w, rwkv_ln_b=rwkv_ln_b, w_branch=w_branch, w_out=w_out,
                  ffn2_norm=ffn2_norm, ffn2_w_gate=ffn2_w_gate, ffn2_w_up=ffn2_w_up, ffn2_w_down=ffn2_w_down)
    states_p = [zero_states(x_prompt.shape[0]) for _ in range(DEPTH)]
    y_prompt, (p_conv, p_lru_h, p_mlstm_C, p_mlstm_n, p_mlstm_m, p_rwkv_shift, p_rwkv_S) = run_trunk(
        x_prompt, states_p, params, final_norm)
    states_s = [(state_conv[l], state_lru_h[l], state_mlstm_C[l], state_mlstm_n[l], state_mlstm_m[l],
                 state_rwkv_shift[l], state_rwkv_S[l]) for l in range(DEPTH)]
    y_sample, (s_conv, s_lru_h, s_mlstm_C, s_mlstm_n, s_mlstm_m, s_rwkv_shift, s_rwkv_S) = run_trunk(
        x_sample, states_s, params, final_norm)
    return (y_prompt, y_sample,
            p_conv, p_lru_h, p_mlstm_C, p_mlstm_n, p_mlstm_m, p_rwkv_shift, p_rwkv_S,
            s_conv, s_lru_h, s_mlstm_C, s_mlstm_n, s_mlstm_m, s_rwkv_shift, s_rwkv_S)
```

```python
import functools

import jax
import jax.numpy as jnp
from jax import lax
from jax.experimental import pallas as pl
from jax.experimental.pallas import tpu as pltpu

F32 = jnp.float32
BF16 = jnp.bfloat16

V7X_LANES = 128
V7X_SUBLANES = 8
V7X_VMEM_BYTES = 64 * 1024 * 1024
VMEM_LIMIT = 56 * 1024 * 1024

D_MODEL = 1024
D_FF = 2816
CONV_W = 4
LRU_W = 1024
LRU_BLOCKS = 8
LRU_BD = LRU_W // LRU_BLOCKS
LRU_PAIR = 2 * LRU_BD
LRU_C = 8.0
M_HEADS = 4
M_HD = 256
R_HD = 64
R_HEADS = 16
R_PAIRS = R_HEADS // 2
R_W = 1024
R_IN = 3 * R_W + 256
CONV_CH = 3 * 1024
RMS_EPS = 1e-6
MH_EPS = 1e-6
RWKV_GN_EPS = 64e-5

ZC_CONV = 0
ZC_GATE = 24
ZC_MV = 48
ZC_MO = 56
ZC_RW = 64
ZC_IF = 90
Z_BLOCKS = 91
Z_W = Z_BLOCKS * V7X_LANES
Z_TN = 13 * V7X_LANES

FFN_TF = D_FF // 2

HI = lax.Precision.HIGHEST


def _cparams(sem):
    return pltpu.CompilerParams(dimension_semantics=sem, vmem_limit_bytes=VMEM_LIMIT)


def _rms(x, g):
    return (x * lax.rsqrt(jnp.mean(x * x, axis=-1, keepdims=True) + RMS_EPS)) * g


def _softplus(x):
    return jnp.maximum(x, 0.0) + jnp.log1p(jnp.exp(-jnp.abs(x)))


def _sigmoid(x):
    return jax.nn.sigmoid(x)


def _dot(a, b, prec=None):
    return jnp.dot(a, b, preferred_element_type=F32, precision=prec)


def _dot_nt(a, b, prec=None):
    return lax.dot_general(a, b, (((1,), (1,)), ((), ())), preferred_element_type=F32, precision=prec)


def _dot_tn(a, b, prec=None):
    return lax.dot_general(a, b, (((0,), (0,)), ((), ())), preferred_element_type=F32, precision=prec)


def _shift_rows(x, d, row):
    return jnp.where(row >= d, pltpu.roll(x, d, axis=0), 0.0)


def _cumsum_rows(x):
    n = x.shape[0]
    row = lax.broadcasted_iota(jnp.int32, x.shape, 0)
    d = 1
    while d < n:
        x = x + _shift_rows(x, d, row)
        d *= 2
    return x


def _ffn_kernel(x_ref, g_ref, wg_ref, wu_ref, wd_ref, *rest, final_norm):
    if final_norm:
        fg_ref, o_ref, xn_ref, acc_ref = rest
    else:
        o_ref, xn_ref, acc_ref = rest
    k = pl.program_id(1)

    @pl.when(k == 0)
    def _():
        xn_ref[...] = _rms(x_ref[...], g_ref[...]).astype(BF16)
        acc_ref[...] = jnp.zeros_like(acc_ref)

    xn = xn_ref[...]
    hg = _dot(xn, wg_ref[...])
    hu = _dot(xn, wu_ref[...])
    h = (hg * _sigmoid(hg)) * hu
    acc_ref[...] += _dot(h.astype(BF16), wd_ref[...])

    @pl.when(k == pl.num_programs(1) - 1)
    def _():
        y = x_ref[...] + 0.5 * acc_ref[...]
        if final_norm:
            y = _rms(y, fg_ref[...])
        o_ref[...] = y


def _ffn(x, g, wg, wu, wd, final_g=None):
    m = x.shape[0]
    tm = min(512, m)
    nk = D_FF // FFN_TF
    in_specs = [
        pl.BlockSpec((tm, D_MODEL), lambda i, k: (i, 0)),
        pl.BlockSpec((1, D_MODEL), lambda i, k: (0, 0)),
        pl.BlockSpec((D_MODEL, FFN_TF), lambda i, k: (0, k)),
        pl.BlockSpec((D_MODEL, FFN_TF), lambda i, k: (0, k)),
        pl.BlockSpec((FFN_TF, D_MODEL), lambda i, k: (k, 0)),
    ]
    args = [x, g.reshape(1, D_MODEL), wg, wu, wd]
    if final_g is not None:
        in_specs.append(pl.BlockSpec((1, D_MODEL), lambda i, k: (0, 0)))
        args.append(final_g.reshape(1, D_MODEL))
    return pl.pallas_call(
        functools.partial(_ffn_kernel, final_norm=final_g is not None),
        grid=(m // tm, nk),
        in_specs=in_specs,
        out_specs=pl.BlockSpec((tm, D_MODEL), lambda i, k: (i, 0)),
        out_shape=jax.ShapeDtypeStruct((m, D_MODEL), F32),
        scratch_shapes=[pltpu.VMEM((tm, D_MODEL), BF16), pltpu.VMEM((tm, D_MODEL), F32)],
        compiler_params=_cparams(("parallel", "arbitrary")),
        name="ffn",
    )(*args)


def _inproj_kernel(x_ref, g_ref, w_ref, o_ref, xn_ref):
    @pl.when(pl.program_id(1) == 0)
    def _():
        xn_ref[...] = _rms(x_ref[...], g_ref[...]).astype(BF16)

    o_ref[...] = _dot(xn_ref[...], w_ref[...])


def _inproj(x, g, w_all):
    m = x.shape[0]
    tm = min(512, m)
    return pl.pallas_call(
        _inproj_kernel,
        grid=(m // tm, Z_W // Z_TN),
        in_specs=[
            pl.BlockSpec((tm, D_MODEL), lambda i, j: (i, 0)),
            pl.BlockSpec((1, D_MODEL), lambda i, j: (0, 0)),
            pl.BlockSpec((D_MODEL, Z_TN), lambda i, j: (0, j)),
        ],
        out_specs=pl.BlockSpec((tm, Z_TN), lambda i, j: (i, j)),
        out_shape=jax.ShapeDtypeStruct((m, Z_W), F32),
        scratch_shapes=[pltpu.VMEM((tm, D_MODEL), BF16)],
        compiler_params=_cparams(("parallel", "arbitrary")),
        name="in_proj",
    )(x, g.reshape(1, D_MODEL), w_all)


def _conv_lru_kernel(z_ref, cprev_ref, h0_ref, cw_ref, cb_ref, w2_ref, bax_ref, lam_ref,
                     hl_ref, qk_ref, hlast_ref, xp_ref, a_ref, b_ref, hs_ref, hc_ref, *, tt):
    pad = V7X_SUBLANES

    @pl.when(pl.program_id(1) == 0)
    def _():
        xp_ref[pad - 3:pad, :] = cprev_ref[0]
        hc_ref[...] = h0_ref[0]

    x = z_ref[0]
    xp_ref[pad:pad + tt, :] = x
    c = cb_ref[...] + xp_ref[pad - 3:pad - 3 + tt, :] * cw_ref[0:1, :]
    c = c + xp_ref[pad - 2:pad - 2 + tt, :] * cw_ref[1:2, :]
    c = c + xp_ref[pad - 1:pad - 1 + tt, :] * cw_ref[2:3, :]
    c = c + x * cw_ref[3:4, :]
    xp_ref[pad - 3:pad, :] = x[tt - 3:tt, :]

    qk_ref[0, :, 0:1024] = (c[:, 1024:2048] * _sigmoid(c[:, 1024:2048])).astype(BF16)
    kk = c[:, 2048:3072]
    qk_ref[0, :, 1024:2048] = ((kk * _sigmoid(kk)) * (M_HD ** -0.5)).astype(BF16)

    sp = _softplus(-lam_ref[...])
    for p in range(LRU_W // LRU_PAIR):
        sl = slice(p * LRU_PAIR, (p + 1) * LRU_PAIR)
        xl = c[:, sl]
        gpre = _dot(xl.astype(BF16), w2_ref[p])
        r = _sigmoid(gpre[:, :LRU_PAIR] + bax_ref[0:1, sl])
        i = _sigmoid(gpre[:, LRU_PAIR:] + bax_ref[1:2, sl])
        log_a = (-LRU_C * r) * sp[:, sl]
        a = jnp.exp(log_a)
        mult = jnp.sqrt(-jnp.tanh(log_a) * (a * a + 1.0))
        a_ref[:, sl] = a
        b_ref[:, sl] = mult * (i * xl)

    def body(t, h):
        h = a_ref[pl.ds(t, 1), :] * h + b_ref[pl.ds(t, 1), :]
        hs_ref[pl.ds(t, 1), :] = h
        return h

    h = lax.fori_loop(0, tt, body, hc_ref[...], unroll=8)
    hc_ref[...] = h
    hlast_ref[0] = h
    hl_ref[0] = hs_ref[...].astype(BF16)


def _conv_lru(z, conv_prev, h0, cw, cb, w2, bax, lam):
    b, t, _ = z.shape
    tt = min(256, t)
    return pl.pallas_call(
        functools.partial(_conv_lru_kernel, tt=tt),
        grid=(b, t // tt),
        in_specs=[
            pl.BlockSpec((1, tt, CONV_CH), lambda i, j: (i, j, ZC_CONV * V7X_LANES // CONV_CH)),
            pl.BlockSpec((1, CONV_W - 1, CONV_CH), lambda i, j: (i, 0, 0)),
            pl.BlockSpec((1, 1, LRU_W), lambda i, j: (i, 0, 0)),
            pl.BlockSpec((CONV_W, CONV_CH), lambda i, j: (0, 0)),
            pl.BlockSpec((1, CONV_CH), lambda i, j: (0, 0)),
            pl.BlockSpec((LRU_W // LRU_PAIR, LRU_PAIR, 2 * LRU_PAIR), lambda i, j: (0, 0, 0)),
            pl.BlockSpec((2, LRU_W), lambda i, j: (0, 0)),
            pl.BlockSpec((1, LRU_W), lambda i, j: (0, 0)),
        ],
        out_specs=[
            pl.BlockSpec((1, tt, LRU_W), lambda i, j: (i, j, 0)),
            pl.BlockSpec((1, tt, 2048), lambda i, j: (i, j, 0)),
            pl.BlockSpec((1, 1, LRU_W), lambda i, j: (i, 0, 0)),
        ],
        out_shape=[
            jax.ShapeDtypeStruct((b, t, LRU_W), BF16),
            jax.ShapeDtypeStruct((b, t, 2048), BF16),
            jax.ShapeDtypeStruct((b, 1, LRU_W), F32),
        ],
        scratch_shapes=[
            pltpu.VMEM((tt + V7X_SUBLANES, CONV_CH), F32),
            pltpu.VMEM((tt, LRU_W), F32),
            pltpu.VMEM((tt, LRU_W), F32),
            pltpu.VMEM((tt, LRU_W), F32),
            pltpu.VMEM((1, LRU_W), F32),
        ],
        compiler_params=_cparams(("parallel", "arbitrary")),
        name="conv_lru",
    )(z, conv_prev, h0.reshape(b, 1, LRU_W), cw, cb.reshape(1, CONV_CH), w2, bax, lam.reshape(1, LRU_W))


def _mlstm_kernel(qk_ref, v_ref, o_ref, if_ref, c0_ref, n0_ref, m0_ref, ifb_ref, nw_ref,
                  om_ref, c_ref, n_ref, m_ref, *, lc):
    @pl.when(pl.program_id(1) == 0)
    def _():
        c_ref[...] = c0_ref[...]
        n_ref[...] = n0_ref[...]
        m_ref[...] = m0_ref[...]

    gif = if_ref[0] + ifb_ref[...]
    lf = jnp.minimum(gif, 0.0) - jnp.log1p(jnp.exp(-jnp.abs(gif)))
    cum = _cumsum_rows(lf)
    src = gif - pltpu.roll(cum, V7X_LANES - M_HEADS, axis=1)
    lane = lax.broadcasted_iota(jnp.int32, (lc, V7X_LANES), 1)
    tpos = lax.broadcasted_iota(jnp.int32, (lc, lc), 0)
    spos = lax.broadcasted_iota(jnp.int32, (lc, lc), 1)
    causal = spos <= tpos
    m_prev_all = m_ref[0]
    mlane = lax.broadcasted_iota(jnp.int32, (1, M_HEADS), 1)
    m_new_all = m_prev_all

    for h in range(M_HEADS):
        sl = slice(h * M_HD, (h + 1) * M_HD)
        q = qk_ref[0, :, sl]
        k = qk_ref[0, :, M_HEADS * M_HD + h * M_HD:M_HEADS * M_HD + (h + 1) * M_HD]
        v = v_ref[0, :, sl]
        bc = cum[:, M_HEADS + h:M_HEADS + h + 1]
        ig = gif[:, h:h + 1]
        m_prev = m_prev_all[:, h:h + 1]
        sel = jnp.where(lane == h, 1.0, 0.0)
        rowv = _dot_nt(sel, src, HI)
        dmat = jnp.where(causal, bc + rowv, -jnp.inf)
        inter = bc + m_prev
        m_t = jnp.maximum(inter, jnp.max(dmat, axis=1, keepdims=True))
        w_intra = jnp.exp(dmat - m_t)
        w_inter = jnp.exp(inter - m_t)
        s = _dot_nt(q, k) * w_intra
        c_prev = c_ref[0, h]
        n_prev = n_ref[0, h:h + 1, :]
        vb = v.astype(BF16)
        num = _dot(s.astype(BF16), vb) + w_inter * _dot_nt(q, c_prev.astype(BF16))
        qf = q.astype(F32)
        den = jnp.sum(s, axis=1, keepdims=True) + w_inter * jnp.sum(qf * n_prev, axis=1, keepdims=True)
        hh = num / jnp.maximum(jnp.abs(den), jnp.exp(-m_t))
        m_new = m_t[lc - 1:lc, :]
        bl = bc[lc - 1:lc, :]
        g_state = jnp.exp(bl + m_prev - m_new)
        g_src = jnp.exp(bl - bc + ig - m_new)
        c_ref[0, h] = g_state * c_prev + _dot_tn((g_src * v).astype(BF16), k)
        n_ref[0, h:h + 1, :] = g_state * n_prev + jnp.sum(g_src * k.astype(F32), axis=0, keepdims=True)
        m_new_all = jnp.where(mlane == h, m_new, m_new_all)
        d = hh - jnp.mean(hh, axis=1, keepdims=True)
        hn = d * lax.rsqrt(jnp.mean(d * d, axis=1, keepdims=True) + MH_EPS)
        om_ref[0, :, sl] = (_sigmoid(o_ref[0, :, sl]) * (hn * nw_ref[:, sl])).astype(BF16)

    m_ref[0] = m_new_all


def _mlstm(qk, z, c0, n0, m0, if_bias, norm_w):
    b, t, _ = z.shape
    lc = min(256, t)
    w = M_HEADS * M_HD
    return pl.pallas_call(
        functools.partial(_mlstm_kernel, lc=lc),
        grid=(b, t // lc),
        in_specs=[
            pl.BlockSpec((1, lc, 2 * w), lambda i, j: (i, j, 0)),
            pl.BlockSpec((1, lc, w), lambda i, j: (i, j, ZC_MV * V7X_LANES // w)),
            pl.BlockSpec((1, lc, w), lambda i, j: (i, j, ZC_MO * V7X_LANES // w)),
            pl.BlockSpec((1, lc, V7X_LANES), lambda i, j: (i, j, ZC_IF)),
            pl.BlockSpec((1, M_HEADS, M_HD, M_HD), lambda i, j: (i, 0, 0, 0)),
            pl.BlockSpec((1, M_HEADS, M_HD), lambda i, j: (i, 0, 0)),
            pl.BlockSpec((1, 1, M_HEADS), lambda i, j: (i, 0, 0)),
            pl.BlockSpec((1, V7X_LANES), lambda i, j: (0, 0)),
            pl.BlockSpec((1, w), lambda i, j: (0, 0)),
        ],
        out_specs=[
            pl.BlockSpec((1, lc, w), lambda i, j: (i, j, 0)),
            pl.BlockSpec((1, M_HEADS, M_HD, M_HD), lambda i, j: (i, 0, 0, 0)),
            pl.BlockSpec((1, M_HEADS, M_HD), lambda i, j: (i, 0, 0)),
            pl.BlockSpec((1, 1, M_HEADS), lambda i, j: (i, 0, 0)),
        ],
        out_shape=[
            jax.ShapeDtypeStruct((b, t, w), BF16),
            jax.ShapeDtypeStruct((b, M_HEADS, M_HD, M_HD), F32),
            jax.ShapeDtypeStruct((b, M_HEADS, M_HD), F32),
            jax.ShapeDtypeStruct((b, 1, M_HEADS), F32),
        ],
        compiler_params=_cparams(("parallel", "arbitrary")),
        name="mlstm",
    )(qk, z, z, z, c0, n0, m0.reshape(b, 1, M_HEADS), if_bias, norm_w.reshape(1, w))


def _seg_sum(x, e_ref, et_ref):
    hi = x.astype(BF16)
    lo = (x - hi.astype(F32)).astype(BF16)
    s = _dot(hi, e_ref[...]) + _dot(lo, e_ref[...])
    shi = s.astype(BF16)
    slo = (s - shi.astype(F32)).astype(BF16)
    return _dot(shi, et_ref[...]) + _dot(slo, et_ref[...])


def _rwkv_prep_kernel(r_ref, k_ref, v_ref, wa_ref, gd_ref, sh_ref, mu_ref, w0_ref, w2_ref, a0_ref, a2_ref,
                      g2_ref, kk_ref, ka_ref, e_ref, et_ref,
                      ro_ref, lwo_ref, ko_ref, vo_ref, ao_ref, bo_ref, go_ref, car_ref, *, tt):
    @pl.when(pl.program_id(1) == 0)
    def _():
        car_ref[...] = sh_ref[0]

    row = lax.broadcasted_iota(jnp.int32, (tt, 1), 0)

    def shifted(ref, lo, width):
        zp = ref[0]
        prev = jnp.where(row == 0, car_ref[:, lo:lo + width], pltpu.roll(zp, 1, axis=0))
        xs = zp + (prev - zp) * mu_ref[:, lo:lo + width]
        return zp, xs

    zr, r = shifted(r_ref, 0, R_W)
    zk, k = shifted(k_ref, R_W, R_W)
    zv, v = shifted(v_ref, 2 * R_W, R_W)
    zwa, xwa = shifted(wa_ref, 3 * R_W, V7X_LANES)
    zgd, xgd = shifted(gd_ref, 3 * R_W + V7X_LANES, V7X_LANES)
    car_ref[:, 0:R_W] = zr[tt - 1:tt, :]
    car_ref[:, R_W:2 * R_W] = zk[tt - 1:tt, :]
    car_ref[:, 2 * R_W:3 * R_W] = zv[tt - 1:tt, :]
    car_ref[:, 3 * R_W:3 * R_W + V7X_LANES] = zwa[tt - 1:tt, :]
    car_ref[:, 3 * R_W + V7X_LANES:R_IN] = zgd[tt - 1:tt, :]

    w_log = -_softplus(-(w0_ref[...] + _dot(jnp.tanh(xwa).astype(BF16), w2_ref[...]))) - 0.5
    lw = -jnp.exp(w_log)
    a = _sigmoid(a0_ref[...] + _dot(xwa.astype(BF16), a2_ref[...]))
    g = _dot(_sigmoid(xgd).astype(BF16), g2_ref[...])
    kk = k * kk_ref[...]
    ss = _seg_sum(kk * kk, e_ref, et_ref)
    kk = kk * lax.rsqrt(jnp.maximum(ss, 1e-24))
    kmod = k * (1.0 + (a - 1.0) * ka_ref[...])
    outs = ((ro_ref, r), (lwo_ref, lw), (ko_ref, kmod), (vo_ref, v), (ao_ref, -kk), (bo_ref, kk * a), (go_ref, g))
    for ref, val in outs:
        for p in range(R_PAIRS):
            ref[0, p] = val[:, p * V7X_LANES:(p + 1) * V7X_LANES]


def _rwkv_prep(z, shift_prev, mu, w0, w2p, a0, a2p, g2, k_k, k_a, e_mat, et_mat):
    b, t, _ = z.shape
    tt = min(256, t)
    row = lambda a: a.reshape(1, -1)
    zc = lambda blk, w: (lambda i, j: (i, j, blk * V7X_LANES // w))
    const = lambda i, j: (0, 0)
    out_sds = jax.ShapeDtypeStruct((b, R_PAIRS, t, V7X_LANES), F32)
    out_spec = pl.BlockSpec((1, R_PAIRS, tt, V7X_LANES), lambda i, j: (i, 0, j, 0))
    return pl.pallas_call(
        functools.partial(_rwkv_prep_kernel, tt=tt),
        grid=(b, t // tt),
        in_specs=[
            pl.BlockSpec((1, tt, R_W), zc(ZC_RW, R_W)),
            pl.BlockSpec((1, tt, R_W), zc(ZC_RW + 8, R_W)),
            pl.BlockSpec((1, tt, R_W), zc(ZC_RW + 16, R_W)),
            pl.BlockSpec((1, tt, V7X_LANES), zc(ZC_RW + 24, V7X_LANES)),
            pl.BlockSpec((1, tt, V7X_LANES), zc(ZC_RW + 25, V7X_LANES)),
            pl.BlockSpec((1, 1, R_IN), lambda i, j: (i, 0, 0)),
            pl.BlockSpec((1, R_IN), const),
            pl.BlockSpec((1, R_W), const),
            pl.BlockSpec((V7X_LANES, R_W), const),
            pl.BlockSpec((1, R_W), const),
            pl.BlockSpec((V7X_LANES, R_W), const),
            pl.BlockSpec((V7X_LANES, R_W), const),
            pl.BlockSpec((1, R_W), const),
            pl.BlockSpec((1, R_W), const),
            pl.BlockSpec((R_W, V7X_LANES), const),
            pl.BlockSpec((V7X_LANES, R_W), const),
        ],
        out_specs=[out_spec] * 7,
        out_shape=[out_sds] * 7,
        scratch_shapes=[pltpu.VMEM((1, R_IN), F32)],
        compiler_params=_cparams(("parallel", "arbitrary")),
        name="rwkv_prep",
    )(z, z, z, z, z, shift_prev, row(mu), row(w0), w2p, row(a0), a2p, g2, row(k_k), row(k_a), e_mat, et_mat)


def _rwkv_mix_kernel(r_ref, lw_ref, k_ref, v_ref, a_ref, b_ref, g_ref, s0_ref, lnw_ref, lnb_ref, rk_ref, e2_ref,
                     o_ref, s_ref, y_ref, *, tt, lc):
    @pl.when(pl.program_id(2) == 0)
    def _():
        s_ref[...] = s0_ref[...]

    tpos = lax.broadcasted_iota(jnp.int32, (lc, lc), 0)
    spos = lax.broadcasted_iota(jnp.int32, (lc, lc), 1)
    strict = spos < tpos
    incl = spos <= tpos
    n_sq = max(lc.bit_length() - 1, 0)

    def chunk(c, carry):
        off = pl.multiple_of(c * lc, lc)
        rows = pl.ds(off, lc)
        r = r_ref[0, 0, rows, :]
        lw = lw_ref[0, 0, rows, :]
        k = k_ref[0, 0, rows, :]
        v = v_ref[0, 0, rows, :]
        a = a_ref[0, 0, rows, :]
        b = b_ref[0, 0, rows, :]
        cum = _cumsum_rows(lw)
        cum_end = cum[lc - 1:lc, :]
        e_neg = jnp.exp(-cum)
        at = a * jnp.exp(cum - lw)
        rt = r * jnp.exp(cum)
        bt = b * e_neg
        kt = k * e_neg
        e_end = jnp.exp(cum_end - cum)
        be = b * e_end
        ke = k * e_end
        w_end = jnp.exp(cum_end)
        ys = []
        for h in range(2):
            sl = slice(h * R_HD, (h + 1) * R_HD)
            s0 = s_ref[0, h]
            ath, rth, bth, kth, vh = at[:, sl], rt[:, sl], bt[:, sl], kt[:, sl], v[:, sl]
            a_ab = jnp.where(strict, _dot_nt(ath, bth, HI), 0.0)
            a_ak = jnp.where(strict, _dot_nt(ath, kth, HI), 0.0)
            m_rb = jnp.where(incl, _dot_nt(rth, bth, HI), 0.0)
            m_rk = jnp.where(incl, _dot_nt(rth, kth, HI), 0.0)
            u = _dot_nt(ath, s0, HI) + _dot(a_ak, vh, HI)
            pw = a_ab
            for i in range(n_sq):
                u = u + _dot(pw, u, HI)
                if i + 1 < n_sq:
                    pw = _dot(pw, pw, HI)
            ys.append(_dot_nt(rth, s0, HI) + _dot(m_rb, u, HI) + _dot(m_rk, vh, HI))
            s_ref[0, h] = s0 * w_end[:, sl] + _dot_tn(u, be[:, sl], HI) + _dot_tn(vh, ke[:, sl], HI)
        y_ref[rows, :] = jnp.concatenate(ys, axis=1)
        return carry

    lax.fori_loop(0, tt // lc, chunk, 0)

    y = y_ref[...]
    r = r_ref[0, 0]
    k = k_ref[0, 0]
    v = v_ref[0, 0]

    def seg(x):
        hi = x.astype(BF16)
        lo = (x - hi.astype(F32)).astype(BF16)
        return _dot(hi, e2_ref[...]) + _dot(lo, e2_ref[...])

    d = y - seg(y) * (1.0 / R_HD)
    var = seg(d * d) * (1.0 / R_HD)
    yn = (d * lax.rsqrt(var + RWKV_GN_EPS)) * lnw_ref[0] + lnb_ref[0]
    yn = yn + seg((r * k) * rk_ref[0]) * v
    o_ref[0] = (yn * g_ref[0, 0]).astype(BF16)


def _rwkv_mix(prep, s0, ln_w, ln_b, r_k, e2):
    r, lw, k, v, a, b, g = prep
    bsz, _, t, _ = r.shape
    tt = min(512, t)
    lc = min(64, t)
    seq_spec = pl.BlockSpec((1, 1, tt, V7X_LANES), lambda i, p, j: (i, p, j, 0))
    par_spec = pl.BlockSpec((1, 1, V7X_LANES), lambda i, p, j: (p, 0, 0))
    st_spec = pl.BlockSpec((1, 2, R_HD, R_HD), lambda i, p, j: (i, p, 0, 0))
    pair = lambda x: x.reshape(R_PAIRS, 1, V7X_LANES)
    return pl.pallas_call(
        functools.partial(_rwkv_mix_kernel, tt=tt, lc=lc),
        grid=(bsz, R_PAIRS, t // tt),
        in_specs=[seq_spec] * 7 + [st_spec, par_spec, par_spec, par_spec,
                                   pl.BlockSpec((V7X_LANES, V7X_LANES), lambda i, p, j: (0, 0))],
        out_specs=[
            pl.BlockSpec((1, tt, V7X_LANES), lambda i, p, j: (i, j, p)),
            st_spec,
        ],
        out_shape=[
            jax.ShapeDtypeStruct((bsz, t, R_W), BF16),
            jax.ShapeDtypeStruct((bsz, R_HEADS, R_HD, R_HD), F32),
        ],
        scratch_shapes=[pltpu.VMEM((tt, V7X_LANES), F32)],
        compiler_params=_cparams(("parallel", "parallel", "arbitrary")),
        name="rwkv_mix",
    )(r, lw, k, v, a, b, g, s0, pair(ln_w), pair(ln_b), pair(r_k), e2)


def _merge_kernel(x_ref, zg_ref, hl_ref, om_ref, or_ref, wb_ref, wo_ref, o_ref):
    acc = None
    for gidx, br in enumerate((hl_ref, om_ref, or_ref)):
        sl = slice(gidx * D_MODEL, (gidx + 1) * D_MODEL)
        term = _sigmoid(zg_ref[:, sl]) * _dot(br[...], wb_ref[gidx])
        acc = term if acc is None else acc + term
    o_ref[...] = x_ref[...] + _dot(acc.astype(BF16), wo_ref[...])


def _merge(x, z2d, hl, om, orw, wb, wo):
    m = x.shape[0]
    tm = min(512, m)
    tok = pl.BlockSpec((tm, D_MODEL), lambda i: (i, 0))
    return pl.pallas_call(
        _merge_kernel,
        grid=(m // tm,),
        in_specs=[
            tok,
            pl.BlockSpec((tm, 3 * D_MODEL), lambda i: (i, ZC_GATE * V7X_LANES // (3 * D_MODEL))),
            tok, tok, tok,
            pl.BlockSpec((3, D_MODEL, D_MODEL), lambda i: (0, 0, 0)),
            pl.BlockSpec((D_MODEL, D_MODEL), lambda i: (0, 0)),
        ],
        out_specs=tok,
        out_shape=jax.ShapeDtypeStruct((m, D_MODEL), F32),
        compiler_params=_cparams(("parallel",)),
        name="merge",
    )(x, z2d, hl, om, orw, wb, wo)


def _prep_layer_weights(p, l):
    w_in = p["w_in"][l]
    c0, c1, c2, c3, c4 = 3072, 4096, 5120, 5128, 8456
    w_if = jnp.pad(w_in[:, c2:c3], ((0, 0), (0, V7X_LANES - 2 * M_HEADS)))
    w_all = jnp.concatenate([w_in[:, :c0], w_in[:, c4:], w_in[:, c0:c1], w_in[:, c1:c2], w_in[:, c3:c4], w_if],
                            axis=1).astype(BF16)
    def pairs(w):
        z = jnp.zeros((LRU_BD, LRU_BD), F32)
        return jnp.stack([jnp.block([[w[2 * i], z], [z, w[2 * i + 1]]]) for i in range(LRU_BLOCKS // 2)])
    w2 = jnp.concatenate([pairs(p["lru_wa"][l]), pairs(p["lru_wx"][l])], axis=2).astype(BF16)
    bax = jnp.stack([p["lru_ba"][l], p["lru_bx"][l]])
    if_bias = jnp.pad(p["mlstm_if_bias"][l], (0, V7X_LANES - 2 * M_HEADS)).reshape(1, V7X_LANES)
    zpad = jnp.zeros((64, R_W), F32)
    w2p = jnp.concatenate([p["rwkv_w2"][l], zpad], axis=0).astype(BF16)
    a2p = jnp.concatenate([zpad, p["rwkv_a2"][l]], axis=0).astype(BF16)
    return dict(
        w_all=w_all, w2=w2, bax=bax, if_bias=if_bias, w2p=w2p, a2p=a2p,
        g2=p["rwkv_g2"][l].astype(BF16),
        wb=p["w_branch"][l].astype(BF16), wo=p["w_out"][l].astype(BF16),
        f1=(p["ffn1_w_gate"][l].astype(BF16), p["ffn1_w_up"][l].astype(BF16), p["ffn1_w_down"][l].astype(BF16)),
        f2=(p["ffn2_w_gate"][l].astype(BF16), p["ffn2_w_up"][l].astype(BF16), p["ffn2_w_down"][l].astype(BF16)),
    )


def _segment_matrices():
    head = jnp.arange(R_W) // R_HD
    e_mat = (head[:, None] == jnp.arange(V7X_LANES)[None, :]).astype(BF16)
    half = jnp.arange(V7X_LANES) // R_HD
    e2 = (half[:, None] == half[None, :]).astype(BF16)
    return e_mat, e_mat.T, e2


def _run_group(x, states, p, lw_list, final_norm):
    bsz, t, _ = x.shape
    m = bsz * t
    e_mat, et_mat, e2 = _segment_matrices()
    xf = x.reshape(m, D_MODEL)
    new_states = []
    depth = p["w_in"].shape[0]
    for l in range(depth):
        lw = lw_list[l]
        conv_prev, lru_h, m_c, m_n, m_m, shift_prev, rwkv_s = states[l]
        xf = _ffn(xf, p["ffn1_norm"][l], *lw["f1"])
        z2d = _inproj(xf, p["mix_norm"][l], lw["w_all"])
        z = z2d.reshape(bsz, t, Z_W)
        hl, qk, h_last = _conv_lru(z, conv_prev, lru_h, p["conv_w"][l], p["conv_b"][l], lw["w2"], lw["bax"],
                                   p["lru_lambda"][l])
        om, c_new, n_new, m_new = _mlstm(qk, z, m_c, m_n, m_m, lw["if_bias"], p["mlstm_norm"][l])
        prep = _rwkv_prep(z, shift_prev, p["rwkv_mu"][l], p["rwkv_w0"][l], lw["w2p"], p["rwkv_a0"][l], lw["a2p"],
                          lw["g2"], p["rwkv_k_k"][l], p["rwkv_k_a"][l], e_mat, et_mat)
        orw, s_new = _rwkv_mix(prep, rwkv_s, p["rwkv_ln_w"][l], p["rwkv_ln_b"][l], p["rwkv_r_k"][l].reshape(-1), e2)
        xf = _merge(xf, z2d, hl.reshape(m, D_MODEL), om.reshape(m, D_MODEL), orw.reshape(m, D_MODEL),
                    lw["wb"], lw["wo"])
        xf = _ffn(xf, p["ffn2_norm"][l], *lw["f2"], final_g=final_norm if l == depth - 1 else None)
        conv_new = z[:, t - (CONV_W - 1):, ZC_CONV * V7X_LANES:ZC_CONV * V7X_LANES + CONV_CH]
        shift_new = z[:, t - 1:, ZC_RW * V7X_LANES:ZC_RW * V7X_LANES + R_IN]
        new_states.append((conv_new, h_last.reshape(bsz, LRU_W), c_new, n_new, m_new.reshape(bsz, M_HEADS),
                           shift_new, s_new))
    stacked = tuple(jnp.stack([st[i] for st in new_states]) for i in range(7))
    return xf.reshape(bsz, t, D_MODEL), stacked


def _zero_states(bsz):
    return (jnp.zeros((bsz, CONV_W - 1, CONV_CH), F32), jnp.zeros((bsz, LRU_W), F32),
            jnp.zeros((bsz, M_HEADS, M_HD, M_HD), F32), jnp.zeros((bsz, M_HEADS, M_HD), F32),
            jnp.zeros((bsz, M_HEADS), F32), jnp.zeros((bsz, 1, R_IN), F32),
            jnp.zeros((bsz, R_HEADS, R_HD, R_HD), F32))


def kernel(x_prompt, x_sample, state_conv, state_lru_h, state_mlstm_C, state_mlstm_n, state_mlstm_m, state_rwkv_shift, state_rwkv_S, ffn1_norm, ffn1_w_gate, ffn1_w_up, ffn1_w_down, mix_norm, w_in, conv_w, conv_b, lru_wa, lru_ba, lru_wx, lru_bx, lru_lambda, mlstm_if_bias, mlstm_norm, rwkv_mu, rwkv_w0, rwkv_w2, rwkv_a0, rwkv_a2, rwkv_g2, rwkv_k_k, rwkv_k_a, rwkv_r_k, rwkv_ln_w, rwkv_ln_b, w_branch, w_out, ffn2_norm, ffn2_w_gate, ffn2_w_up, ffn2_w_down, final_norm):
    p = dict(ffn1_norm=ffn1_norm, ffn1_w_gate=ffn1_w_gate, ffn1_w_up=ffn1_w_up, ffn1_w_down=ffn1_w_down,
             mix_norm=mix_norm, w_in=w_in, conv_w=conv_w, conv_b=conv_b,
             lru_wa=lru_wa, lru_ba=lru_ba, lru_wx=lru_wx, lru_bx=lru_bx, lru_lambda=lru_lambda,
             mlstm_if_bias=mlstm_if_bias, mlstm_norm=mlstm_norm,
             rwkv_mu=rwkv_mu, rwkv_w0=rwkv_w0, rwkv_w2=rwkv_w2, rwkv_a0=rwkv_a0, rwkv_a2=rwkv_a2,
             rwkv_g2=rwkv_g2, rwkv_k_k=rwkv_k_k, rwkv_k_a=rwkv_k_a, rwkv_r_k=rwkv_r_k,
             rwkv_ln_w=rwkv_ln_w, rwkv_ln_b=rwkv_ln_b, w_branch=w_branch, w_out=w_out,
             ffn2_norm=ffn2_norm, ffn2_w_gate=ffn2_w_gate, ffn2_w_up=ffn2_w_up, ffn2_w_down=ffn2_w_down)
    depth = w_in.shape[0]
    lw_list = [_prep_layer_weights(p, l) for l in range(depth)]
    states_p = [_zero_states(x_prompt.shape[0]) for _ in range(depth)]
    y_p, st_p = _run_group(x_prompt, states_p, p, lw_list, final_norm)
    states_s = [(state_conv[l], state_lru_h[l], state_mlstm_C[l], state_mlstm_n[l], state_mlstm_m[l],
                 state_rwkv_shift[l], state_rwkv_S[l]) for l in range(depth)]
    y_s, st_s = _run_group(x_sample, states_s, p, lw_list, final_norm)
    return (y_p, y_s) + st_p + st_s
```

```python
import functools

import jax
import jax.numpy as jnp
from jax import lax
from jax.experimental import pallas as pl
from jax.experimental.pallas import tpu as pltpu

F32 = jnp.float32
BF16 = jnp.bfloat16

V7X_LANES = 128
V7X_SUBLANES = 8
V7X_VMEM_BYTES = 64 * 1024 * 1024
VMEM_LIMIT = 56 * 1024 * 1024

D_MODEL = 1024
D_FF = 2816
CONV_W = 4
LRU_W = 1024
LRU_BLOCKS = 8
LRU_BD = LRU_W // LRU_BLOCKS
LRU_PAIR = 2 * LRU_BD
LRU_C = 8.0
M_HEADS = 4
M_HD = 256
R_HD = 64
R_HEADS = 16
R_PAIRS = R_HEADS // 2
R_W = 1024
R_IN = 3 * R_W + 256
CONV_CH = 3 * 1024
RMS_EPS = 1e-6
MH_EPS = 1e-6
RWKV_GN_EPS = 64e-5

ZC_CONV = 0
ZC_GATE = 24
ZC_MV = 48
ZC_MO = 56
ZC_RW = 64
ZC_IF = 90
Z_BLOCKS = 91
Z_W = Z_BLOCKS * V7X_LANES
Z_TN = 13 * V7X_LANES

FFN_TF = D_FF // 2
RWKV_CHUNK = 64
RWKV_GROUP = 128
RWKV_TT = 1024

HI = lax.Precision.HIGHEST


def _cparams(sem):
    return pltpu.CompilerParams(dimension_semantics=sem, vmem_limit_bytes=VMEM_LIMIT)


def _rms(x, g):
    return (x * lax.rsqrt(jnp.mean(x * x, axis=-1, keepdims=True) + RMS_EPS)) * g


def _softplus(x):
    return jnp.maximum(x, 0.0) + jnp.log1p(jnp.exp(-jnp.abs(x)))


def _sigmoid(x):
    return jax.nn.sigmoid(x)


def _dot(a, b, prec=None):
    return jnp.dot(a, b, preferred_element_type=F32, precision=prec)


def _dot_nt(a, b, prec=None):
    return lax.dot_general(a, b, (((1,), (1,)), ((), ())), preferred_element_type=F32, precision=prec)


def _dot_tn(a, b, prec=None):
    return lax.dot_general(a, b, (((0,), (0,)), ((), ())), preferred_element_type=F32, precision=prec)


def _shift_rows(x, d, row):
    return jnp.where(row >= d, pltpu.roll(x, d, axis=0), 0.0)


def _cumsum_rows(x):
    n = x.shape[0]
    row = lax.broadcasted_iota(jnp.int32, x.shape, 0)
    d = 1
    while d < n:
        x = x + _shift_rows(x, d, row)
        d *= 2
    return x


def _ffn_kernel(x_ref, g_ref, wg_ref, wu_ref, wd_ref, *rest, final_norm):
    if final_norm:
        fg_ref, o_ref, xn_ref, acc_ref = rest
    else:
        o_ref, xn_ref, acc_ref = rest
    k = pl.program_id(1)

    @pl.when(k == 0)
    def _():
        xn_ref[...] = _rms(x_ref[...], g_ref[...]).astype(BF16)
        acc_ref[...] = jnp.zeros_like(acc_ref)

    xn = xn_ref[...]
    hg = _dot(xn, wg_ref[...])
    hu = _dot(xn, wu_ref[...])
    h = (hg * _sigmoid(hg)) * hu
    acc_ref[...] += _dot(h.astype(BF16), wd_ref[...])

    @pl.when(k == pl.num_programs(1) - 1)
    def _():
        y = x_ref[...] + 0.5 * acc_ref[...]
        if final_norm:
            y = _rms(y, fg_ref[...])
        o_ref[...] = y


def _ffn(x, g, wg, wu, wd, final_g=None):
    m = x.shape[0]
    tm = min(512, m)
    nk = D_FF // FFN_TF
    in_specs = [
        pl.BlockSpec((tm, D_MODEL), lambda i, k: (i, 0)),
        pl.BlockSpec((1, D_MODEL), lambda i, k: (0, 0)),
        pl.BlockSpec((D_MODEL, FFN_TF), lambda i, k: (0, k)),
        pl.BlockSpec((D_MODEL, FFN_TF), lambda i, k: (0, k)),
        pl.BlockSpec((FFN_TF, D_MODEL), lambda i, k: (k, 0)),
    ]
    args = [x, g.reshape(1, D_MODEL), wg, wu, wd]
    if final_g is not None:
        in_specs.append(pl.BlockSpec((1, D_MODEL), lambda i, k: (0, 0)))
        args.append(final_g.reshape(1, D_MODEL))
    return pl.pallas_call(
        functools.partial(_ffn_kernel, final_norm=final_g is not None),
        grid=(m // tm, nk),
        in_specs=in_specs,
        out_specs=pl.BlockSpec((tm, D_MODEL), lambda i, k: (i, 0)),
        out_shape=jax.ShapeDtypeStruct((m, D_MODEL), F32),
        scratch_shapes=[pltpu.VMEM((tm, D_MODEL), BF16), pltpu.VMEM((tm, D_MODEL), F32)],
        compiler_params=_cparams(("parallel", "arbitrary")),
        name="ffn",
    )(*args)


def _inproj_kernel(x_ref, g_ref, w_ref, o_ref, xn_ref):
    @pl.when(pl.program_id(1) == 0)
    def _():
        xn_ref[...] = _rms(x_ref[...], g_ref[...]).astype(BF16)

    o_ref[...] = _dot(xn_ref[...], w_ref[...])


def _inproj(x, g, w_all):
    m = x.shape[0]
    tm = min(1024, m)
    return pl.pallas_call(
        _inproj_kernel,
        grid=(m // tm, Z_W // Z_TN),
        in_specs=[
            pl.BlockSpec((tm, D_MODEL), lambda i, j: (i, 0)),
            pl.BlockSpec((1, D_MODEL), lambda i, j: (0, 0)),
            pl.BlockSpec((D_MODEL, Z_TN), lambda i, j: (0, j)),
        ],
        out_specs=pl.BlockSpec((tm, Z_TN), lambda i, j: (i, j)),
        out_shape=jax.ShapeDtypeStruct((m, Z_W), F32),
        scratch_shapes=[pltpu.VMEM((tm, D_MODEL), BF16)],
        compiler_params=_cparams(("parallel", "arbitrary")),
        name="in_proj",
    )(x, g.reshape(1, D_MODEL), w_all)


def _conv_lru_kernel(z_ref, cprev_ref, h0_ref, cw_ref, cb_ref, w2_ref, bax_ref, lam_ref,
                     hl_ref, qk_ref, hlast_ref, xp_ref, a_ref, b_ref, hs_ref, hc_ref, *, tt):
    pad = V7X_SUBLANES

    @pl.when(pl.program_id(1) == 0)
    def _():
        xp_ref[pad - 3:pad, :] = cprev_ref[0]
        hc_ref[...] = h0_ref[0]

    x = z_ref[0]
    xp_ref[pad:pad + tt, :] = x
    c = cb_ref[...] + xp_ref[pad - 3:pad - 3 + tt, :] * cw_ref[0:1, :]
    c = c + xp_ref[pad - 2:pad - 2 + tt, :] * cw_ref[1:2, :]
    c = c + xp_ref[pad - 1:pad - 1 + tt, :] * cw_ref[2:3, :]
    c = c + x * cw_ref[3:4, :]
    xp_ref[pad - 3:pad, :] = x[tt - 3:tt, :]

    qk_ref[0, :, 0:1024] = (c[:, 1024:2048] * _sigmoid(c[:, 1024:2048])).astype(BF16)
    kk = c[:, 2048:3072]
    qk_ref[0, :, 1024:2048] = ((kk * _sigmoid(kk)) * (M_HD ** -0.5)).astype(BF16)

    sp = _softplus(-lam_ref[...])
    for p in range(LRU_W // LRU_PAIR):
        sl = slice(p * LRU_PAIR, (p + 1) * LRU_PAIR)
        xl = c[:, sl]
        gpre = _dot(xl.astype(BF16), w2_ref[p])
        r = _sigmoid(gpre[:, :LRU_PAIR] + bax_ref[0:1, sl])
        i = _sigmoid(gpre[:, LRU_PAIR:] + bax_ref[1:2, sl])
        log_a = (-LRU_C * r) * sp[:, sl]
        a = jnp.exp(log_a)
        mult = jnp.sqrt(-jnp.tanh(log_a) * (a * a + 1.0))
        a_ref[:, sl] = a
        b_ref[:, sl] = mult * (i * xl)

    def body(t, h):
        h = a_ref[pl.ds(t, 1), :] * h + b_ref[pl.ds(t, 1), :]
        hs_ref[pl.ds(t, 1), :] = h
        return h

    h = lax.fori_loop(0, tt, body, hc_ref[...], unroll=8)
    hc_ref[...] = h
    hlast_ref[0] = h
    hl_ref[0] = hs_ref[...].astype(BF16)


def _conv_lru(z, conv_prev, h0, cw, cb, w2, bax, lam):
    b, t, _ = z.shape
    tt = min(256, t)
    return pl.pallas_call(
        functools.partial(_conv_lru_kernel, tt=tt),
        grid=(b, t // tt),
        in_specs=[
            pl.BlockSpec((1, tt, CONV_CH), lambda i, j: (i, j, ZC_CONV * V7X_LANES // CONV_CH)),
            pl.BlockSpec((1, CONV_W - 1, CONV_CH), lambda i, j: (i, 0, 0)),
            pl.BlockSpec((1, 1, LRU_W), lambda i, j: (i, 0, 0)),
            pl.BlockSpec((CONV_W, CONV_CH), lambda i, j: (0, 0)),
            pl.BlockSpec((1, CONV_CH), lambda i, j: (0, 0)),
            pl.BlockSpec((LRU_W // LRU_PAIR, LRU_PAIR, 2 * LRU_PAIR), lambda i, j: (0, 0, 0)),
            pl.BlockSpec((2, LRU_W), lambda i, j: (0, 0)),
            pl.BlockSpec((1, LRU_W), lambda i, j: (0, 0)),
        ],
        out_specs=[
            pl.BlockSpec((1, tt, LRU_W), lambda i, j: (i, j, 0)),
            pl.BlockSpec((1, tt, 2048), lambda i, j: (i, j, 0)),
            pl.BlockSpec((1, 1, LRU_W), lambda i, j: (i, 0, 0)),
        ],
        out_shape=[
            jax.ShapeDtypeStruct((b, t, LRU_W), BF16),
            jax.ShapeDtypeStruct((b, t, 2048), BF16),
            jax.ShapeDtypeStruct((b, 1, LRU_W), F32),
        ],
        scratch_shapes=[
            pltpu.VMEM((tt + V7X_SUBLANES, CONV_CH), F32),
            pltpu.VMEM((tt, LRU_W), F32),
            pltpu.VMEM((tt, LRU_W), F32),
            pltpu.VMEM((tt, LRU_W), F32),
            pltpu.VMEM((1, LRU_W), F32),
        ],
        compiler_params=_cparams(("parallel", "arbitrary")),
        name="conv_lru",
    )(z, conv_prev, h0.reshape(b, 1, LRU_W), cw, cb.reshape(1, CONV_CH), w2, bax, lam.reshape(1, LRU_W))


def _mlstm_kernel(qk_ref, v_ref, o_ref, if_ref, c0_ref, n0_ref, m0_ref, ifb_ref, nw_ref,
                  om_ref, c_ref, n_ref, m_ref, *, lc):
    @pl.when(pl.program_id(1) == 0)
    def _():
        c_ref[...] = c0_ref[...]
        n_ref[...] = n0_ref[...]
        m_ref[...] = m0_ref[...]

    gif = if_ref[0] + ifb_ref[...]
    lf = jnp.minimum(gif, 0.0) - jnp.log1p(jnp.exp(-jnp.abs(gif)))
    cum = _cumsum_rows(lf)
    src = gif - pltpu.roll(cum, V7X_LANES - M_HEADS, axis=1)
    lane = lax.broadcasted_iota(jnp.int32, (lc, V7X_LANES), 1)
    tpos = lax.broadcasted_iota(jnp.int32, (lc, lc), 0)
    spos = lax.broadcasted_iota(jnp.int32, (lc, lc), 1)
    causal = spos <= tpos
    m_prev_all = m_ref[0]
    mlane = lax.broadcasted_iota(jnp.int32, (1, M_HEADS), 1)
    m_new_all = m_prev_all

    for h in range(M_HEADS):
        sl = slice(h * M_HD, (h + 1) * M_HD)
        q = qk_ref[0, :, sl]
        k = qk_ref[0, :, M_HEADS * M_HD + h * M_HD:M_HEADS * M_HD + (h + 1) * M_HD]
        v = v_ref[0, :, sl]
        bc = cum[:, M_HEADS + h:M_HEADS + h + 1]
        ig = gif[:, h:h + 1]
        m_prev = m_prev_all[:, h:h + 1]
        sel = jnp.where(lane == h, 1.0, 0.0)
        rowv = _dot_nt(sel, src, HI)
        dmat = jnp.where(causal, bc + rowv, -jnp.inf)
        inter = bc + m_prev
        m_t = jnp.maximum(inter, jnp.max(dmat, axis=1, keepdims=True))
        w_intra = jnp.exp(dmat - m_t)
        w_inter = jnp.exp(inter - m_t)
        s = _dot_nt(q, k) * w_intra
        c_prev = c_ref[0, h]
        n_prev = n_ref[0, h:h + 1, :]
        vb = v.astype(BF16)
        num = _dot(s.astype(BF16), vb) + w_inter * _dot_nt(q, c_prev.astype(BF16))
        qf = q.astype(F32)
        den = jnp.sum(s, axis=1, keepdims=True) + w_inter * jnp.sum(qf * n_prev, axis=1, keepdims=True)
        hh = num / jnp.maximum(jnp.abs(den), jnp.exp(-m_t))
        m_new = m_t[lc - 1:lc, :]
        bl = bc[lc - 1:lc, :]
        g_state = jnp.exp(bl + m_prev - m_new)
        g_src = jnp.exp(bl - bc + ig - m_new)
        c_ref[0, h] = g_state * c_prev + _dot_tn((g_src * v).astype(BF16), k)
        n_ref[0, h:h + 1, :] = g_state * n_prev + jnp.sum(g_src * k.astype(F32), axis=0, keepdims=True)
        m_new_all = jnp.where(mlane == h, m_new, m_new_all)
        d = hh - jnp.mean(hh, axis=1, keepdims=True)
        hn = d * lax.rsqrt(jnp.mean(d * d, axis=1, keepdims=True) + MH_EPS)
        om_ref[0, :, sl] = (_sigmoid(o_ref[0, :, sl]) * (hn * nw_ref[:, sl])).astype(BF16)

    m_ref[0] = m_new_all


def _mlstm(qk, z, c0, n0, m0, if_bias, norm_w):
    b, t, _ = z.shape
    lc = min(256, t)
    w = M_HEADS * M_HD
    return pl.pallas_call(
        functools.partial(_mlstm_kernel, lc=lc),
        grid=(b, t // lc),
        in_specs=[
            pl.BlockSpec((1, lc, 2 * w), lambda i, j: (i, j, 0)),
            pl.BlockSpec((1, lc, w), lambda i, j: (i, j, ZC_MV * V7X_LANES // w)),
            pl.BlockSpec((1, lc, w), lambda i, j: (i, j, ZC_MO * V7X_LANES // w)),
            pl.BlockSpec((1, lc, V7X_LANES), lambda i, j: (i, j, ZC_IF)),
            pl.BlockSpec((1, M_HEADS, M_HD, M_HD), lambda i, j: (i, 0, 0, 0)),
            pl.BlockSpec((1, M_HEADS, M_HD), lambda i, j: (i, 0, 0)),
            pl.BlockSpec((1, 1, M_HEADS), lambda i, j: (i, 0, 0)),
            pl.BlockSpec((1, V7X_LANES), lambda i, j: (0, 0)),
            pl.BlockSpec((1, w), lambda i, j: (0, 0)),
        ],
        out_specs=[
            pl.BlockSpec((1, lc, w), lambda i, j: (i, j, 0)),
            pl.BlockSpec((1, M_HEADS, M_HD, M_HD), lambda i, j: (i, 0, 0, 0)),
            pl.BlockSpec((1, M_HEADS, M_HD), lambda i, j: (i, 0, 0)),
            pl.BlockSpec((1, 1, M_HEADS), lambda i, j: (i, 0, 0)),
        ],
        out_shape=[
            jax.ShapeDtypeStruct((b, t, w), BF16),
            jax.ShapeDtypeStruct((b, M_HEADS, M_HD, M_HD), F32),
            jax.ShapeDtypeStruct((b, M_HEADS, M_HD), F32),
            jax.ShapeDtypeStruct((b, 1, M_HEADS), F32),
        ],
        compiler_params=_cparams(("parallel", "arbitrary")),
        name="mlstm",
    )(qk, z, z, z, c0, n0, m0.reshape(b, 1, M_HEADS), if_bias, norm_w.reshape(1, w))


def _seg_sum(x, e_ref, et_ref):
    hi = x.astype(BF16)
    lo = (x - hi.astype(F32)).astype(BF16)
    s = _dot(hi, e_ref[...]) + _dot(lo, e_ref[...])
    shi = s.astype(BF16)
    slo = (s - shi.astype(F32)).astype(BF16)
    return _dot(shi, et_ref[...]) + _dot(slo, et_ref[...])


def _rwkv_prep_kernel(r_ref, k_ref, v_ref, wa_ref, gd_ref, sh_ref, mu_ref, w0_ref, w2_ref, a0_ref, a2_ref,
                      g2_ref, kk_ref, ka_ref, e_ref, et_ref,
                      ro_ref, lwo_ref, ko_ref, vo_ref, ao_ref, bo_ref, go_ref, car_ref, *, tt):
    @pl.when(pl.program_id(1) == 0)
    def _():
        car_ref[...] = sh_ref[0]

    row = lax.broadcasted_iota(jnp.int32, (tt, 1), 0)

    def shifted(ref, lo, width):
        zp = ref[0]
        prev = jnp.where(row == 0, car_ref[:, lo:lo + width], pltpu.roll(zp, 1, axis=0))
        xs = zp + (prev - zp) * mu_ref[:, lo:lo + width]
        return zp, xs

    zr, r = shifted(r_ref, 0, R_W)
    zk, k = shifted(k_ref, R_W, R_W)
    zv, v = shifted(v_ref, 2 * R_W, R_W)
    zwa, xwa = shifted(wa_ref, 3 * R_W, V7X_LANES)
    zgd, xgd = shifted(gd_ref, 3 * R_W + V7X_LANES, V7X_LANES)
    car_ref[:, 0:R_W] = zr[tt - 1:tt, :]
    car_ref[:, R_W:2 * R_W] = zk[tt - 1:tt, :]
    car_ref[:, 2 * R_W:3 * R_W] = zv[tt - 1:tt, :]
    car_ref[:, 3 * R_W:3 * R_W + V7X_LANES] = zwa[tt - 1:tt, :]
    car_ref[:, 3 * R_W + V7X_LANES:R_IN] = zgd[tt - 1:tt, :]

    w_log = -_softplus(-(w0_ref[...] + _dot(jnp.tanh(xwa).astype(BF16), w2_ref[...]))) - 0.5
    lw = -jnp.exp(w_log)
    a = _sigmoid(a0_ref[...] + _dot(xwa.astype(BF16), a2_ref[...]))
    g = _dot(_sigmoid(xgd).astype(BF16), g2_ref[...])
    kk = k * kk_ref[...]
    ss = _seg_sum(kk * kk, e_ref, et_ref)
    kk = kk * lax.rsqrt(jnp.maximum(ss, 1e-24))
    kmod = k * (1.0 + (a - 1.0) * ka_ref[...])
    outs = ((ro_ref, r), (lwo_ref, lw), (ko_ref, kmod), (vo_ref, v), (ao_ref, -kk), (bo_ref, kk * a), (go_ref, g))
    for ref, val in outs:
        for p in range(R_PAIRS):
            ref[0, p] = val[:, p * V7X_LANES:(p + 1) * V7X_LANES]


def _rwkv_prep(z, shift_prev, mu, w0, w2p, a0, a2p, g2, k_k, k_a, e_mat, et_mat):
    b, t, _ = z.shape
    tt = min(256, t)
    row = lambda a: a.reshape(1, -1)
    zc = lambda blk, w: (lambda i, j: (i, j, blk * V7X_LANES // w))
    const = lambda i, j: (0, 0)
    out_sds = jax.ShapeDtypeStruct((b, R_PAIRS, t, V7X_LANES), F32)
    out_spec = pl.BlockSpec((1, R_PAIRS, tt, V7X_LANES), lambda i, j: (i, 0, j, 0))
    return pl.pallas_call(
        functools.partial(_rwkv_prep_kernel, tt=tt),
        grid=(b, t // tt),
        in_specs=[
            pl.BlockSpec((1, tt, R_W), zc(ZC_RW, R_W)),
            pl.BlockSpec((1, tt, R_W), zc(ZC_RW + 8, R_W)),
            pl.BlockSpec((1, tt, R_W), zc(ZC_RW + 16, R_W)),
            pl.BlockSpec((1, tt, V7X_LANES), zc(ZC_RW + 24, V7X_LANES)),
            pl.BlockSpec((1, tt, V7X_LANES), zc(ZC_RW + 25, V7X_LANES)),
            pl.BlockSpec((1, 1, R_IN), lambda i, j: (i, 0, 0)),
            pl.BlockSpec((1, R_IN), const),
            pl.BlockSpec((1, R_W), const),
            pl.BlockSpec((V7X_LANES, R_W), const),
            pl.BlockSpec((1, R_W), const),
            pl.BlockSpec((V7X_LANES, R_W), const),
            pl.BlockSpec((V7X_LANES, R_W), const),
            pl.BlockSpec((1, R_W), const),
            pl.BlockSpec((1, R_W), const),
            pl.BlockSpec((R_W, V7X_LANES), const),
            pl.BlockSpec((V7X_LANES, R_W), const),
        ],
        out_specs=[out_spec] * 7,
        out_shape=[out_sds] * 7,
        scratch_shapes=[pltpu.VMEM((1, R_IN), F32)],
        compiler_params=_cparams(("parallel", "arbitrary")),
        name="rwkv_prep",
    )(z, z, z, z, z, shift_prev, row(mu), row(w0), w2p, row(a0), a2p, g2, row(k_k), row(k_a), e_mat, et_mat)


def _rwkv_mix_kernel(r_ref, lw_ref, k_ref, v_ref, a_ref, b_ref, g_ref, s0_ref, lnw_ref, lnb_ref, rk_ref, e2_ref,
                     o_ref, s_ref, y_ref, *, tt, lc, rg):
    @pl.when(pl.program_id(2) == 0)
    def _():
        s_ref[...] = s0_ref[...]

    nb = rg // lc
    trow = lax.broadcasted_iota(jnp.int32, (rg, 2 * rg), 0)
    scol = lax.broadcasted_iota(jnp.int32, (rg, 2 * rg), 1)
    scol = jnp.where(scol >= rg, scol - rg, scol)
    dist = trow - scol
    tin = trow & (lc - 1)
    smask = (dist > 0) & (dist <= tin)
    imask = (dist >= 0) & (dist <= tin)
    crow = lax.broadcasted_iota(jnp.int32, (rg, V7X_LANES), 0) & (lc - 1)
    xrow = lax.broadcasted_iota(jnp.int32, (rg, nb * R_HD), 0)
    xcol = lax.broadcasted_iota(jnp.int32, (rg, nb * R_HD), 1)
    xmask = (xrow >> (lc.bit_length() - 1)) == (xcol >> (R_HD.bit_length() - 1))
    n_sq = max(lc.bit_length() - 1, 0)

    groups = range(tt // rg)
    chains = [(g, h) for g in groups for h in range(2)]
    hs = lambda h: slice(h * R_HD, (h + 1) * R_HD)
    ew = {}
    for g in groups:
        rows = slice(g * rg, (g + 1) * rg)
        r = r_ref[0, 0, rows, :]
        lw = lw_ref[0, 0, rows, :]
        k = k_ref[0, 0, rows, :]
        b = b_ref[0, 0, rows, :]
        cum = lw
        d = 1
        while d < lc:
            cum = cum + jnp.where(crow >= d, pltpu.roll(cum, d, axis=0), 0.0)
            d *= 2
        ends = [cum[c * lc + lc - 1:c * lc + lc, :] for c in range(nb)]
        cum_end = ends[0] if nb == 1 else jnp.concatenate(
            [jnp.broadcast_to(e, (lc, V7X_LANES)) for e in ends], axis=0)
        e_neg = jnp.exp(-cum)
        at = a_ref[0, 0, rows, :] * jnp.exp(cum - lw)
        rt = r * jnp.exp(cum)
        e_end = jnp.exp(cum_end - cum)
        ew[g] = dict(at=at, rt=rt, atb=at.astype(BF16), rtb=rt.astype(BF16),
                     rhs=jnp.concatenate([b * e_neg, k * e_neg], axis=0).astype(BF16),
                     be=(b * e_end).astype(BF16), ke=(k * e_end).astype(BF16),
                     vb=v_ref[0, 0, rows, :].astype(BF16), w_end=[jnp.exp(e) for e in ends])

    def expand(x):
        if nb == 1:
            return x
        return jnp.where(xmask, jnp.concatenate([x] * nb, axis=1), jnp.zeros((), x.dtype))

    ga = {c: jnp.where(smask, _dot_nt(ew[c[0]]["atb"][:, hs(c[1])], ew[c[0]]["rhs"][:, hs(c[1])]), 0.0) for c in chains}
    gr = {c: jnp.where(imask, _dot_nt(ew[c[0]]["rtb"][:, hs(c[1])], ew[c[0]]["rhs"][:, hs(c[1])]), 0.0) for c in chains}
    gva = {c: _dot(ga[c][:, rg:].astype(BF16), ew[c[0]]["vb"][:, hs(c[1])]) for c in chains}
    gvr = {c: _dot(gr[c][:, rg:].astype(BF16), ew[c[0]]["vb"][:, hs(c[1])]) for c in chains}
    x = {c: jnp.concatenate([ew[c[0]]["at"][:, hs(c[1])], gva[c]], axis=1) for c in chains}
    pw = {c: ga[c][:, :rg] for c in chains}
    for i in range(n_sq):
        pwb = {c: pw[c].astype(BF16) for c in chains}
        x = {c: x[c] + _dot(pwb[c], x[c].astype(BF16)) for c in chains}
        if i + 1 < n_sq:
            pw = {c: _dot(pwb[c], pwb[c]) for c in chains}
    xb = {c: x[c].astype(BF16) for c in chains}
    pq2 = {c: jnp.concatenate([ew[c[0]]["rt"][:, hs(c[1])], gvr[c]], axis=1)
           + _dot(gr[c][:, :rg].astype(BF16), xb[c]) for c in chains}
    xe = {c: _dot_tn(xb[c], expand(ew[c[0]]["be"][:, hs(c[1])])) for c in chains}
    vk = {c: _dot_tn(ew[c[0]]["vb"][:, hs(c[1])], expand(ew[c[0]]["ke"][:, hs(c[1])])) for c in chains}

    s = [s_ref[0, 0], s_ref[0, 1]]
    for g in groups:
        for c in range(nb):
            ys = []
            rows = slice(c * lc, (c + 1) * lc)
            cols = slice(c * R_HD, (c + 1) * R_HD)
            for h in range(2):
                sb = s[h].astype(BF16)
                ys.append(_dot_nt(pq2[g, h][rows, :R_HD].astype(BF16), sb) + pq2[g, h][rows, R_HD:])
                s[h] = (s[h] * ew[g]["w_end"][c][:, hs(h)] + _dot(sb, xe[g, h][:R_HD, cols].astype(BF16))
                        + (xe[g, h][R_HD:, cols] + vk[g, h][:, cols]))
            y_ref[g * rg + c * lc:g * rg + (c + 1) * lc, :] = jnp.concatenate(ys, axis=1)
    s_ref[0, 0] = s[0]
    s_ref[0, 1] = s[1]

    y = y_ref[...]
    r = r_ref[0, 0]
    k = k_ref[0, 0]
    v = v_ref[0, 0]

    def seg(x):
        hi = x.astype(BF16)
        lo = (x - hi.astype(F32)).astype(BF16)
        return _dot(hi, e2_ref[...]) + _dot(lo, e2_ref[...])

    d = y - seg(y) * (1.0 / R_HD)
    var = seg(d * d) * (1.0 / R_HD)
    yn = (d * lax.rsqrt(var + RWKV_GN_EPS)) * lnw_ref[0] + lnb_ref[0]
    yn = yn + seg((r * k) * rk_ref[0]) * v
    o_ref[0] = (yn * g_ref[0, 0]).astype(BF16)


def _rwkv_mix(prep, s0, ln_w, ln_b, r_k, e2):
    r, lw, k, v, a, b, g = prep
    bsz, _, t, _ = r.shape
    tt = min(RWKV_TT, t)
    lc = min(RWKV_CHUNK, t)
    rg = min(RWKV_GROUP, t)
    seq_spec = pl.BlockSpec((1, 1, tt, V7X_LANES), lambda i, p, j: (i, p, j, 0))
    par_spec = pl.BlockSpec((1, 1, V7X_LANES), lambda i, p, j: (p, 0, 0))
    st_spec = pl.BlockSpec((1, 2, R_HD, R_HD), lambda i, p, j: (i, p, 0, 0))
    pair = lambda x: x.reshape(R_PAIRS, 1, V7X_LANES)
    return pl.pallas_call(
        functools.partial(_rwkv_mix_kernel, tt=tt, lc=lc, rg=rg),
        grid=(bsz, R_PAIRS, t // tt),
        in_specs=[seq_spec] * 7 + [st_spec, par_spec, par_spec, par_spec,
                                   pl.BlockSpec((V7X_LANES, V7X_LANES), lambda i, p, j: (0, 0))],
        out_specs=[
            pl.BlockSpec((1, tt, V7X_LANES), lambda i, p, j: (i, j, p)),
            st_spec,
        ],
        out_shape=[
            jax.ShapeDtypeStruct((bsz, t, R_W), BF16),
            jax.ShapeDtypeStruct((bsz, R_HEADS, R_HD, R_HD), F32),
        ],
        scratch_shapes=[pltpu.VMEM((tt, V7X_LANES), F32)],
        compiler_params=_cparams(("parallel", "parallel", "arbitrary")),
        name="rwkv_mix",
    )(r, lw, k, v, a, b, g, s0, pair(ln_w), pair(ln_b), pair(r_k), e2)


def _merge_kernel(x_ref, zg_ref, hl_ref, om_ref, or_ref, wb_ref, wo_ref, o_ref):
    acc = None
    for gidx, br in enumerate((hl_ref, om_ref, or_ref)):
        sl = slice(gidx * D_MODEL, (gidx + 1) * D_MODEL)
        term = _sigmoid(zg_ref[:, sl]) * _dot(br[...], wb_ref[gidx])
        acc = term if acc is None else acc + term
    o_ref[...] = x_ref[...] + _dot(acc.astype(BF16), wo_ref[...])


def _merge(x, z2d, hl, om, orw, wb, wo):
    m = x.shape[0]
    tm = min(512, m)
    tok = pl.BlockSpec((tm, D_MODEL), lambda i: (i, 0))
    return pl.pallas_call(
        _merge_kernel,
        grid=(m // tm,),
        in_specs=[
            tok,
            pl.BlockSpec((tm, 3 * D_MODEL), lambda i: (i, ZC_GATE * V7X_LANES // (3 * D_MODEL))),
            tok, tok, tok,
            pl.BlockSpec((3, D_MODEL, D_MODEL), lambda i: (0, 0, 0)),
            pl.BlockSpec((D_MODEL, D_MODEL), lambda i: (0, 0)),
        ],
        out_specs=tok,
        out_shape=jax.ShapeDtypeStruct((m, D_MODEL), F32),
        compiler_params=_cparams(("parallel",)),
        name="merge",
    )(x, z2d, hl, om, orw, wb, wo)


def _prep_layer_weights(p, l):
    w_in = p["w_in"][l]
    c0, c1, c2, c3, c4 = 3072, 4096, 5120, 5128, 8456
    w_if = jnp.pad(w_in[:, c2:c3], ((0, 0), (0, V7X_LANES - 2 * M_HEADS)))
    w_all = jnp.concatenate([w_in[:, :c0], w_in[:, c4:], w_in[:, c0:c1], w_in[:, c1:c2], w_in[:, c3:c4], w_if],
                            axis=1).astype(BF16)
    def pairs(w):
        z = jnp.zeros((LRU_BD, LRU_BD), F32)
        return jnp.stack([jnp.block([[w[2 * i], z], [z, w[2 * i + 1]]]) for i in range(LRU_BLOCKS // 2)])
    w2 = jnp.concatenate([pairs(p["lru_wa"][l]), pairs(p["lru_wx"][l])], axis=2).astype(BF16)
    bax = jnp.stack([p["lru_ba"][l], p["lru_bx"][l]])
    if_bias = jnp.pad(p["mlstm_if_bias"][l], (0, V7X_LANES - 2 * M_HEADS)).reshape(1, V7X_LANES)
    zpad = jnp.zeros((64, R_W), F32)
    w2p = jnp.concatenate([p["rwkv_w2"][l], zpad], axis=0).astype(BF16)
    a2p = jnp.concatenate([zpad, p["rwkv_a2"][l]], axis=0).astype(BF16)
    return dict(
        w_all=w_all, w2=w2, bax=bax, if_bias=if_bias, w2p=w2p, a2p=a2p,
        g2=p["rwkv_g2"][l].astype(BF16),
        wb=p["w_branch"][l].astype(BF16), wo=p["w_out"][l].astype(BF16),
        f1=(p["ffn1_w_gate"][l].astype(BF16), p["ffn1_w_up"][l].astype(BF16), p["ffn1_w_down"][l].astype(BF16)),
        f2=(p["ffn2_w_gate"][l].astype(BF16), p["ffn2_w_up"][l].astype(BF16), p["ffn2_w_down"][l].astype(BF16)),
    )


def _segment_matrices():
    head = jnp.arange(R_W) // R_HD
    e_mat = (head[:, None] == jnp.arange(V7X_LANES)[None, :]).astype(BF16)
    half = jnp.arange(V7X_LANES) // R_HD
    e2 = (half[:, None] == half[None, :]).astype(BF16)
    return e_mat, e_mat.T, e2


def _run_group(x, states, p, lw_list, final_norm):
    bsz, t, _ = x.shape
    m = bsz * t
    e_mat, et_mat, e2 = _segment_matrices()
    xf = x.reshape(m, D_MODEL)
    new_states = []
    depth = p["w_in"].shape[0]
    for l in range(depth):
        lw = lw_list[l]
        conv_prev, lru_h, m_c, m_n, m_m, shift_prev, rwkv_s = states[l]
        xf = _ffn(xf, p["ffn1_norm"][l], *lw["f1"])
        z2d = _inproj(xf, p["mix_norm"][l], lw["w_all"])
        z = z2d.reshape(bsz, t, Z_W)
        hl, qk, h_last = _conv_lru(z, conv_prev, lru_h, p["conv_w"][l], p["conv_b"][l], lw["w2"], lw["bax"],
                                   p["lru_lambda"][l])
        om, c_new, n_new, m_new = _mlstm(qk, z, m_c, m_n, m_m, lw["if_bias"], p["mlstm_norm"][l])
        prep = _rwkv_prep(z, shift_prev, p["rwkv_mu"][l], p["rwkv_w0"][l], lw["w2p"], p["rwkv_a0"][l], lw["a2p"],
                          lw["g2"], p["rwkv_k_k"][l], p["rwkv_k_a"][l], e_mat, et_mat)
        orw, s_new = _rwkv_mix(prep, rwkv_s, p["rwkv_ln_w"][l], p["rwkv_ln_b"][l], p["rwkv_r_k"][l].reshape(-1), e2)
        xf = _merge(xf, z2d, hl.reshape(m, D_MODEL), om.reshape(m, D_MODEL), orw.reshape(m, D_MODEL),
                    lw["wb"], lw["wo"])
        xf = _ffn(xf, p["ffn2_norm"][l], *lw["f2"], final_g=final_norm if l == depth - 1 else None)
        conv_new = z[:, t - (CONV_W - 1):, ZC_CONV * V7X_LANES:ZC_CONV * V7X_LANES + CONV_CH]
        shift_new = z[:, t - 1:, ZC_RW * V7X_LANES:ZC_RW * V7X_LANES + R_IN]
        new_states.append((conv_new, h_last.reshape(bsz, LRU_W), c_new, n_new, m_new.reshape(bsz, M_HEADS),
                           shift_new, s_new))
    stacked = tuple(jnp.stack([st[i] for st in new_states]) for i in range(7))
    return xf.reshape(bsz, t, D_MODEL), stacked


def _zero_states(bsz):
    return (jnp.zeros((bsz, CONV_W - 1, CONV_CH), F32), jnp.zeros((bsz, LRU_W), F32),
            jnp.zeros((bsz, M_HEADS, M_HD, M_HD), F32), jnp.zeros((bsz, M_HEADS, M_HD), F32),
            jnp.zeros((bsz, M_HEADS), F32), jnp.zeros((bsz, 1, R_IN), F32),
            jnp.zeros((bsz, R_HEADS, R_HD, R_HD), F32))


def kernel(x_prompt, x_sample, state_conv, state_lru_h, state_mlstm_C, state_mlstm_n, state_mlstm_m, state_rwkv_shift, state_rwkv_S, ffn1_norm, ffn1_w_gate, ffn1_w_up, ffn1_w_down, mix_norm, w_in, conv_w, conv_b, lru_wa, lru_ba, lru_wx, lru_bx, lru_lambda, mlstm_if_bias, mlstm_norm, rwkv_mu, rwkv_w0, rwkv_w2, rwkv_a0, rwkv_a2, rwkv_g2, rwkv_k_k, rwkv_k_a, rwkv_r_k, rwkv_ln_w, rwkv_ln_b, w_branch, w_out, ffn2_norm, ffn2_w_gate, ffn2_w_up, ffn2_w_down, final_norm):
    p = dict(ffn1_norm=ffn1_norm, ffn1_w_gate=ffn1_w_gate, ffn1_w_up=ffn1_w_up, ffn1_w_down=ffn1_w_down,
             mix_norm=mix_norm, w_in=w_in, conv_w=conv_w, conv_b=conv_b,
             lru_wa=lru_wa, lru_ba=lru_ba, lru_wx=lru_wx, lru_bx=lru_bx, lru_lambda=lru_lambda,
             mlstm_if_bias=mlstm_if_bias, mlstm_norm=mlstm_norm,
             rwkv_mu=rwkv_mu, rwkv_w0=rwkv_w0, rwkv_w2=rwkv_w2, rwkv_a0=rwkv_a0, rwkv_a2=rwkv_a2,
             rwkv_g2=rwkv_g2, rwkv_k_k=rwkv_k_k, rwkv_k_a=rwkv_k_a, rwkv_r_k=rwkv_r_k,
             rwkv_ln_w=rwkv_ln_w, rwkv_ln_b=rwkv_ln_b, w_branch=w_branch, w_out=w_out,
             ffn2_norm=ffn2_norm, ffn2_w_gate=ffn2_w_gate, ffn2_w_up=ffn2_w_up, ffn2_w_down=ffn2_w_down)
    depth = w_in.shape[0]
    lw_list = [_prep_layer_weights(p, l) for l in range(depth)]
    states_p = [_zero_states(x_prompt.shape[0]) for _ in range(depth)]
    y_p, st_p = _run_group(x_prompt, states_p, p, lw_list, final_norm)
    states_s = [(state_conv[l], state_lru_h[l], state_mlstm_C[l], state_mlstm_n[l], state_mlstm_m[l],
                 state_rwkv_shift[l], state_rwkv_S[l]) for l in range(depth)]
    y_s, st_s = _run_group(x_sample, states_s, p, lw_list, final_norm)
    return (y_p, y_s) + st_p + st_s
```

```python
import functools

import jax
import jax.numpy as jnp
from jax import lax
from jax.experimental import pallas as pl
from jax.experimental.pallas import tpu as pltpu

F32 = jnp.float32
BF16 = jnp.bfloat16

V7X_LANES = 128
V7X_SUBLANES = 8
V7X_VMEM_BYTES = 64 * 1024 * 1024
VMEM_LIMIT = 56 * 1024 * 1024

D_MODEL = 1024
D_FF = 2816
CONV_W = 4
LRU_W = 1024
LRU_BLOCKS = 8
LRU_BD = LRU_W // LRU_BLOCKS
LRU_PAIR = 2 * LRU_BD
LRU_C = 8.0
M_HEADS = 4
M_HD = 256
R_HD = 64
R_HEADS = 16
R_PAIRS = R_HEADS // 2
R_W = 1024
R_IN = 3 * R_W + 256
CONV_CH = 3 * 1024
RMS_EPS = 1e-6
MH_EPS = 1e-6
RWKV_GN_EPS = 64e-5

ZC_CONV = 0
ZC_GATE = 24
ZC_MV = 48
ZC_MO = 56
ZC_RW = 64
Z_BLOCKS = 90
Z_W = Z_BLOCKS * V7X_LANES
Z_TN = 18 * V7X_LANES

FFN_TF = D_FF // 2
RWKV_CHUNK = 64
RWKV_GROUP = 128
RWKV_TT = 1024

HI = lax.Precision.HIGHEST


def _cparams(sem):
    return pltpu.CompilerParams(dimension_semantics=sem, vmem_limit_bytes=VMEM_LIMIT)


def _rms(x, g):
    return (x * lax.rsqrt(jnp.mean(x * x, axis=-1, keepdims=True) + RMS_EPS)) * g


def _softplus(x):
    return jnp.maximum(x, 0.0) + jnp.log1p(jnp.exp(-jnp.abs(x)))


def _sigmoid(x):
    return jax.nn.sigmoid(x)


def _dot(a, b, prec=None):
    return jnp.dot(a, b, preferred_element_type=F32, precision=prec)


def _dot_nt(a, b, prec=None):
    return lax.dot_general(a, b, (((1,), (1,)), ((), ())), preferred_element_type=F32, precision=prec)


def _dot_tn(a, b, prec=None):
    return lax.dot_general(a, b, (((0,), (0,)), ((), ())), preferred_element_type=F32, precision=prec)


def _shift_rows(x, d, row):
    return jnp.where(row >= d, pltpu.roll(x, d, axis=0), 0.0)


def _cumsum_rows(x):
    n = x.shape[0]
    row = lax.broadcasted_iota(jnp.int32, x.shape, 0)
    d = 1
    while d < n:
        x = x + _shift_rows(x, d, row)
        d *= 2
    return x


def _ffn_kernel(x_ref, g_ref, wg_ref, wu_ref, wd_ref, *rest, final_norm):
    if final_norm:
        fg_ref, o_ref, xn_ref, acc_ref = rest
    else:
        o_ref, xn_ref, acc_ref = rest
    k = pl.program_id(1)

    @pl.when(k == 0)
    def _():
        xn_ref[...] = _rms(x_ref[...], g_ref[...]).astype(BF16)
        acc_ref[...] = jnp.zeros_like(acc_ref)

    xn = xn_ref[...]
    hg = _dot(xn, wg_ref[...])
    hu = _dot(xn, wu_ref[...])
    h = (hg * _sigmoid(hg)) * hu
    acc_ref[...] += _dot(h.astype(BF16), wd_ref[...])

    @pl.when(k == pl.num_programs(1) - 1)
    def _():
        y = x_ref[...] + 0.5 * acc_ref[...]
        if final_norm:
            y = _rms(y, fg_ref[...])
        o_ref[...] = y


def _ffn(x, g, wg, wu, wd, final_g=None):
    m = x.shape[0]
    tm = min(512, m)
    nk = D_FF // FFN_TF
    in_specs = [
        pl.BlockSpec((tm, D_MODEL), lambda i, k: (i, 0)),
        pl.BlockSpec((1, D_MODEL), lambda i, k: (0, 0)),
        pl.BlockSpec((D_MODEL, FFN_TF), lambda i, k: (0, k)),
        pl.BlockSpec((D_MODEL, FFN_TF), lambda i, k: (0, k)),
        pl.BlockSpec((FFN_TF, D_MODEL), lambda i, k: (k, 0)),
    ]
    args = [x, g.reshape(1, D_MODEL), wg, wu, wd]
    if final_g is not None:
        in_specs.append(pl.BlockSpec((1, D_MODEL), lambda i, k: (0, 0)))
        args.append(final_g.reshape(1, D_MODEL))
    return pl.pallas_call(
        functools.partial(_ffn_kernel, final_norm=final_g is not None),
        grid=(m // tm, nk),
        in_specs=in_specs,
        out_specs=pl.BlockSpec((tm, D_MODEL), lambda i, k: (i, 0)),
        out_shape=jax.ShapeDtypeStruct((m, D_MODEL), F32),
        scratch_shapes=[pltpu.VMEM((tm, D_MODEL), BF16), pltpu.VMEM((tm, D_MODEL), F32)],
        compiler_params=_cparams(("parallel", "arbitrary")),
        name="ffn",
    )(*args)


def _inproj_kernel(x_ref, g_ref, w_ref, wif_ref, o_ref, oif_ref, xn_ref):
    @pl.when(pl.program_id(1) == 0)
    def _():
        xn = _rms(x_ref[...], g_ref[...]).astype(BF16)
        xn_ref[...] = xn
        oif_ref[...] = _dot(xn, wif_ref[...])

    o_ref[...] = _dot(xn_ref[...], w_ref[...]).astype(BF16)


def _inproj(x, g, w_all, w_if):
    m = x.shape[0]
    tm = min(1024, m)
    return pl.pallas_call(
        _inproj_kernel,
        grid=(m // tm, Z_W // Z_TN),
        in_specs=[
            pl.BlockSpec((tm, D_MODEL), lambda i, j: (i, 0)),
            pl.BlockSpec((1, D_MODEL), lambda i, j: (0, 0)),
            pl.BlockSpec((D_MODEL, Z_TN), lambda i, j: (0, j)),
            pl.BlockSpec((D_MODEL, V7X_LANES), lambda i, j: (0, 0)),
        ],
        out_specs=[
            pl.BlockSpec((tm, Z_TN), lambda i, j: (i, j)),
            pl.BlockSpec((tm, V7X_LANES), lambda i, j: (i, 0)),
        ],
        out_shape=[
            jax.ShapeDtypeStruct((m, Z_W), BF16),
            jax.ShapeDtypeStruct((m, V7X_LANES), F32),
        ],
        scratch_shapes=[pltpu.VMEM((tm, D_MODEL), BF16)],
        compiler_params=_cparams(("parallel", "arbitrary")),
        name="in_proj",
    )(x, g.reshape(1, D_MODEL), w_all, w_if)


def _conv_lru_kernel(z_ref, cprev_ref, h0_ref, cw_ref, cb_ref, w2_ref, bax_ref, lam_ref,
                     hl_ref, qk_ref, hlast_ref, xp_ref, a_ref, b_ref, hs_ref, hc_ref, *, tt):
    pad = V7X_SUBLANES

    @pl.when(pl.program_id(1) == 0)
    def _():
        xp_ref[pad - 3:pad, :] = cprev_ref[0]
        hc_ref[...] = h0_ref[0]

    x = z_ref[0].astype(F32)
    xp_ref[pad:pad + tt, :] = x
    c = cb_ref[...] + xp_ref[pad - 3:pad - 3 + tt, :] * cw_ref[0:1, :]
    c = c + xp_ref[pad - 2:pad - 2 + tt, :] * cw_ref[1:2, :]
    c = c + xp_ref[pad - 1:pad - 1 + tt, :] * cw_ref[2:3, :]
    c = c + x * cw_ref[3:4, :]
    xp_ref[pad - 3:pad, :] = x[tt - 3:tt, :]

    qk_ref[0, :, 0:1024] = (c[:, 1024:2048] * _sigmoid(c[:, 1024:2048])).astype(BF16)
    kk = c[:, 2048:3072]
    qk_ref[0, :, 1024:2048] = ((kk * _sigmoid(kk)) * (M_HD ** -0.5)).astype(BF16)

    sp = _softplus(-lam_ref[...])
    for p in range(LRU_W // LRU_PAIR):
        sl = slice(p * LRU_PAIR, (p + 1) * LRU_PAIR)
        xl = c[:, sl]
        gpre = _dot(xl.astype(BF16), w2_ref[p])
        r = _sigmoid(gpre[:, :LRU_PAIR] + bax_ref[0:1, sl])
        i = _sigmoid(gpre[:, LRU_PAIR:] + bax_ref[1:2, sl])
        log_a = (-LRU_C * r) * sp[:, sl]
        a = jnp.exp(log_a)
        mult = jnp.sqrt(-jnp.tanh(log_a) * (a * a + 1.0))
        a_ref[:, sl] = a
        b_ref[:, sl] = mult * (i * xl)

    def body(t, h):
        h = a_ref[pl.ds(t, 1), :] * h + b_ref[pl.ds(t, 1), :]
        hs_ref[pl.ds(t, 1), :] = h
        return h

    h = lax.fori_loop(0, tt, body, hc_ref[...], unroll=8)
    hc_ref[...] = h
    hlast_ref[0] = h
    hl_ref[0] = hs_ref[...].astype(BF16)


def _conv_lru(z, conv_prev, h0, cw, cb, w2, bax, lam):
    b, t, _ = z.shape
    tt = min(256, t)
    return pl.pallas_call(
        functools.partial(_conv_lru_kernel, tt=tt),
        grid=(b, t // tt),
        in_specs=[
            pl.BlockSpec((1, tt, CONV_CH), lambda i, j: (i, j, ZC_CONV * V7X_LANES // CONV_CH)),
            pl.BlockSpec((1, CONV_W - 1, CONV_CH), lambda i, j: (i, 0, 0)),
            pl.BlockSpec((1, 1, LRU_W), lambda i, j: (i, 0, 0)),
            pl.BlockSpec((CONV_W, CONV_CH), lambda i, j: (0, 0)),
            pl.BlockSpec((1, CONV_CH), lambda i, j: (0, 0)),
            pl.BlockSpec((LRU_W // LRU_PAIR, LRU_PAIR, 2 * LRU_PAIR), lambda i, j: (0, 0, 0)),
            pl.BlockSpec((2, LRU_W), lambda i, j: (0, 0)),
            pl.BlockSpec((1, LRU_W), lambda i, j: (0, 0)),
        ],
        out_specs=[
            pl.BlockSpec((1, tt, LRU_W), lambda i, j: (i, j, 0)),
            pl.BlockSpec((1, tt, 2048), lambda i, j: (i, j, 0)),
            pl.BlockSpec((1, 1, LRU_W), lambda i, j: (i, 0, 0)),
        ],
        out_shape=[
            jax.ShapeDtypeStruct((b, t, LRU_W), BF16),
            jax.ShapeDtypeStruct((b, t, 2048), BF16),
            jax.ShapeDtypeStruct((b, 1, LRU_W), F32),
        ],
        scratch_shapes=[
            pltpu.VMEM((tt + V7X_SUBLANES, CONV_CH), F32),
            pltpu.VMEM((tt, LRU_W), F32),
            pltpu.VMEM((tt, LRU_W), F32),
            pltpu.VMEM((tt, LRU_W), F32),
            pltpu.VMEM((1, LRU_W), F32),
        ],
        compiler_params=_cparams(("parallel", "arbitrary")),
        name="conv_lru",
    )(z, conv_prev, h0.reshape(b, 1, LRU_W), cw, cb.reshape(1, CONV_CH), w2, bax, lam.reshape(1, LRU_W))


def _mlstm_kernel(qk_ref, v_ref, o_ref, if_ref, c0_ref, n0_ref, m0_ref, ifb_ref, nw_ref,
                  om_ref, c_ref, n_ref, m_ref, *, lc):
    @pl.when(pl.program_id(1) == 0)
    def _():
        c_ref[...] = c0_ref[...]
        n_ref[...] = n0_ref[...]
        m_ref[...] = m0_ref[...]

    gif = if_ref[0] + ifb_ref[...]
    lf = jnp.minimum(gif, 0.0) - jnp.log1p(jnp.exp(-jnp.abs(gif)))
    cum = _cumsum_rows(lf)
    src = gif - pltpu.roll(cum, V7X_LANES - M_HEADS, axis=1)
    src_t = src.T if lc % V7X_LANES == 0 else None
    lane = lax.broadcasted_iota(jnp.int32, (lc, V7X_LANES), 1)
    tpos = lax.broadcasted_iota(jnp.int32, (lc, lc), 0)
    spos = lax.broadcasted_iota(jnp.int32, (lc, lc), 1)
    causal = spos <= tpos
    m_prev_all = m_ref[0]
    mlane = lax.broadcasted_iota(jnp.int32, (1, M_HEADS), 1)
    m_new_all = m_prev_all

    for h in range(M_HEADS):
        sl = slice(h * M_HD, (h + 1) * M_HD)
        q = qk_ref[0, :, sl]
        k = qk_ref[0, :, M_HEADS * M_HD + h * M_HD:M_HEADS * M_HD + (h + 1) * M_HD]
        vb = v_ref[0, :, sl]
        bc = cum[:, M_HEADS + h:M_HEADS + h + 1]
        ig = gif[:, h:h + 1]
        m_prev = m_prev_all[:, h:h + 1]
        if src_t is not None:
            rowv = src_t[h:h + 1, :]
        else:
            rowv = _dot_nt(jnp.where(lane == h, 1.0, 0.0), src, HI)
        dmat = jnp.where(causal, bc + rowv, -jnp.inf)
        inter = bc + m_prev
        m_t = jnp.maximum(inter, jnp.max(dmat, axis=1, keepdims=True))
        w_intra = jnp.exp(dmat - m_t)
        w_inter = jnp.exp(inter - m_t)
        s = _dot_nt(q, k) * w_intra
        c_prev = c_ref[0, h]
        n_prev = n_ref[0, h:h + 1, :]
        num = _dot(s.astype(BF16), vb) + w_inter * _dot_nt(q, c_prev.astype(BF16))
        qf = q.astype(F32)
        den = jnp.sum(s, axis=1, keepdims=True) + w_inter * jnp.sum(qf * n_prev, axis=1, keepdims=True)
        hh = num / jnp.maximum(jnp.abs(den), jnp.exp(-m_t))
        m_new = m_t[lc - 1:lc, :]
        bl = bc[lc - 1:lc, :]
        g_state = jnp.exp(bl + m_prev - m_new)
        g_src = jnp.exp(bl - bc + ig - m_new)
        c_ref[0, h] = g_state * c_prev + _dot_tn((g_src * vb.astype(F32)).astype(BF16), k)
        n_ref[0, h:h + 1, :] = g_state * n_prev + jnp.sum(g_src * k.astype(F32), axis=0, keepdims=True)
        m_new_all = jnp.where(mlane == h, m_new, m_new_all)
        d = hh - jnp.mean(hh, axis=1, keepdims=True)
        hn = d * lax.rsqrt(jnp.mean(d * d, axis=1, keepdims=True) + MH_EPS)
        om_ref[0, :, sl] = (_sigmoid(o_ref[0, :, sl].astype(F32)) * (hn * nw_ref[:, sl])).astype(BF16)

    m_ref[0] = m_new_all


def _mlstm(qk, z, zif, c0, n0, m0, if_bias, norm_w):
    b, t, _ = z.shape
    lc = min(256, t)
    w = M_HEADS * M_HD
    return pl.pallas_call(
        functools.partial(_mlstm_kernel, lc=lc),
        grid=(b, t // lc),
        in_specs=[
            pl.BlockSpec((1, lc, 2 * w), lambda i, j: (i, j, 0)),
            pl.BlockSpec((1, lc, w), lambda i, j: (i, j, ZC_MV * V7X_LANES // w)),
            pl.BlockSpec((1, lc, w), lambda i, j: (i, j, ZC_MO * V7X_LANES // w)),
            pl.BlockSpec((1, lc, V7X_LANES), lambda i, j: (i, j, 0)),
            pl.BlockSpec((1, M_HEADS, M_HD, M_HD), lambda i, j: (i, 0, 0, 0)),
            pl.BlockSpec((1, M_HEADS, M_HD), lambda i, j: (i, 0, 0)),
            pl.BlockSpec((1, 1, M_HEADS), lambda i, j: (i, 0, 0)),
            pl.BlockSpec((1, V7X_LANES), lambda i, j: (0, 0)),
            pl.BlockSpec((1, w), lambda i, j: (0, 0)),
        ],
        out_specs=[
            pl.BlockSpec((1, lc, w), lambda i, j: (i, j, 0)),
            pl.BlockSpec((1, M_HEADS, M_HD, M_HD), lambda i, j: (i, 0, 0, 0)),
            pl.BlockSpec((1, M_HEADS, M_HD), lambda i, j: (i, 0, 0)),
            pl.BlockSpec((1, 1, M_HEADS), lambda i, j: (i, 0, 0)),
        ],
        out_shape=[
            jax.ShapeDtypeStruct((b, t, w), BF16),
            jax.ShapeDtypeStruct((b, M_HEADS, M_HD, M_HD), F32),
            jax.ShapeDtypeStruct((b, M_HEADS, M_HD), F32),
            jax.ShapeDtypeStruct((b, 1, M_HEADS), F32),
        ],
        compiler_params=_cparams(("parallel", "arbitrary")),
        name="mlstm",
    )(qk, z, z, zif, c0, n0, m0.reshape(b, 1, M_HEADS), if_bias, norm_w.reshape(1, w))


def _rwkv_mix_kernel(zr_ref, zk_ref, zv_ref, zwa_ref, zgd_ref, sh_ref, mu_ref, w0_ref, w2_ref, a0_ref, a2_ref, g2_ref,
                     kk_ref, ka_ref, s0_ref, lnw_ref, lnb_ref, rk_ref, e2_ref,
                     o_ref, s_ref, y_ref, car_ref, *, tt, lc, rg):
    @pl.when(pl.program_id(2) == 0)
    def _():
        s_ref[...] = s0_ref[...]
        car_ref[...] = sh_ref[0, 0]

    def seg(x):
        hi = x.astype(BF16)
        lo = (x - hi.astype(F32)).astype(BF16)
        return _dot(hi, e2_ref[...]) + _dot(lo, e2_ref[...])

    first_row = lax.broadcasted_iota(jnp.int32, (tt, 1), 0) == 0

    def shifted(ref, i):
        zp = ref[0].astype(F32)
        prev = jnp.where(first_row, car_ref[i:i + 1, :], pltpu.roll(zp, 1, axis=0))
        car_ref[i:i + 1, :] = zp[tt - 1:tt, :]
        return zp + (prev - zp) * mu_ref[0, i:i + 1, :]

    r_all = shifted(zr_ref, 0)
    k_in = shifted(zk_ref, 1)
    v_all = shifted(zv_ref, 2)
    xwa = shifted(zwa_ref, 3)
    xgd = shifted(zgd_ref, 4)
    w_log = -_softplus(-(w0_ref[0] + _dot(jnp.tanh(xwa).astype(BF16), w2_ref[...]))) - 0.5
    lw_all = -jnp.exp(w_log)
    a_gate = _sigmoid(a0_ref[0] + _dot(xwa.astype(BF16), a2_ref[...]))
    g_all = _dot(_sigmoid(xgd).astype(BF16), g2_ref[...])
    kk = k_in * kk_ref[0]
    kk = kk * lax.rsqrt(jnp.maximum(seg(kk * kk), 1e-24))
    k_all = k_in * (1.0 + (a_gate - 1.0) * ka_ref[0])
    a_all = -kk
    b_all = kk * a_gate

    nb = rg // lc
    trow = lax.broadcasted_iota(jnp.int32, (rg, 2 * rg), 0)
    scol = lax.broadcasted_iota(jnp.int32, (rg, 2 * rg), 1)
    scol = jnp.where(scol >= rg, scol - rg, scol)
    dist = trow - scol
    tin = trow & (lc - 1)
    smask = (dist > 0) & (dist <= tin)
    imask = (dist >= 0) & (dist <= tin)
    crow = lax.broadcasted_iota(jnp.int32, (rg, V7X_LANES), 0) & (lc - 1)
    xrow = lax.broadcasted_iota(jnp.int32, (rg, nb * R_HD), 0)
    xcol = lax.broadcasted_iota(jnp.int32, (rg, nb * R_HD), 1)
    xmask = (xrow >> (lc.bit_length() - 1)) == (xcol >> (R_HD.bit_length() - 1))
    n_sq = max(lc.bit_length() - 1, 0)

    groups = range(tt // rg)
    chains = [(g, h) for g in groups for h in range(2)]
    hs = lambda h: slice(h * R_HD, (h + 1) * R_HD)
    ew = {}
    for g in groups:
        rows = slice(g * rg, (g + 1) * rg)
        r = r_all[rows]
        lw = lw_all[rows]
        k = k_all[rows]
        b = b_all[rows]
        cum = lw
        d = 1
        while d < lc:
            cum = cum + jnp.where(crow >= d, pltpu.roll(cum, d, axis=0), 0.0)
            d *= 2
        ends = [cum[c * lc + lc - 1:c * lc + lc, :] for c in range(nb)]
        cum_end = ends[0] if nb == 1 else jnp.concatenate(
            [jnp.broadcast_to(e, (lc, V7X_LANES)) for e in ends], axis=0)
        e_neg = jnp.exp(-cum)
        at = a_all[rows] * jnp.exp(cum - lw)
        rt = r * jnp.exp(cum)
        e_end = jnp.exp(cum_end - cum)
        ew[g] = dict(at=at, rt=rt, atb=at.astype(BF16), rtb=rt.astype(BF16),
                     rhs=jnp.concatenate([b * e_neg, k * e_neg], axis=0).astype(BF16),
                     be=(b * e_end).astype(BF16), ke=(k * e_end).astype(BF16),
                     vb=v_all[rows].astype(BF16), w_end=[jnp.exp(e) for e in ends])

    def expand(x):
        if nb == 1:
            return x
        return jnp.where(xmask, jnp.concatenate([x] * nb, axis=1), jnp.zeros((), x.dtype))

    ga = {c: jnp.where(smask, _dot_nt(ew[c[0]]["atb"][:, hs(c[1])], ew[c[0]]["rhs"][:, hs(c[1])]), 0.0) for c in chains}
    gr = {c: jnp.where(imask, _dot_nt(ew[c[0]]["rtb"][:, hs(c[1])], ew[c[0]]["rhs"][:, hs(c[1])]), 0.0) for c in chains}
    gva = {c: _dot(ga[c][:, rg:].astype(BF16), ew[c[0]]["vb"][:, hs(c[1])]) for c in chains}
    gvr = {c: _dot(gr[c][:, rg:].astype(BF16), ew[c[0]]["vb"][:, hs(c[1])]) for c in chains}
    x = {c: jnp.concatenate([ew[c[0]]["at"][:, hs(c[1])], gva[c]], axis=1) for c in chains}
    pw = {c: ga[c][:, :rg] for c in chains}
    for i in range(n_sq):
        pwb = {c: pw[c].astype(BF16) for c in chains}
        x = {c: x[c] + _dot(pwb[c], x[c].astype(BF16)) for c in chains}
        if i + 1 < n_sq:
            pw = {c: _dot(pwb[c], pwb[c]) for c in chains}
    xb = {c: x[c].astype(BF16) for c in chains}
    pq2 = {c: jnp.concatenate([ew[c[0]]["rt"][:, hs(c[1])], gvr[c]], axis=1)
           + _dot(gr[c][:, :rg].astype(BF16), xb[c]) for c in chains}
    xe = {c: _dot_tn(xb[c], expand(ew[c[0]]["be"][:, hs(c[1])])) for c in chains}
    vk = {c: _dot_tn(ew[c[0]]["vb"][:, hs(c[1])], expand(ew[c[0]]["ke"][:, hs(c[1])])) for c in chains}

    s = [s_ref[0, 0], s_ref[0, 1]]
    for g in groups:
        for c in range(nb):
            ys = []
            rows = slice(c * lc, (c + 1) * lc)
            cols = slice(c * R_HD, (c + 1) * R_HD)
            for h in range(2):
                sb = s[h].astype(BF16)
                ys.append(_dot_nt(pq2[g, h][rows, :R_HD].astype(BF16), sb) + pq2[g, h][rows, R_HD:])
                s[h] = (s[h] * ew[g]["w_end"][c][:, hs(h)] + _dot(sb, xe[g, h][:R_HD, cols].astype(BF16))
                        + (xe[g, h][R_HD:, cols] + vk[g, h][:, cols]))
            y_ref[g * rg + c * lc:g * rg + (c + 1) * lc, :] = jnp.concatenate(ys, axis=1)
    s_ref[0, 0] = s[0]
    s_ref[0, 1] = s[1]

    y = y_ref[...]
    d = y - seg(y) * (1.0 / R_HD)
    var = seg(d * d) * (1.0 / R_HD)
    yn = (d * lax.rsqrt(var + RWKV_GN_EPS)) * lnw_ref[0] + lnb_ref[0]
    yn = yn + seg((r_all * k_all) * rk_ref[0]) * v_all
    o_ref[0] = (yn * g_all).astype(BF16)


def _rwkv_mix(z, shift_prev, s0, mu, w0, w2p, a0, a2p, g2, k_k, k_a, ln_w, ln_b, r_k, e2):
    bsz, t, _ = z.shape
    tt = min(RWKV_TT, t)
    lc = min(RWKV_CHUNK, t)
    rg = min(RWKV_GROUP, t)
    zpair = lambda blk: pl.BlockSpec((1, tt, V7X_LANES), lambda i, p, j: (i, j, blk + p))
    zshared = lambda blk: pl.BlockSpec((1, tt, V7X_LANES), lambda i, p, j: (i, j, blk))
    par_spec = pl.BlockSpec((1, 1, V7X_LANES), lambda i, p, j: (p, 0, 0))
    lora_spec = pl.BlockSpec((V7X_LANES, V7X_LANES), lambda i, p, j: (0, p))
    st_spec = pl.BlockSpec((1, 2, R_HD, R_HD), lambda i, p, j: (i, p, 0, 0))
    pair = lambda x: x.reshape(R_PAIRS, 1, V7X_LANES)

    def pieces(x):
        lead = x.shape[:-1]
        rkv = x[..., :3 * R_W].reshape(lead + (3, R_PAIRS, V7X_LANES))
        rkv = jnp.moveaxis(rkv, -3, -2)
        lora = jnp.broadcast_to(x[..., 3 * R_W:].reshape(lead + (1, 2, V7X_LANES)), lead + (R_PAIRS, 2, V7X_LANES))
        return jnp.concatenate([rkv, lora], axis=-2)

    return pl.pallas_call(
        functools.partial(_rwkv_mix_kernel, tt=tt, lc=lc, rg=rg),
        grid=(bsz, R_PAIRS, t // tt),
        in_specs=[
            zpair(ZC_RW), zpair(ZC_RW + 8), zpair(ZC_RW + 16), zshared(ZC_RW + 24), zshared(ZC_RW + 25),
            pl.BlockSpec((1, 1, 5, V7X_LANES), lambda i, p, j: (i, p, 0, 0)),
            pl.BlockSpec((1, 5, V7X_LANES), lambda i, p, j: (p, 0, 0)),
            par_spec, lora_spec, par_spec, lora_spec, lora_spec, par_spec, par_spec,
            st_spec, par_spec, par_spec, par_spec,
            pl.BlockSpec((V7X_LANES, V7X_LANES), lambda i, p, j: (0, 0)),
        ],
        out_specs=[
            pl.BlockSpec((1, tt, V7X_LANES), lambda i, p, j: (i, j, p)),
            st_spec,
        ],
        out_shape=[
            jax.ShapeDtypeStruct((bsz, t, R_W), BF16),
            jax.ShapeDtypeStruct((bsz, R_HEADS, R_HD, R_HD), F32),
        ],
        scratch_shapes=[pltpu.VMEM((tt, V7X_LANES), F32), pltpu.VMEM((5, V7X_LANES), F32)],
        compiler_params=_cparams(("parallel", "parallel", "arbitrary")),
        name="rwkv_mix",
    )(z, z, z, z, z, pieces(shift_prev.reshape(bsz, R_IN)), pieces(mu), pair(w0), w2p, pair(a0), a2p, g2,
      pair(k_k), pair(k_a), s0, pair(ln_w), pair(ln_b), pair(r_k), e2)


def _merge_kernel(x_ref, zg_ref, hl_ref, om_ref, or_ref, wb_ref, wo_ref, o_ref):
    acc = None
    for gidx, br in enumerate((hl_ref, om_ref, or_ref)):
        sl = slice(gidx * D_MODEL, (gidx + 1) * D_MODEL)
        term = _sigmoid(zg_ref[:, sl].astype(F32)) * _dot(br[...], wb_ref[gidx])
        acc = term if acc is None else acc + term
    o_ref[...] = x_ref[...] + _dot(acc.astype(BF16), wo_ref[...])


def _merge(x, z2d, hl, om, orw, wb, wo):
    m = x.shape[0]
    tm = min(512, m)
    tok = pl.BlockSpec((tm, D_MODEL), lambda i: (i, 0))
    return pl.pallas_call(
        _merge_kernel,
        grid=(m // tm,),
        in_specs=[
            tok,
            pl.BlockSpec((tm, 3 * D_MODEL), lambda i: (i, ZC_GATE * V7X_LANES // (3 * D_MODEL))),
            tok, tok, tok,
            pl.BlockSpec((3, D_MODEL, D_MODEL), lambda i: (0, 0, 0)),
            pl.BlockSpec((D_MODEL, D_MODEL), lambda i: (0, 0)),
        ],
        out_specs=tok,
        out_shape=jax.ShapeDtypeStruct((m, D_MODEL), F32),
        compiler_params=_cparams(("parallel",)),
        name="merge",
    )(x, z2d, hl, om, orw, wb, wo)


def _prep_layer_weights(p, l):
    w_in = p["w_in"][l]
    c0, c1, c2, c3, c4 = 3072, 4096, 5120, 5128, 8456
    w_if = jnp.pad(w_in[:, c2:c3], ((0, 0), (0, V7X_LANES - 2 * M_HEADS))).astype(BF16)
    w_all = jnp.concatenate([w_in[:, :c0], w_in[:, c4:], w_in[:, c0:c1], w_in[:, c1:c2], w_in[:, c3:c4]],
                            axis=1).astype(BF16)
    def pairs(w):
        z = jnp.zeros((LRU_BD, LRU_BD), F32)
        return jnp.stack([jnp.block([[w[2 * i], z], [z, w[2 * i + 1]]]) for i in range(LRU_BLOCKS // 2)])
    w2 = jnp.concatenate([pairs(p["lru_wa"][l]), pairs(p["lru_wx"][l])], axis=2).astype(BF16)
    bax = jnp.stack([p["lru_ba"][l], p["lru_bx"][l]])
    if_bias = jnp.pad(p["mlstm_if_bias"][l], (0, V7X_LANES - 2 * M_HEADS)).reshape(1, V7X_LANES)
    zpad = jnp.zeros((64, R_W), F32)
    w2p = jnp.concatenate([p["rwkv_w2"][l], zpad], axis=0).astype(BF16)
    a2p = jnp.concatenate([zpad, p["rwkv_a2"][l]], axis=0).astype(BF16)
    return dict(
        w_all=w_all, w_if=w_if, w2=w2, bax=bax, if_bias=if_bias, w2p=w2p, a2p=a2p,
        g2=p["rwkv_g2"][l].astype(BF16),
        wb=p["w_branch"][l].astype(BF16), wo=p["w_out"][l].astype(BF16),
        f1=(p["ffn1_w_gate"][l].astype(BF16), p["ffn1_w_up"][l].astype(BF16), p["ffn1_w_down"][l].astype(BF16)),
        f2=(p["ffn2_w_gate"][l].astype(BF16), p["ffn2_w_up"][l].astype(BF16), p["ffn2_w_down"][l].astype(BF16)),
    )


def _segment_matrix():
    half = jnp.arange(V7X_LANES) // R_HD
    return (half[:, None] == half[None, :]).astype(BF16)


def _run_group(x, states, p, lw_list, final_norm):
    bsz, t, _ = x.shape
    m = bsz * t
    e2 = _segment_matrix()
    xf = x.reshape(m, D_MODEL)
    new_states = []
    depth = p["w_in"].shape[0]
    for l in range(depth):
        lw = lw_list[l]
        conv_prev, lru_h, m_c, m_n, m_m, shift_prev, rwkv_s = states[l]
        xf = _ffn(xf, p["ffn1_norm"][l], *lw["f1"])
        z2d, zif = _inproj(xf, p["mix_norm"][l], lw["w_all"], lw["w_if"])
        z = z2d.reshape(bsz, t, Z_W)
        hl, qk, h_last = _conv_lru(z, conv_prev, lru_h, p["conv_w"][l], p["conv_b"][l], lw["w2"], lw["bax"],
                                   p["lru_lambda"][l])
        om, c_new, n_new, m_new = _mlstm(qk, z, zif.reshape(bsz, t, V7X_LANES), m_c, m_n, m_m, lw["if_bias"],
                                         p["mlstm_norm"][l])
        orw, s_new = _rwkv_mix(z, shift_prev, rwkv_s, p["rwkv_mu"][l], p["rwkv_w0"][l], lw["w2p"], p["rwkv_a0"][l],
                               lw["a2p"], lw["g2"], p["rwkv_k_k"][l], p["rwkv_k_a"][l], p["rwkv_ln_w"][l],
                               p["rwkv_ln_b"][l], p["rwkv_r_k"][l].reshape(-1), e2)
        xf = _merge(xf, z2d, hl.reshape(m, D_MODEL), om.reshape(m, D_MODEL), orw.reshape(m, D_MODEL),
                    lw["wb"], lw["wo"])
        xf = _ffn(xf, p["ffn2_norm"][l], *lw["f2"], final_g=final_norm if l == depth - 1 else None)
        conv_new = z[:, t - (CONV_W - 1):, ZC_CONV * V7X_LANES:ZC_CONV * V7X_LANES + CONV_CH].astype(F32)
        shift_new = z[:, t - 1:, ZC_RW * V7X_LANES:ZC_RW * V7X_LANES + R_IN].astype(F32)
        new_states.append((conv_new, h_last.reshape(bsz, LRU_W), c_new, n_new, m_new.reshape(bsz, M_HEADS),
                           shift_new, s_new))
    stacked = tuple(jnp.stack([st[i] for st in new_states]) for i in range(7))
    return xf.reshape(bsz, t, D_MODEL), stacked


def _zero_states(bsz):
    return (jnp.zeros((bsz, CONV_W - 1, CONV_CH), F32), jnp.zeros((bsz, LRU_W), F32),
            jnp.zeros((bsz, M_HEADS, M_HD, M_HD), F32), jnp.zeros((bsz, M_HEADS, M_HD), F32),
            jnp.zeros((bsz, M_HEADS), F32), jnp.zeros((bsz, 1, R_IN), F32),
            jnp.zeros((bsz, R_HEADS, R_HD, R_HD), F32))


def kernel(x_prompt, x_sample, state_conv, state_lru_h, state_mlstm_C, state_mlstm_n, state_mlstm_m, state_rwkv_shift, state_rwkv_S, ffn1_norm, ffn1_w_gate, ffn1_w_up, ffn1_w_down, mix_norm, w_in, conv_w, conv_b, lru_wa, lru_ba, lru_wx, lru_bx, lru_lambda, mlstm_if_bias, mlstm_norm, rwkv_mu, rwkv_w0, rwkv_w2, rwkv_a0, rwkv_a2, rwkv_g2, rwkv_k_k, rwkv_k_a, rwkv_r_k, rwkv_ln_w, rwkv_ln_b, w_branch, w_out, ffn2_norm, ffn2_w_gate, ffn2_w_up, ffn2_w_down, final_norm):
    p = dict(ffn1_norm=ffn1_norm, ffn1_w_gate=ffn1_w_gate, ffn1_w_up=ffn1_w_up, ffn1_w_down=ffn1_w_down,
             mix_norm=mix_norm, w_in=w_in, conv_w=conv_w, conv_b=conv_b,
             lru_wa=lru_wa, lru_ba=lru_ba, lru_wx=lru_wx, lru_bx=lru_bx, lru_lambda=lru_lambda,
             mlstm_if_bias=mlstm_if_bias, mlstm_norm=mlstm_norm,
             rwkv_mu=rwkv_mu, rwkv_w0=rwkv_w0, rwkv_w2=rwkv_w2, rwkv_a0=rwkv_a0, rwkv_a2=rwkv_a2,
             rwkv_g2=rwkv_g2, rwkv_k_k=rwkv_k_k, rwkv_k_a=rwkv_k_a, rwkv_r_k=rwkv_r_k,
             rwkv_ln_w=rwkv_ln_w, rwkv_ln_b=rwkv_ln_b, w_branch=w_branch, w_out=w_out,
             ffn2_norm=ffn2_norm, ffn2_w_gate=ffn2_w_gate, ffn2_w_up=ffn2_w_up, ffn2_w_down=ffn2_w_down)
    depth = w_in.shape[0]
    lw_list = [_prep_layer_weights(p, l) for l in range(depth)]
    states_p = [_zero_states(x_prompt.shape[0]) for _ in range(depth)]
    y_p, st_p = _run_group(x_prompt, states_p, p, lw_list, final_norm)
    states_s = [(state_conv[l], state_lru_h[l], state_mlstm_C[l], state_mlstm_n[l], state_mlstm_m[l],
                 state_rwkv_shift[l], state_rwkv_S[l]) for l in range(depth)]
    y_s, st_s = _run_group(x_sample, states_s, p, lw_list, final_norm)
    return (y_p, y_s) + st_p + st_s
```

```python
import functools

import jax
import jax.numpy as jnp
from jax import lax
from jax.experimental import pallas as pl
from jax.experimental.pallas import tpu as pltpu

F32 = jnp.float32
BF16 = jnp.bfloat16

V7X_LANES = 128
V7X_SUBLANES = 8
V7X_VMEM_BYTES = 64 * 1024 * 1024
VMEM_LIMIT = 56 * 1024 * 1024

D_MODEL = 1024
D_FF = 2816
CONV_W = 4
LRU_W = 1024
LRU_BLOCKS = 8
LRU_BD = LRU_W // LRU_BLOCKS
LRU_PAIR = 2 * LRU_BD
LRU_C = 8.0
M_HEADS = 4
M_HD = 256
R_HD = 64
R_HEADS = 16
R_PAIRS = R_HEADS // 2
R_W = 1024
R_IN = 3 * R_W + 256
CONV_CH = 3 * 1024
RMS_EPS = 1e-6
MH_EPS = 1e-6
RWKV_GN_EPS = 64e-5

ZC_CONV = 0
ZC_GATE = 24
ZC_MV = 48
ZC_MO = 56
ZC_RW = 64
Z_BLOCKS = 90
Z_W = Z_BLOCKS * V7X_LANES
Z_TN = 18 * V7X_LANES

FFN_TF = D_FF // 2
FFN_SUB = ((0, 768), (768, FFN_TF))
RWKV_CHUNK = 64
RWKV_GROUP = 128
RWKV_TT = 128
RWKV_PAIRS_PER_STEP = 8

HI = lax.Precision.HIGHEST


def _cparams(sem):
    return pltpu.CompilerParams(dimension_semantics=sem, vmem_limit_bytes=VMEM_LIMIT)


def _rms(x, g):
    return (x * lax.rsqrt(jnp.mean(x * x, axis=-1, keepdims=True) + RMS_EPS)) * g


def _softplus(x):
    return jnp.maximum(x, 0.0) + jnp.log1p(jnp.exp(-jnp.abs(x)))


def _sigmoid(x):
    return jax.nn.sigmoid(x)


def _dot(a, b, prec=None):
    return jnp.dot(a, b, preferred_element_type=F32, precision=prec)


def _dot_nt(a, b, prec=None):
    return lax.dot_general(a, b, (((1,), (1,)), ((), ())), preferred_element_type=F32, precision=prec)


def _dot_tn(a, b, prec=None):
    return lax.dot_general(a, b, (((0,), (0,)), ((), ())), preferred_element_type=F32, precision=prec)


def _shift_rows(x, d, row):
    return jnp.where(row >= d, pltpu.roll(x, d, axis=0), 0.0)


def _cumsum_rows(x):
    n = x.shape[0]
    row = lax.broadcasted_iota(jnp.int32, x.shape, 0)
    d = 1
    while d < n:
        x = x + _shift_rows(x, d, row)
        d *= 2
    return x


def _ffn_kernel(x_ref, g_ref, wg_ref, wu_ref, wd_ref, *rest, final_norm):
    if final_norm:
        fg_ref, o_ref, xn_ref = rest
    else:
        o_ref, xn_ref = rest
    k = pl.program_id(1)

    @pl.when(k == 0)
    def _():
        xn_ref[...] = _rms(x_ref[...], g_ref[...]).astype(BF16)

    xn = xn_ref[...]
    part = None
    for c0, c1 in FFN_SUB:
        hg = _dot(xn, wg_ref[:, c0:c1])
        hu = _dot(xn, wu_ref[:, c0:c1])
        h = (hg * _sigmoid(hg)) * hu
        d = _dot(h.astype(BF16), wd_ref[c0:c1, :])
        part = d if part is None else part + d

    @pl.when(k == 0)
    def _():
        o_ref[...] = part

    @pl.when(k == 1)
    def _():
        y = x_ref[...] + 0.5 * (o_ref[...] + part)
        if final_norm:
            y = _rms(y, fg_ref[...])
        o_ref[...] = y


def _ffn(x, g, wg, wu, wd, final_g=None):
    m = x.shape[0]
    tm = min(1024, m)
    nk = D_FF // FFN_TF
    assert nk == 2
    in_specs = [
        pl.BlockSpec((tm, D_MODEL), lambda i, k: (i, 0)),
        pl.BlockSpec((1, D_MODEL), lambda i, k: (0, 0)),
        pl.BlockSpec((D_MODEL, FFN_TF), lambda i, k: (0, k)),
        pl.BlockSpec((D_MODEL, FFN_TF), lambda i, k: (0, k)),
        pl.BlockSpec((FFN_TF, D_MODEL), lambda i, k: (k, 0)),
    ]
    args = [x, g.reshape(1, D_MODEL), wg, wu, wd]
    if final_g is not None:
        in_specs.append(pl.BlockSpec((1, D_MODEL), lambda i, k: (0, 0)))
        args.append(final_g.reshape(1, D_MODEL))
    return pl.pallas_call(
        functools.partial(_ffn_kernel, final_norm=final_g is not None),
        grid=(m // tm, nk),
        in_specs=in_specs,
        out_specs=pl.BlockSpec((tm, D_MODEL), lambda i, k: (i, 0)),
        out_shape=jax.ShapeDtypeStruct((m, D_MODEL), F32),
        scratch_shapes=[pltpu.VMEM((tm, D_MODEL), BF16)],
        compiler_params=_cparams(("parallel", "arbitrary")),
        name="ffn",
    )(*args)


def _inproj_kernel(x_ref, g_ref, w_ref, wif_ref, o_ref, oif_ref, xn_ref):
    @pl.when(pl.program_id(1) == 0)
    def _():
        xn = _rms(x_ref[...], g_ref[...]).astype(BF16)
        xn_ref[...] = xn
        oif_ref[...] = _dot(xn, wif_ref[...])

    o_ref[...] = _dot(xn_ref[...], w_ref[...]).astype(BF16)


def _inproj(x, g, w_all, w_if):
    m = x.shape[0]
    tm = min(1024, m)
    return pl.pallas_call(
        _inproj_kernel,
        grid=(m // tm, Z_W // Z_TN),
        in_specs=[
            pl.BlockSpec((tm, D_MODEL), lambda i, j: (i, 0)),
            pl.BlockSpec((1, D_MODEL), lambda i, j: (0, 0)),
            pl.BlockSpec((D_MODEL, Z_TN), lambda i, j: (0, j)),
            pl.BlockSpec((D_MODEL, V7X_LANES), lambda i, j: (0, 0)),
        ],
        out_specs=[
            pl.BlockSpec((tm, Z_TN), lambda i, j: (i, j)),
            pl.BlockSpec((tm, V7X_LANES), lambda i, j: (i, 0)),
        ],
        out_shape=[
            jax.ShapeDtypeStruct((m, Z_W), BF16),
            jax.ShapeDtypeStruct((m, V7X_LANES), F32),
        ],
        scratch_shapes=[pltpu.VMEM((tm, D_MODEL), BF16)],
        compiler_params=_cparams(("parallel", "arbitrary")),
        name="in_proj",
    )(x, g.reshape(1, D_MODEL), w_all, w_if)


def _conv_lru_kernel(z_ref, cprev_ref, h0_ref, cw_ref, cb_ref, w2_ref, bax_ref, lam_ref,
                     hl_ref, qk_ref, hlast_ref, xp_ref, a_ref, b_ref, hs_ref, hc_ref, *, tt):
    pad = V7X_SUBLANES

    @pl.when(pl.program_id(1) == 0)
    def _():
        xp_ref[pad - 3:pad, :] = cprev_ref[0]
        hc_ref[...] = h0_ref[0]

    x = z_ref[0].astype(F32)
    xp_ref[pad:pad + tt, :] = x
    c = cb_ref[...] + xp_ref[pad - 3:pad - 3 + tt, :] * cw_ref[0:1, :]
    c = c + xp_ref[pad - 2:pad - 2 + tt, :] * cw_ref[1:2, :]
    c = c + xp_ref[pad - 1:pad - 1 + tt, :] * cw_ref[2:3, :]
    c = c + x * cw_ref[3:4, :]
    xp_ref[pad - 3:pad, :] = x[tt - 3:tt, :]

    qk_ref[0, :, 0:1024] = (c[:, 1024:2048] * _sigmoid(c[:, 1024:2048])).astype(BF16)
    kk = c[:, 2048:3072]
    qk_ref[0, :, 1024:2048] = ((kk * _sigmoid(kk)) * (M_HD ** -0.5)).astype(BF16)

    sp = _softplus(-lam_ref[...])
    for p in range(LRU_W // LRU_PAIR):
        sl = slice(p * LRU_PAIR, (p + 1) * LRU_PAIR)
        xl = c[:, sl]
        gpre = _dot(xl.astype(BF16), w2_ref[p])
        r = _sigmoid(gpre[:, :LRU_PAIR] + bax_ref[0:1, sl])
        i = _sigmoid(gpre[:, LRU_PAIR:] + bax_ref[1:2, sl])
        log_a = (-LRU_C * r) * sp[:, sl]
        a = jnp.exp(log_a)
        mult = jnp.sqrt(-jnp.tanh(log_a) * (a * a + 1.0))
        a_ref[:, sl] = a
        b_ref[:, sl] = mult * (i * xl)

    def body(t, h):
        h = a_ref[pl.ds(t, 1), :] * h + b_ref[pl.ds(t, 1), :]
        hs_ref[pl.ds(t, 1), :] = h
        return h

    h = lax.fori_loop(0, tt, body, hc_ref[...], unroll=8)
    hc_ref[...] = h
    hlast_ref[0] = h
    hl_ref[0] = hs_ref[...].astype(BF16)


def _conv_lru(z, conv_prev, h0, cw, cb, w2, bax, lam):
    b, t, _ = z.shape
    tt = min(256, t)
    return pl.pallas_call(
        functools.partial(_conv_lru_kernel, tt=tt),
        grid=(b, t // tt),
        in_specs=[
            pl.BlockSpec((1, tt, CONV_CH), lambda i, j: (i, j, ZC_CONV * V7X_LANES // CONV_CH)),
            pl.BlockSpec((1, CONV_W - 1, CONV_CH), lambda i, j: (i, 0, 0)),
            pl.BlockSpec((1, 1, LRU_W), lambda i, j: (i, 0, 0)),
            pl.BlockSpec((CONV_W, CONV_CH), lambda i, j: (0, 0)),
            pl.BlockSpec((1, CONV_CH), lambda i, j: (0, 0)),
            pl.BlockSpec((LRU_W // LRU_PAIR, LRU_PAIR, 2 * LRU_PAIR), lambda i, j: (0, 0, 0)),
            pl.BlockSpec((2, LRU_W), lambda i, j: (0, 0)),
            pl.BlockSpec((1, LRU_W), lambda i, j: (0, 0)),
        ],
        out_specs=[
            pl.BlockSpec((1, tt, LRU_W), lambda i, j: (i, j, 0)),
            pl.BlockSpec((1, tt, 2048), lambda i, j: (i, j, 0)),
            pl.BlockSpec((1, 1, LRU_W), lambda i, j: (i, 0, 0)),
        ],
        out_shape=[
            jax.ShapeDtypeStruct((b, t, LRU_W), BF16),
            jax.ShapeDtypeStruct((b, t, 2048), BF16),
            jax.ShapeDtypeStruct((b, 1, LRU_W), F32),
        ],
        scratch_shapes=[
            pltpu.VMEM((tt + V7X_SUBLANES, CONV_CH), F32),
            pltpu.VMEM((tt, LRU_W), F32),
            pltpu.VMEM((tt, LRU_W), F32),
            pltpu.VMEM((tt, LRU_W), F32),
            pltpu.VMEM((1, LRU_W), F32),
        ],
        compiler_params=_cparams(("parallel", "arbitrary")),
        name="conv_lru",
    )(z, conv_prev, h0.reshape(b, 1, LRU_W), cw, cb.reshape(1, CONV_CH), w2, bax, lam.reshape(1, LRU_W))


def _mlstm_kernel(qk_ref, v_ref, o_ref, if_ref, c0_ref, n0_ref, m0_ref, ifb_ref, nw_ref,
                  om_ref, c_ref, n_ref, m_ref, *, lc):
    @pl.when(pl.program_id(1) == 0)
    def _():
        c_ref[...] = c0_ref[...]
        n_ref[...] = n0_ref[...]
        m_ref[...] = m0_ref[...]

    gif = if_ref[0] + ifb_ref[...]
    lf = jnp.minimum(gif, 0.0) - jnp.log1p(jnp.exp(-jnp.abs(gif)))
    cum = _cumsum_rows(lf)
    src = gif - pltpu.roll(cum, V7X_LANES - M_HEADS, axis=1)
    src_t = src.T if lc % V7X_LANES == 0 else None
    lane = lax.broadcasted_iota(jnp.int32, (lc, V7X_LANES), 1)
    tpos = lax.broadcasted_iota(jnp.int32, (lc, lc), 0)
    spos = lax.broadcasted_iota(jnp.int32, (lc, lc), 1)
    causal = spos <= tpos
    m_prev_all = m_ref[0]
    mlane = lax.broadcasted_iota(jnp.int32, (1, M_HEADS), 1)
    m_new_all = m_prev_all

    for h in range(M_HEADS):
        sl = slice(h * M_HD, (h + 1) * M_HD)
        q = qk_ref[0, :, sl]
        k = qk_ref[0, :, M_HEADS * M_HD + h * M_HD:M_HEADS * M_HD + (h + 1) * M_HD]
        vb = v_ref[0, :, sl]
        bc = cum[:, M_HEADS + h:M_HEADS + h + 1]
        ig = gif[:, h:h + 1]
        m_prev = m_prev_all[:, h:h + 1]
        if src_t is not None:
            rowv = src_t[h:h + 1, :]
        else:
            rowv = _dot_nt(jnp.where(lane == h, 1.0, 0.0), src, HI)
        dmat = jnp.where(causal, bc + rowv, -jnp.inf)
        inter = bc + m_prev
        m_t = jnp.maximum(inter, jnp.max(dmat, axis=1, keepdims=True))
        w_intra = jnp.exp(dmat - m_t)
        w_inter = jnp.exp(inter - m_t)
        s = _dot_nt(q, k) * w_intra
        c_prev = c_ref[0, h]
        n_prev = n_ref[0, h:h + 1, :]
        num = _dot(s.astype(BF16), vb) + w_inter * _dot_nt(q, c_prev.astype(BF16))
        qf = q.astype(F32)
        den = jnp.sum(s, axis=1, keepdims=True) + w_inter * jnp.sum(qf * n_prev, axis=1, keepdims=True)
        hh = num / jnp.maximum(jnp.abs(den), jnp.exp(-m_t))
        m_new = m_t[lc - 1:lc, :]
        bl = bc[lc - 1:lc, :]
        g_state = jnp.exp(bl + m_prev - m_new)
        g_src = jnp.exp(bl - bc + ig - m_new)
        c_ref[0, h] = g_state * c_prev + _dot_tn((g_src * vb.astype(F32)).astype(BF16), k)
        n_ref[0, h:h + 1, :] = g_state * n_prev + jnp.sum(g_src * k.astype(F32), axis=0, keepdims=True)
        m_new_all = jnp.where(mlane == h, m_new, m_new_all)
        d = hh - jnp.mean(hh, axis=1, keepdims=True)
        hn = d * lax.rsqrt(jnp.mean(d * d, axis=1, keepdims=True) + MH_EPS)
        om_ref[0, :, sl] = (_sigmoid(o_ref[0, :, sl].astype(F32)) * (hn * nw_ref[:, sl])).astype(BF16)

    m_ref[0] = m_new_all


def _mlstm(qk, z, zif, c0, n0, m0, if_bias, norm_w):
    b, t, _ = z.shape
    lc = min(256, t)
    w = M_HEADS * M_HD
    return pl.pallas_call(
        functools.partial(_mlstm_kernel, lc=lc),
        grid=(b, t // lc),
        in_specs=[
            pl.BlockSpec((1, lc, 2 * w), lambda i, j: (i, j, 0)),
            pl.BlockSpec((1, lc, w), lambda i, j: (i, j, ZC_MV * V7X_LANES // w)),
            pl.BlockSpec((1, lc, w), lambda i, j: (i, j, ZC_MO * V7X_LANES // w)),
            pl.BlockSpec((1, lc, V7X_LANES), lambda i, j: (i, j, 0)),
            pl.BlockSpec((1, M_HEADS, M_HD, M_HD), lambda i, j: (i, 0, 0, 0)),
            pl.BlockSpec((1, M_HEADS, M_HD), lambda i, j: (i, 0, 0)),
            pl.BlockSpec((1, 1, M_HEADS), lambda i, j: (i, 0, 0)),
            pl.BlockSpec((1, V7X_LANES), lambda i, j: (0, 0)),
            pl.BlockSpec((1, w), lambda i, j: (0, 0)),
        ],
        out_specs=[
            pl.BlockSpec((1, lc, w), lambda i, j: (i, j, 0)),
            pl.BlockSpec((1, M_HEADS, M_HD, M_HD), lambda i, j: (i, 0, 0, 0)),
            pl.BlockSpec((1, M_HEADS, M_HD), lambda i, j: (i, 0, 0)),
            pl.BlockSpec((1, 1, M_HEADS), lambda i, j: (i, 0, 0)),
        ],
        out_shape=[
            jax.ShapeDtypeStruct((b, t, w), BF16),
            jax.ShapeDtypeStruct((b, M_HEADS, M_HD, M_HD), F32),
            jax.ShapeDtypeStruct((b, M_HEADS, M_HD), F32),
            jax.ShapeDtypeStruct((b, 1, M_HEADS), F32),
        ],
        compiler_params=_cparams(("parallel", "arbitrary")),
        name="mlstm",
    )(qk, z, z, zif, c0, n0, m0.reshape(b, 1, M_HEADS), if_bias, norm_w.reshape(1, w))


def _rwkv_mix_kernel(zr_ref, zk_ref, zv_ref, zwa_ref, zgd_ref, sh_ref, mu_ref, w0_ref, w2_ref, a0_ref, a2_ref, g2_ref,
                     kk_ref, ka_ref, s0_ref, lnw_ref, lnb_ref, rk_ref, e2_ref,
                     o_ref, s_ref, y_ref, car_ref, *, npair, tt, lc, rg):
    @pl.when(pl.program_id(2) == 0)
    def _():
        s_ref[...] = s0_ref[...]
        car_ref[...] = sh_ref[0]

    def seg(x):
        hi = x.astype(BF16)
        lo = (x - hi.astype(F32)).astype(BF16)
        return _dot(hi, e2_ref[...]) + _dot(lo, e2_ref[...])

    first_row = lax.broadcasted_iota(jnp.int32, (tt, 1), 0) == 0
    pairs = range(npair)
    ps = lambda p: slice(p * V7X_LANES, (p + 1) * V7X_LANES)
    hs = lambda h: slice(h * R_HD, (h + 1) * R_HD)

    def shifted(zp, p, i):
        zp = zp.astype(F32)
        prev = jnp.where(first_row, car_ref[p, i:i + 1, :], pltpu.roll(zp, 1, axis=0))
        car_ref[p, i:i + 1, :] = zp[tt - 1:tt, :]
        return zp + (prev - zp) * mu_ref[p, i:i + 1, :]

    xwa = shifted(zwa_ref[0], 0, 3)
    xgd = shifted(zgd_ref[0], 0, 4)
    twa_b = jnp.tanh(xwa).astype(BF16)
    xwa_b = xwa.astype(BF16)
    sgd_b = _sigmoid(xgd).astype(BF16)
    r_all, k_all, v_all, lw_all, a_all, b_all, g_all = {}, {}, {}, {}, {}, {}, {}
    for p in pairs:
        r_all[p] = shifted(zr_ref[0, :, ps(p)], p, 0)
        k_in = shifted(zk_ref[0, :, ps(p)], p, 1)
        v_all[p] = shifted(zv_ref[0, :, ps(p)], p, 2)
        w_log = -_softplus(-(w0_ref[p] + _dot(twa_b, w2_ref[:, ps(p)]))) - 0.5
        lw_all[p] = -jnp.exp(w_log)
        a_gate = _sigmoid(a0_ref[p] + _dot(xwa_b, a2_ref[:, ps(p)]))
        g_all[p] = _dot(sgd_b, g2_ref[:, ps(p)])
        kk = k_in * kk_ref[p]
        kk = kk * lax.rsqrt(jnp.maximum(seg(kk * kk), 1e-24))
        k_all[p] = k_in * (1.0 + (a_gate - 1.0) * ka_ref[p])
        a_all[p] = -kk
        b_all[p] = kk * a_gate

    nb = rg // lc
    trow = lax.broadcasted_iota(jnp.int32, (rg, 2 * rg), 0)
    scol = lax.broadcasted_iota(jnp.int32, (rg, 2 * rg), 1)
    scol = jnp.where(scol >= rg, scol - rg, scol)
    dist = trow - scol
    tin = trow & (lc - 1)
    smask = (dist > 0) & (dist <= tin)
    imask = (dist >= 0) & (dist <= tin)
    lsh = lc.bit_length() - 1
    qrow = lax.broadcasted_iota(jnp.int32, (rg, rg), 0)
    qcol = lax.broadcasted_iota(jnp.int32, (rg, rg), 1)
    same = (qrow >> lsh) == (qcol >> lsh)
    blk = jnp.where(same, 1.0, 0.0).astype(BF16)
    tril = jnp.where(qcol <= qrow, blk, jnp.zeros((), BF16))
    xrow = lax.broadcasted_iota(jnp.int32, (rg, nb * R_HD), 0)
    xcol = lax.broadcasted_iota(jnp.int32, (rg, nb * R_HD), 1)
    xmask = (xrow >> (lc.bit_length() - 1)) == (xcol >> (R_HD.bit_length() - 1))
    n_sq = max(lc.bit_length() - 1, 0)

    groups = range(tt // rg)
    chains = [(p, g, h) for p in pairs for g in groups for h in range(2)]
    ew = {}
    for p in pairs:
        for g in groups:
            rows = slice(g * rg, (g + 1) * rg)
            lw = lw_all[p][rows]
            k = k_all[p][rows]
            b = b_all[p][rows]
            lw_hi = lw.astype(BF16)
            lw_lo = (lw - lw_hi.astype(F32)).astype(BF16)
            cum = _dot(tril, lw_hi) + _dot(tril, lw_lo)
            cum_end = _dot(blk, lw_hi) + _dot(blk, lw_lo)
            e_neg = jnp.exp(-cum)
            at = a_all[p][rows] * jnp.exp(cum - lw)
            rt = r_all[p][rows] * jnp.exp(cum)
            e_end = jnp.exp(cum_end - cum)
            ew[p, g] = dict(at=at, rt=rt, atb=at.astype(BF16), rtb=rt.astype(BF16),
                            rhs=jnp.concatenate([b * e_neg, k * e_neg], axis=0).astype(BF16),
                            be=(b * e_end).astype(BF16), ke=(k * e_end).astype(BF16),
                            vb=v_all[p][rows].astype(BF16),
                            w_end=[jnp.exp(cum_end[c * lc:c * lc + 1, :]) for c in range(nb)])

    def expand(x):
        if nb == 1:
            return x
        return jnp.where(xmask, jnp.concatenate([x] * nb, axis=1), jnp.zeros((), x.dtype))

    piece = lambda c, name: ew[c[0], c[1]][name][:, hs(c[2])]
    ga = {c: jnp.where(smask, _dot_nt(piece(c, "atb"), piece(c, "rhs")), 0.0) for c in chains}
    gr = {c: jnp.where(imask, _dot_nt(piece(c, "rtb"), piece(c, "rhs")), 0.0) for c in chains}
    gva = {c: _dot(ga[c][:, rg:].astype(BF16), piece(c, "vb")) for c in chains}
    gvr = {c: _dot(gr[c][:, rg:].astype(BF16), piece(c, "vb")) for c in chains}
    x = {c: jnp.concatenate([piece(c, "at"), gva[c]], axis=1) for c in chains}
    pw = {c: ga[c][:, :rg] for c in chains}
    for i in range(n_sq):
        pwb = {c: pw[c].astype(BF16) for c in chains}
        x = {c: x[c] + _dot(pwb[c], x[c].astype(BF16)) for c in chains}
        if i + 1 < n_sq:
            pw = {c: _dot(pwb[c], pwb[c]) for c in chains}
    xb = {c: x[c].astype(BF16) for c in chains}
    pq2 = {c: jnp.concatenate([piece(c, "rt"), gvr[c]], axis=1) + _dot(gr[c][:, :rg].astype(BF16), xb[c])
           for c in chains}
    xe = {c: _dot_tn(xb[c], expand(piece(c, "be"))) for c in chains}
    vk = {c: _dot_tn(piece(c, "vb"), expand(piece(c, "ke"))) for c in chains}

    s = {(p, h): s_ref[0, 2 * p + h] for p in pairs for h in range(2)}
    for g in groups:
        for c in range(nb):
            rows = slice(c * lc, (c + 1) * lc)
            cols = slice(c * R_HD, (c + 1) * R_HD)
            sb = {ph: s[ph].astype(BF16) for ph in s}
            for p in pairs:
                ys = [_dot_nt(pq2[p, g, h][rows, :R_HD].astype(BF16), sb[p, h]) + pq2[p, g, h][rows, R_HD:]
                      for h in range(2)]
                y_ref[g * rg + c * lc:g * rg + (c + 1) * lc, ps(p)] = jnp.concatenate(ys, axis=1)
            s = {(p, h): (s[p, h] * ew[p, g]["w_end"][c][:, hs(h)]
                          + _dot(sb[p, h], xe[p, g, h][:R_HD, cols].astype(BF16))
                          + (xe[p, g, h][R_HD:, cols] + vk[p, g, h][:, cols])) for (p, h) in s}
    for (p, h) in s:
        s_ref[0, 2 * p + h] = s[p, h]

    for p in pairs:
        y = y_ref[:, ps(p)]
        d = y - seg(y) * (1.0 / R_HD)
        var = seg(d * d) * (1.0 / R_HD)
        yn = (d * lax.rsqrt(var + RWKV_GN_EPS)) * lnw_ref[p] + lnb_ref[p]
        yn = yn + seg((r_all[p] * k_all[p]) * rk_ref[p]) * v_all[p]
        o_ref[0, :, ps(p)] = (yn * g_all[p]).astype(BF16)


def _rwkv_mix(z, shift_prev, s0, mu, w0, w2p, a0, a2p, g2, k_k, k_a, ln_w, ln_b, r_k, e2):
    bsz, t, _ = z.shape
    tt = min(RWKV_TT, t)
    lc = min(RWKV_CHUNK, t)
    rg = min(RWKV_GROUP, t)
    npair = RWKV_PAIRS_PER_STEP
    wide = npair * V7X_LANES
    zpair = lambda blk: pl.BlockSpec((1, tt, wide), lambda i, p, j: (i, j, blk // npair + p))
    zshared = lambda blk: pl.BlockSpec((1, tt, V7X_LANES), lambda i, p, j: (i, j, blk))
    par_spec = pl.BlockSpec((npair, 1, V7X_LANES), lambda i, p, j: (p, 0, 0))
    lora_spec = pl.BlockSpec((V7X_LANES, wide), lambda i, p, j: (0, p))
    st_spec = pl.BlockSpec((1, 2 * npair, R_HD, R_HD), lambda i, p, j: (i, p, 0, 0))
    pair = lambda x: x.reshape(R_PAIRS, 1, V7X_LANES)

    def pieces(x):
        lead = x.shape[:-1]
        rkv = x[..., :3 * R_W].reshape(lead + (3, R_PAIRS, V7X_LANES))
        rkv = jnp.moveaxis(rkv, -3, -2)
        lora = jnp.broadcast_to(x[..., 3 * R_W:].reshape(lead + (1, 2, V7X_LANES)), lead + (R_PAIRS, 2, V7X_LANES))
        return jnp.concatenate([rkv, lora], axis=-2)

    return pl.pallas_call(
        functools.partial(_rwkv_mix_kernel, npair=npair, tt=tt, lc=lc, rg=rg),
        grid=(bsz, R_PAIRS // npair, t // tt),
        in_specs=[
            zpair(ZC_RW), zpair(ZC_RW + 8), zpair(ZC_RW + 16), zshared(ZC_RW + 24), zshared(ZC_RW + 25),
            pl.BlockSpec((1, npair, 5, V7X_LANES), lambda i, p, j: (i, p, 0, 0)),
            pl.BlockSpec((npair, 5, V7X_LANES), lambda i, p, j: (p, 0, 0)),
            par_spec, lora_spec, par_spec, lora_spec, lora_spec, par_spec, par_spec,
            st_spec, par_spec, par_spec, par_spec,
            pl.BlockSpec((V7X_LANES, V7X_LANES), lambda i, p, j: (0, 0)),
        ],
        out_specs=[
            pl.BlockSpec((1, tt, wide), lambda i, p, j: (i, j, p)),
            st_spec,
        ],
        out_shape=[
            jax.ShapeDtypeStruct((bsz, t, R_W), BF16),
            jax.ShapeDtypeStruct((bsz, R_HEADS, R_HD, R_HD), F32),
        ],
        scratch_shapes=[pltpu.VMEM((tt, wide), F32), pltpu.VMEM((npair, 5, V7X_LANES), F32)],
        compiler_params=_cparams(("parallel", "parallel", "arbitrary")),
        name="rwkv_mix",
    )(z, z, z, z, z, pieces(shift_prev.reshape(bsz, R_IN)), pieces(mu), pair(w0), w2p, pair(a0), a2p, g2,
      pair(k_k), pair(k_a), s0, pair(ln_w), pair(ln_b), pair(r_k), e2)


def _merge_kernel(x_ref, zg_ref, hl_ref, om_ref, or_ref, wb_ref, wo_ref, o_ref):
    acc = None
    for gidx, br in enumerate((hl_ref, om_ref, or_ref)):
        sl = slice(gidx * D_MODEL, (gidx + 1) * D_MODEL)
        term = _sigmoid(zg_ref[:, sl].astype(F32)) * _dot(br[...], wb_ref[gidx])
        acc = term if acc is None else acc + term
    o_ref[...] = x_ref[...] + _dot(acc.astype(BF16), wo_ref[...])


def _merge(x, z2d, hl, om, orw, wb, wo):
    m = x.shape[0]
    tm = min(512, m)
    tok = pl.BlockSpec((tm, D_MODEL), lambda i: (i, 0))
    return pl.pallas_call(
        _merge_kernel,
        grid=(m // tm,),
        in_specs=[
            tok,
            pl.BlockSpec((tm, 3 * D_MODEL), lambda i: (i, ZC_GATE * V7X_LANES // (3 * D_MODEL))),
            tok, tok, tok,
            pl.BlockSpec((3, D_MODEL, D_MODEL), lambda i: (0, 0, 0)),
            pl.BlockSpec((D_MODEL, D_MODEL), lambda i: (0, 0)),
        ],
        out_specs=tok,
        out_shape=jax.ShapeDtypeStruct((m, D_MODEL), F32),
        compiler_params=_cparams(("parallel",)),
        name="merge",
    )(x, z2d, hl, om, orw, wb, wo)


def _prep_layer_weights(p, l):
    w_in = p["w_in"][l]
    c0, c1, c2, c3, c4 = 3072, 4096, 5120, 5128, 8456
    w_if = jnp.pad(w_in[:, c2:c3], ((0, 0), (0, V7X_LANES - 2 * M_HEADS))).astype(BF16)
    w_all = jnp.concatenate([w_in[:, :c0], w_in[:, c4:], w_in[:, c0:c1], w_in[:, c1:c2], w_in[:, c3:c4]],
                            axis=1).astype(BF16)
    def pairs(w):
        z = jnp.zeros((LRU_BD, LRU_BD), F32)
        return jnp.stack([jnp.block([[w[2 * i], z], [z, w[2 * i + 1]]]) for i in range(LRU_BLOCKS // 2)])
    w2 = jnp.concatenate([pairs(p["lru_wa"][l]), pairs(p["lru_wx"][l])], axis=2).astype(BF16)
    bax = jnp.stack([p["lru_ba"][l], p["lru_bx"][l]])
    if_bias = jnp.pad(p["mlstm_if_bias"][l], (0, V7X_LANES - 2 * M_HEADS)).reshape(1, V7X_LANES)
    zpad = jnp.zeros((64, R_W), F32)
    w2p = jnp.concatenate([p["rwkv_w2"][l], zpad], axis=0).astype(BF16)
    a2p = jnp.concatenate([zpad, p["rwkv_a2"][l]], axis=0).astype(BF16)
    return dict(
        w_all=w_all, w_if=w_if, w2=w2, bax=bax, if_bias=if_bias, w2p=w2p, a2p=a2p,
        g2=p["rwkv_g2"][l].astype(BF16),
        wb=p["w_branch"][l].astype(BF16), wo=p["w_out"][l].astype(BF16),
        f1=(p["ffn1_w_gate"][l].astype(BF16), p["ffn1_w_up"][l].astype(BF16), p["ffn1_w_down"][l].astype(BF16)),
        f2=(p["ffn2_w_gate"][l].astype(BF16), p["ffn2_w_up"][l].astype(BF16), p["ffn2_w_down"][l].astype(BF16)),
    )


def _segment_matrix():
    half = jnp.arange(V7X_LANES) // R_HD
    return (half[:, None] == half[None, :]).astype(BF16)


def _run_group(x, states, p, lw_list, final_norm):
    bsz, t, _ = x.shape
    m = bsz * t
    e2 = _segment_matrix()
    xf = x.reshape(m, D_MODEL)
    new_states = []
    depth = p["w_in"].shape[0]
    for l in range(depth):
        lw = lw_list[l]
        conv_prev, lru_h, m_c, m_n, m_m, shift_prev, rwkv_s = states[l]
        xf = _ffn(xf, p["ffn1_norm"][l], *lw["f1"])
        z2d, zif = _inproj(xf, p["mix_norm"][l], lw["w_all"], lw["w_if"])
        z = z2d.reshape(bsz, t, Z_W)
        hl, qk, h_last = _conv_lru(z, conv_prev, lru_h, p["conv_w"][l], p["conv_b"][l], lw["w2"], lw["bax"],
                                   p["lru_lambda"][l])
        om, c_new, n_new, m_new = _mlstm(qk, z, zif.reshape(bsz, t, V7X_LANES), m_c, m_n, m_m, lw["if_bias"],
                                         p["mlstm_norm"][l])
        orw, s_new = _rwkv_mix(z, shift_prev, rwkv_s, p["rwkv_mu"][l], p["rwkv_w0"][l], lw["w2p"], p["rwkv_a0"][l],
                               lw["a2p"], lw["g2"], p["rwkv_k_k"][l], p["rwkv_k_a"][l], p["rwkv_ln_w"][l],
                               p["rwkv_ln_b"][l], p["rwkv_r_k"][l].reshape(-1), e2)
        xf = _merge(xf, z2d, hl.reshape(m, D_MODEL), om.reshape(m, D_MODEL), orw.reshape(m, D_MODEL),
                    lw["wb"], lw["wo"])
        xf = _ffn(xf, p["ffn2_norm"][l], *lw["f2"], final_g=final_norm if l == depth - 1 else None)
        conv_new = z[:, t - (CONV_W - 1):, ZC_CONV * V7X_LANES:ZC_CONV * V7X_LANES + CONV_CH].astype(F32)
        shift_new = z[:, t - 1:, ZC_RW * V7X_LANES:ZC_RW * V7X_LANES + R_IN].astype(F32)
        new_states.append((conv_new, h_last.reshape(bsz, LRU_W), c_new, n_new, m_new.reshape(bsz, M_HEADS),
                           shift_new, s_new))
    stacked = tuple(jnp.stack([st[i] for st in new_states]) for i in range(7))
    return xf.reshape(bsz, t, D_MODEL), stacked


def _zero_states(bsz):
    return (jnp.zeros((bsz, CONV_W - 1, CONV_CH), F32), jnp.zeros((bsz, LRU_W), F32),
            jnp.zeros((bsz, M_HEADS, M_HD, M_HD), F32), jnp.zeros((bsz, M_HEADS, M_HD), F32),
            jnp.zeros((bsz, M_HEADS), F32), jnp.zeros((bsz, 1, R_IN), F32),
            jnp.zeros((bsz, R_HEADS, R_HD, R_HD), F32))


def kernel(x_prompt, x_sample, state_conv, state_lru_h, state_mlstm_C, state_mlstm_n, state_mlstm_m, state_rwkv_shift, state_rwkv_S, ffn1_norm, ffn1_w_gate, ffn1_w_up, ffn1_w_down, mix_norm, w_in, conv_w, conv_b, lru_wa, lru_ba, lru_wx, lru_bx, lru_lambda, mlstm_if_bias, mlstm_norm, rwkv_mu, rwkv_w0, rwkv_w2, rwkv_a0, rwkv_a2, rwkv_g2, rwkv_k_k, rwkv_k_a, rwkv_r_k, rwkv_ln_w, rwkv_ln_b, w_branch, w_out, ffn2_norm, ffn2_w_gate, ffn2_w_up, ffn2_w_down, final_norm):
    p = dict(ffn1_norm=ffn1_norm, ffn1_w_gate=ffn1_w_gate, ffn1_w_up=ffn1_w_up, ffn1_w_down=ffn1_w_down,
             mix_norm=mix_norm, w_in=w_in, conv_w=conv_w, conv_b=conv_b,
             lru_wa=lru_wa, lru_ba=lru_ba, lru_wx=lru_wx, lru_bx=lru_bx, lru_lambda=lru_lambda,
             mlstm_if_bias=mlstm_if_bias, mlstm_norm=mlstm_norm,
             rwkv_mu=rwkv_mu, rwkv_w0=rwkv_w0, rwkv_w2=rwkv_w2, rwkv_a0=rwkv_a0, rwkv_a2=rwkv_a2,
             rwkv_g2=rwkv_g2, rwkv_k_k=rwkv_k_k, rwkv_k_a=rwkv_k_a, rwkv_r_k=rwkv_r_k,
             rwkv_ln_w=rwkv_ln_w, rwkv_ln_b=rwkv_ln_b, w_branch=w_branch, w_out=w_out,
             ffn2_norm=ffn2_norm, ffn2_w_gate=ffn2_w_gate, ffn2_w_up=ffn2_w_up, ffn2_w_down=ffn2_w_down)
    depth = w_in.shape[0]
    lw_list = [_prep_layer_weights(p, l) for l in range(depth)]
    states_p = [_zero_states(x_prompt.shape[0]) for _ in range(depth)]
    y_p, st_p = _run_group(x_prompt, states_p, p, lw_list, final_norm)
    states_s = [(state_conv[l], state_lru_h[l], state_mlstm_C[l], state_mlstm_n[l], state_mlstm_m[l],
                 state_rwkv_shift[l], state_rwkv_S[l]) for l in range(depth)]
    y_s, st_s = _run_group(x_sample, states_s, p, lw_list, final_norm)
    return (y_p, y_s) + st_p + st_s
```

```python
import functools

import jax
import jax.numpy as jnp
from jax import lax
from jax.experimental import pallas as pl
from jax.experimental.pallas import tpu as pltpu

F32 = jnp.float32
BF16 = jnp.bfloat16

V7X_LANES = 128
V7X_SUBLANES = 8
V7X_VMEM_BYTES = 64 * 1024 * 1024
VMEM_LIMIT = 56 * 1024 * 1024

D_MODEL = 1024
D_FF = 2816
CONV_W = 4
LRU_W = 1024
LRU_BLOCKS = 8
LRU_BD = LRU_W // LRU_BLOCKS
LRU_PAIR = 2 * LRU_BD
LRU_C = 8.0
M_HEADS = 4
M_HD = 256
R_HD = 64
R_HEADS = 16
R_PAIRS = R_HEADS // 2
R_W = 1024
R_IN = 3 * R_W + 256
CONV_CH = 3 * 1024
RMS_EPS = 1e-6
MH_EPS = 1e-6
RWKV_GN_EPS = 64e-5

ZC_CONV = 0
ZC_GATE = 24
ZC_MV = 48
ZC_MO = 56
ZC_RW = 64
Z_BLOCKS = 90
Z_W = Z_BLOCKS * V7X_LANES
Z_TN = 18 * V7X_LANES

FFN_TF = D_FF // 2
FFN_SUB = ((0, 768), (768, FFN_TF))
CONV_ROWS = 128
RWKV_CHUNK = 64
RWKV_GROUP = 128
RWKV_TT = 128
RWKV_PAIRS_PER_STEP = 8

HI = lax.Precision.HIGHEST


def _cparams(sem):
    return pltpu.CompilerParams(dimension_semantics=sem, vmem_limit_bytes=VMEM_LIMIT)


def _rms(x, g):
    return (x * lax.rsqrt(jnp.mean(x * x, axis=-1, keepdims=True) + RMS_EPS)) * g


def _softplus(x):
    return jnp.maximum(x, 0.0) + jnp.log1p(jnp.exp(-jnp.abs(x)))


def _sigmoid(x):
    return jax.nn.sigmoid(x)


def _dot(a, b, prec=None):
    return jnp.dot(a, b, preferred_element_type=F32, precision=prec)


def _dot_nt(a, b, prec=None):
    return lax.dot_general(a, b, (((1,), (1,)), ((), ())), preferred_element_type=F32, precision=prec)


def _dot_tn(a, b, prec=None):
    return lax.dot_general(a, b, (((0,), (0,)), ((), ())), preferred_element_type=F32, precision=prec)


def _shift_rows(x, d, row):
    return jnp.where(row >= d, pltpu.roll(x, d, axis=0), 0.0)


def _cumsum_rows(x):
    n = x.shape[0]
    row = lax.broadcasted_iota(jnp.int32, x.shape, 0)
    d = 1
    while d < n:
        x = x + _shift_rows(x, d, row)
        d *= 2
    return x


def _ffn_kernel(x_ref, g_ref, wg_ref, wu_ref, wd_ref, *rest, final_norm):
    if final_norm:
        fg_ref, o_ref, xn_ref = rest
    else:
        o_ref, xn_ref = rest
    k = pl.program_id(1)

    @pl.when(k == 0)
    def _():
        xn_ref[...] = _rms(x_ref[...], g_ref[...]).astype(BF16)

    xn = xn_ref[...]
    part = None
    for c0, c1 in FFN_SUB:
        hg = _dot(xn, wg_ref[:, c0:c1])
        hu = _dot(xn, wu_ref[:, c0:c1])
        h = (hg * _sigmoid(hg)) * hu
        d = _dot(h.astype(BF16), wd_ref[c0:c1, :])
        part = d if part is None else part + d

    @pl.when(k == 0)
    def _():
        o_ref[...] = part

    @pl.when(k == 1)
    def _():
        y = x_ref[...] + 0.5 * (o_ref[...] + part)
        if final_norm:
            y = _rms(y, fg_ref[...])
        o_ref[...] = y


def _ffn(x, g, wg, wu, wd, final_g=None):
    m = x.shape[0]
    tm = min(1024, m)
    nk = D_FF // FFN_TF
    assert nk == 2
    in_specs = [
        pl.BlockSpec((tm, D_MODEL), lambda i, k: (i, 0)),
        pl.BlockSpec((1, D_MODEL), lambda i, k: (0, 0)),
        pl.BlockSpec((D_MODEL, FFN_TF), lambda i, k: (0, k)),
        pl.BlockSpec((D_MODEL, FFN_TF), lambda i, k: (0, k)),
        pl.BlockSpec((FFN_TF, D_MODEL), lambda i, k: (k, 0)),
    ]
    args = [x, g.reshape(1, D_MODEL), wg, wu, wd]
    if final_g is not None:
        in_specs.append(pl.BlockSpec((1, D_MODEL), lambda i, k: (0, 0)))
        args.append(final_g.reshape(1, D_MODEL))
    return pl.pallas_call(
        functools.partial(_ffn_kernel, final_norm=final_g is not None),
        grid=(m // tm, nk),
        in_specs=in_specs,
        out_specs=pl.BlockSpec((tm, D_MODEL), lambda i, k: (i, 0)),
        out_shape=jax.ShapeDtypeStruct((m, D_MODEL), F32),
        scratch_shapes=[pltpu.VMEM((tm, D_MODEL), BF16)],
        compiler_params=_cparams(("parallel", "arbitrary")),
        name="ffn",
    )(*args)


def _inproj_kernel(x_ref, g_ref, w_ref, wif_ref, o_ref, oif_ref, xn_ref):
    @pl.when(pl.program_id(1) == 0)
    def _():
        xn = _rms(x_ref[...], g_ref[...]).astype(BF16)
        xn_ref[...] = xn
        oif_ref[...] = _dot(xn, wif_ref[...])

    o_ref[...] = _dot(xn_ref[...], w_ref[...]).astype(BF16)


def _inproj(x, g, w_all, w_if):
    m = x.shape[0]
    tm = min(1024, m)
    return pl.pallas_call(
        _inproj_kernel,
        grid=(m // tm, Z_W // Z_TN),
        in_specs=[
            pl.BlockSpec((tm, D_MODEL), lambda i, j: (i, 0)),
            pl.BlockSpec((1, D_MODEL), lambda i, j: (0, 0)),
            pl.BlockSpec((D_MODEL, Z_TN), lambda i, j: (0, j)),
            pl.BlockSpec((D_MODEL, V7X_LANES), lambda i, j: (0, 0)),
        ],
        out_specs=[
            pl.BlockSpec((tm, Z_TN), lambda i, j: (i, j)),
            pl.BlockSpec((tm, V7X_LANES), lambda i, j: (i, 0)),
        ],
        out_shape=[
            jax.ShapeDtypeStruct((m, Z_W), BF16),
            jax.ShapeDtypeStruct((m, V7X_LANES), F32),
        ],
        scratch_shapes=[pltpu.VMEM((tm, D_MODEL), BF16)],
        compiler_params=_cparams(("parallel", "arbitrary")),
        name="in_proj",
    )(x, g.reshape(1, D_MODEL), w_all, w_if)


def _conv_lru_kernel(z_ref, cprev_ref, h0_ref, cw_ref, cb_ref, w2_ref, bax_ref, lam_ref,
                     hl_ref, qk_ref, hlast_ref, xp_ref, a_ref, b_ref, hs_ref, hc_ref, *, tt):
    pad = V7X_SUBLANES

    @pl.when(pl.program_id(1) == 0)
    def _():
        xp_ref[0:pad, :] = jnp.zeros((pad, CONV_CH), F32)
        xp_ref[pad - 3:pad, :] = cprev_ref[0]
        hc_ref[...] = h0_ref[0]

    xp_ref[pad:pad + tt, :] = z_ref[0].astype(F32)
    sp = _softplus(-lam_ref[...])
    rb = min(CONV_ROWS, tt)
    for r0 in range(0, tt, rb):
        rows = slice(r0, r0 + rb)
        for cblk in range(CONV_CH // LRU_PAIR):
            cols = slice(cblk * LRU_PAIR, (cblk + 1) * LRU_PAIR)
            xe = xp_ref[r0:pad + r0 + rb, cols]
            c = xe * cw_ref[0:1, cols] + cb_ref[:, cols]
            c = pltpu.roll(c, 1, axis=0) + xe * cw_ref[1:2, cols]
            c = pltpu.roll(c, 1, axis=0) + xe * cw_ref[2:3, cols]
            c = (pltpu.roll(c, 1, axis=0) + xe * cw_ref[3:4, cols])[pad:]
            if cblk < LRU_W // LRU_PAIR:
                gpre = _dot(c.astype(BF16), w2_ref[cblk])
                r = _sigmoid(gpre[:, :LRU_PAIR] + bax_ref[0:1, cols])
                i = _sigmoid(gpre[:, LRU_PAIR:] + bax_ref[1:2, cols])
                log_a = (-LRU_C * r) * sp[:, cols]
                a = jnp.exp(log_a)
                mult = jnp.sqrt(-jnp.tanh(log_a) * (a * a + 1.0))
                a_ref[rows, cols] = a
                b_ref[rows, cols] = mult * (i * c)
            elif cblk < 2 * LRU_W // LRU_PAIR:
                qk_ref[0, rows, cblk * LRU_PAIR - LRU_W:(cblk + 1) * LRU_PAIR - LRU_W] = (
                    c * _sigmoid(c)).astype(BF16)
            else:
                qk_ref[0, rows, cblk * LRU_PAIR - LRU_W:(cblk + 1) * LRU_PAIR - LRU_W] = (
                    (c * _sigmoid(c)) * (M_HD ** -0.5)).astype(BF16)
    xp_ref[pad - 3:pad, :] = xp_ref[pad + tt - 3:pad + tt, :]

    def body(t, h):
        h = a_ref[pl.ds(t, 1), :] * h + b_ref[pl.ds(t, 1), :]
        hs_ref[pl.ds(t, 1), :] = h
        return h

    h = lax.fori_loop(0, tt, body, hc_ref[...], unroll=8)
    hc_ref[...] = h
    hlast_ref[0] = h
    hl_ref[0] = hs_ref[...].astype(BF16)


def _conv_lru(z, conv_prev, h0, cw, cb, w2, bax, lam):
    b, t, _ = z.shape
    tt = min(256, t)
    return pl.pallas_call(
        functools.partial(_conv_lru_kernel, tt=tt),
        grid=(b, t // tt),
        in_specs=[
            pl.BlockSpec((1, tt, CONV_CH), lambda i, j: (i, j, ZC_CONV * V7X_LANES // CONV_CH)),
            pl.BlockSpec((1, CONV_W - 1, CONV_CH), lambda i, j: (i, 0, 0)),
            pl.BlockSpec((1, 1, LRU_W), lambda i, j: (i, 0, 0)),
            pl.BlockSpec((CONV_W, CONV_CH), lambda i, j: (0, 0)),
            pl.BlockSpec((1, CONV_CH), lambda i, j: (0, 0)),
            pl.BlockSpec((LRU_W // LRU_PAIR, LRU_PAIR, 2 * LRU_PAIR), lambda i, j: (0, 0, 0)),
            pl.BlockSpec((2, LRU_W), lambda i, j: (0, 0)),
            pl.BlockSpec((1, LRU_W), lambda i, j: (0, 0)),
        ],
        out_specs=[
            pl.BlockSpec((1, tt, LRU_W), lambda i, j: (i, j, 0)),
            pl.BlockSpec((1, tt, 2048), lambda i, j: (i, j, 0)),
            pl.BlockSpec((1, 1, LRU_W), lambda i, j: (i, 0, 0)),
        ],
        out_shape=[
            jax.ShapeDtypeStruct((b, t, LRU_W), BF16),
            jax.ShapeDtypeStruct((b, t, 2048), BF16),
            jax.ShapeDtypeStruct((b, 1, LRU_W), F32),
        ],
        scratch_shapes=[
            pltpu.VMEM((tt + V7X_SUBLANES, CONV_CH), F32),
            pltpu.VMEM((tt, LRU_W), F32),
            pltpu.VMEM((tt, LRU_W), F32),
            pltpu.VMEM((tt, LRU_W), F32),
            pltpu.VMEM((1, LRU_W), F32),
        ],
        compiler_params=_cparams(("parallel", "arbitrary")),
        name="conv_lru",
    )(z, conv_prev, h0.reshape(b, 1, LRU_W), cw, cb.reshape(1, CONV_CH), w2, bax, lam.reshape(1, LRU_W))


def _mlstm_kernel(qk_ref, v_ref, o_ref, if_ref, c0_ref, n0_ref, m0_ref, ifb_ref, nw_ref,
                  om_ref, c_ref, n_ref, m_ref, *, lc):
    @pl.when(pl.program_id(1) == 0)
    def _():
        c_ref[...] = c0_ref[...]
        n_ref[...] = n0_ref[...]
        m_ref[...] = m0_ref[...]

    gif = if_ref[0] + ifb_ref[...]
    lf = jnp.minimum(gif, 0.0) - jnp.log1p(jnp.exp(-jnp.abs(gif)))
    cum = _cumsum_rows(lf)
    src = gif - pltpu.roll(cum, V7X_LANES - M_HEADS, axis=1)
    src_t = src.T if lc % V7X_LANES == 0 else None
    lane = lax.broadcasted_iota(jnp.int32, (lc, V7X_LANES), 1)
    tpos = lax.broadcasted_iota(jnp.int32, (lc, lc), 0)
    spos = lax.broadcasted_iota(jnp.int32, (lc, lc), 1)
    causal = spos <= tpos
    m_prev_all = m_ref[0]
    mlane = lax.broadcasted_iota(jnp.int32, (1, M_HEADS), 1)
    m_new_all = m_prev_all

    heads = range(M_HEADS)
    sl = lambda h: slice(h * M_HD, (h + 1) * M_HD)
    q = {h: qk_ref[0, :, sl(h)] for h in heads}
    k = {h: qk_ref[0, :, M_HEADS * M_HD + h * M_HD:M_HEADS * M_HD + (h + 1) * M_HD] for h in heads}
    vb = {h: v_ref[0, :, sl(h)] for h in heads}
    bc = {h: cum[:, M_HEADS + h:M_HEADS + h + 1] for h in heads}
    ig = {h: gif[:, h:h + 1] for h in heads}
    m_prev = {h: m_prev_all[:, h:h + 1] for h in heads}
    c_prev = {h: c_ref[0, h] for h in heads}
    n_prev = {h: n_ref[0, h:h + 1, :] for h in heads}
    if src_t is not None:
        rowv = {h: src_t[h:h + 1, :] for h in heads}
    else:
        rowv = {h: _dot_nt(jnp.where(lane == h, 1.0, 0.0), src, HI) for h in heads}
    dmat = {h: jnp.where(causal, bc[h] + rowv[h], -jnp.inf) for h in heads}
    inter = {h: bc[h] + m_prev[h] for h in heads}
    m_t = {h: jnp.maximum(inter[h], jnp.max(dmat[h], axis=1, keepdims=True)) for h in heads}
    qk = {h: _dot_nt(q[h], k[h]) for h in heads}
    qc = {h: _dot_nt(q[h], c_prev[h].astype(BF16)) for h in heads}
    qn = {h: jnp.sum(q[h].astype(F32) * n_prev[h], axis=1, keepdims=True) for h in heads}
    w_inter = {h: jnp.exp(inter[h] - m_t[h]) for h in heads}
    s = {h: qk[h] * jnp.exp(dmat[h] - m_t[h]) for h in heads}
    num = {h: _dot(s[h].astype(BF16), vb[h]) + w_inter[h] * qc[h] for h in heads}
    den = {h: jnp.sum(s[h], axis=1, keepdims=True) + w_inter[h] * qn[h] for h in heads}
    hh = {h: num[h] / jnp.maximum(jnp.abs(den[h]), jnp.exp(-m_t[h])) for h in heads}
    mean = {h: jnp.mean(hh[h], axis=1, keepdims=True) for h in heads}
    d = {h: hh[h] - mean[h] for h in heads}
    var = {h: jnp.mean(d[h] * d[h], axis=1, keepdims=True) for h in heads}
    for h in heads:
        hn = d[h] * lax.rsqrt(var[h] + MH_EPS)
        om_ref[0, :, sl(h)] = (_sigmoid(o_ref[0, :, sl(h)].astype(F32)) * (hn * nw_ref[:, sl(h)])).astype(BF16)
    m_new = {h: m_t[h][lc - 1:lc, :] for h in heads}
    bl = {h: bc[h][lc - 1:lc, :] for h in heads}
    g_state = {h: jnp.exp(bl[h] + m_prev[h] - m_new[h]) for h in heads}
    g_src = {h: jnp.exp(bl[h] - bc[h] + ig[h] - m_new[h]) for h in heads}
    for h in heads:
        c_ref[0, h] = g_state[h] * c_prev[h] + _dot_tn((g_src[h] * vb[h].astype(F32)).astype(BF16), k[h])
        n_ref[0, h:h + 1, :] = g_state[h] * n_prev[h] + jnp.sum(g_src[h] * k[h].astype(F32), axis=0, keepdims=True)
        m_new_all = jnp.where(mlane == h, m_new[h], m_new_all)
    m_ref[0] = m_new_all


def _mlstm(qk, z, zif, c0, n0, m0, if_bias, norm_w):
    b, t, _ = z.shape
    lc = min(256, t)
    w = M_HEADS * M_HD
    return pl.pallas_call(
        functools.partial(_mlstm_kernel, lc=lc),
        grid=(b, t // lc),
        in_specs=[
            pl.BlockSpec((1, lc, 2 * w), lambda i, j: (i, j, 0)),
            pl.BlockSpec((1, lc, w), lambda i, j: (i, j, ZC_MV * V7X_LANES // w)),
            pl.BlockSpec((1, lc, w), lambda i, j: (i, j, ZC_MO * V7X_LANES // w)),
            pl.BlockSpec((1, lc, V7X_LANES), lambda i, j: (i, j, 0)),
            pl.BlockSpec((1, M_HEADS, M_HD, M_HD), lambda i, j: (i, 0, 0, 0)),
            pl.BlockSpec((1, M_HEADS, M_HD), lambda i, j: (i, 0, 0)),
            pl.BlockSpec((1, 1, M_HEADS), lambda i, j: (i, 0, 0)),
            pl.BlockSpec((1, V7X_LANES), lambda i, j: (0, 0)),
            pl.BlockSpec((1, w), lambda i, j: (0, 0)),
        ],
        out_specs=[
            pl.BlockSpec((1, lc, w), lambda i, j: (i, j, 0)),
            pl.BlockSpec((1, M_HEADS, M_HD, M_HD), lambda i, j: (i, 0, 0, 0)),
            pl.BlockSpec((1, M_HEADS, M_HD), lambda i, j: (i, 0, 0)),
            pl.BlockSpec((1, 1, M_HEADS), lambda i, j: (i, 0, 0)),
        ],
        out_shape=[
            jax.ShapeDtypeStruct((b, t, w), BF16),
            jax.ShapeDtypeStruct((b, M_HEADS, M_HD, M_HD), F32),
            jax.ShapeDtypeStruct((b, M_HEADS, M_HD), F32),
            jax.ShapeDtypeStruct((b, 1, M_HEADS), F32),
        ],
        compiler_params=_cparams(("parallel", "arbitrary")),
        name="mlstm",
    )(qk, z, z, zif, c0, n0, m0.reshape(b, 1, M_HEADS), if_bias, norm_w.reshape(1, w))


def _rwkv_mix_kernel(zr_ref, zk_ref, zv_ref, zwa_ref, zgd_ref, sh_ref, mu_ref, w0_ref, w2_ref, a0_ref, a2_ref, g2_ref,
                     kk_ref, ka_ref, s0_ref, lnw_ref, lnb_ref, rk_ref, e2_ref,
                     o_ref, s_ref, y_ref, car_ref, *, npair, tt, lc, rg):
    @pl.when(pl.program_id(2) == 0)
    def _():
        s_ref[...] = s0_ref[...]
        car_ref[...] = sh_ref[0]

    def seg(x):
        return _dot(x.astype(BF16), e2_ref[...])

    first_row = lax.broadcasted_iota(jnp.int32, (tt, 1), 0) == 0
    pairs = range(npair)
    ps = lambda p: slice(p * V7X_LANES, (p + 1) * V7X_LANES)
    hs = lambda h: slice(h * R_HD, (h + 1) * R_HD)

    def shifted(zp, p, i):
        zp = zp.astype(F32)
        prev = jnp.where(first_row, car_ref[p, i:i + 1, :], pltpu.roll(zp, 1, axis=0))
        car_ref[p, i:i + 1, :] = zp[tt - 1:tt, :]
        return zp + (prev - zp) * mu_ref[p, i:i + 1, :]

    xwa = shifted(zwa_ref[0], 0, 3)
    xgd = shifted(zgd_ref[0], 0, 4)
    twa_b = jnp.tanh(xwa).astype(BF16)
    xwa_b = xwa.astype(BF16)
    sgd_b = _sigmoid(xgd).astype(BF16)
    r_all, k_all, v_all, lw_all, a_all, b_all, g_all = {}, {}, {}, {}, {}, {}, {}

    def prologue(p):
        r_all[p] = shifted(zr_ref[0, :, ps(p)], p, 0)
        k_in = shifted(zk_ref[0, :, ps(p)], p, 1)
        v_all[p] = shifted(zv_ref[0, :, ps(p)], p, 2)
        w_log = -_softplus(-(w0_ref[p] + _dot(twa_b, w2_ref[:, ps(p)]))) - 0.5
        lw_all[p] = -jnp.exp(w_log)
        a_gate = _sigmoid(a0_ref[p] + _dot(xwa_b, a2_ref[:, ps(p)]))
        g_all[p] = _dot(sgd_b, g2_ref[:, ps(p)])
        kk = k_in * kk_ref[p]
        kk = kk * lax.rsqrt(jnp.maximum(seg(kk * kk), 1e-24))
        k_all[p] = k_in * (1.0 + (a_gate - 1.0) * ka_ref[p])
        a_all[p] = -kk
        b_all[p] = kk * a_gate

    nb = rg // lc
    urow = lax.broadcasted_iota(jnp.int32, (2 * rg, 2 * rg), 0)
    scol = lax.broadcasted_iota(jnp.int32, (2 * rg, 2 * rg), 1)
    incl = urow >= rg
    trow = jnp.where(incl, urow - rg, urow)
    scol = jnp.where(scol >= rg, scol - rg, scol)
    dist = trow - scol
    gmask = (dist >= jnp.where(incl, 0, 1)) & (dist <= (trow & (lc - 1)))
    lsh = lc.bit_length() - 1
    qrow = lax.broadcasted_iota(jnp.int32, (rg, rg), 0)
    qcol = lax.broadcasted_iota(jnp.int32, (rg, rg), 1)
    same = (qrow >> lsh) == (qcol >> lsh)
    blk = jnp.where(same, 1.0, 0.0).astype(BF16)
    tril = jnp.where(qcol <= qrow, blk, jnp.zeros((), BF16))
    xrow = lax.broadcasted_iota(jnp.int32, (rg, nb * R_HD), 0)
    xcol = lax.broadcasted_iota(jnp.int32, (rg, nb * R_HD), 1)
    xmask = (xrow >> (lc.bit_length() - 1)) == (xcol >> (R_HD.bit_length() - 1))
    n_sq = max(lc.bit_length() - 1, 0)

    groups = range(tt // rg)
    ew = {}

    def prepare(p):
        for g in groups:
            rows = slice(g * rg, (g + 1) * rg)
            lw = lw_all[p][rows]
            k = k_all[p][rows]
            b = b_all[p][rows]
            lw_hi = lw.astype(BF16)
            lw_lo = (lw - lw_hi.astype(F32)).astype(BF16)
            cum = _dot(tril, lw_hi) + _dot(tril, lw_lo)
            ends = [cum[c * lc + lc - 1:c * lc + lc, :] for c in range(nb)]
            cum_end = ends[0] if nb == 1 else jnp.concatenate(
                [jnp.broadcast_to(e, (lc, V7X_LANES)) for e in ends], axis=0)
            e_neg = jnp.exp(-cum)
            at = a_all[p][rows] * jnp.exp(cum - lw)
            rt = r_all[p][rows] * jnp.exp(cum)
            e_end = jnp.exp(cum_end - cum)
            ew[p, g] = dict(at=at, rt=rt, lhs=jnp.concatenate([at, rt], axis=0).astype(BF16),
                            rhs=jnp.concatenate([b * e_neg, k * e_neg], axis=0).astype(BF16),
                            be=(b * e_end).astype(BF16), ke=(k * e_end).astype(BF16),
                            vb=v_all[p][rows].astype(BF16),
                            w_end=[jnp.exp(e) for e in ends])

    def expand(x):
        if nb == 1:
            return x
        return jnp.where(xmask, jnp.concatenate([x] * nb, axis=1), jnp.zeros((), x.dtype))

    piece = lambda c, name: ew[c[0], c[1]][name][:, hs(c[2])]
    pq2, xe, vk = {}, {}, {}

    def solve(chains):
        gm = {c: jnp.where(gmask, _dot_nt(piece(c, "lhs"), piece(c, "rhs")), 0.0) for c in chains}
        gv = {c: _dot(gm[c][:, rg:].astype(BF16), piece(c, "vb")) for c in chains}
        gr = {c: gm[c][rg:] for c in chains}
        gvr = {c: gv[c][rg:] for c in chains}
        x = {c: jnp.concatenate([piece(c, "at"), gv[c][:rg]], axis=1) for c in chains}
        pw = {c: gm[c][:rg, :rg] for c in chains}
        for i in range(n_sq):
            pwb = {c: pw[c].astype(BF16) for c in chains}
            x = {c: x[c] + _dot(pwb[c], x[c].astype(BF16)) for c in chains}
            if i + 1 < n_sq:
                pw = {c: _dot(pwb[c], pwb[c]) for c in chains}
        xb = {c: x[c].astype(BF16) for c in chains}
        for c in chains:
            pq2[c] = jnp.concatenate([piece(c, "rt"), gvr[c]], axis=1) + _dot(gr[c][:, :rg].astype(BF16), xb[c])
            xe[c] = _dot_tn(xb[c], expand(piece(c, "be")))
            vk[c] = _dot_tn(piece(c, "vb"), expand(piece(c, "ke")))

    for p in pairs:
        prologue(p)
    for p in pairs:
        prepare(p)
    solve([(p, g, h) for p in pairs for g in groups for h in range(2)])

    s = {(p, h): s_ref[0, 2 * p + h] for p in pairs for h in range(2)}
    for g in groups:
        for c in range(nb):
            rows = slice(c * lc, (c + 1) * lc)
            cols = slice(c * R_HD, (c + 1) * R_HD)
            sb = {ph: s[ph].astype(BF16) for ph in s}
            for p in pairs:
                ys = [_dot_nt(pq2[p, g, h][rows, :R_HD].astype(BF16), sb[p, h]) + pq2[p, g, h][rows, R_HD:]
                      for h in range(2)]
                y_ref[g * rg + c * lc:g * rg + (c + 1) * lc, ps(p)] = jnp.concatenate(ys, axis=1)
            s = {(p, h): (s[p, h] * ew[p, g]["w_end"][c][:, hs(h)]
                          + _dot(sb[p, h], xe[p, g, h][:R_HD, cols].astype(BF16))
                          + (xe[p, g, h][R_HD:, cols] + vk[p, g, h][:, cols])) for (p, h) in s}
    for (p, h) in s:
        s_ref[0, 2 * p + h] = s[p, h]

    for p in pairs:
        y = y_ref[:, ps(p)]
        d = y - seg(y) * (1.0 / R_HD)
        var = seg(d * d) * (1.0 / R_HD)
        yn = (d * lax.rsqrt(var + RWKV_GN_EPS)) * lnw_ref[p] + lnb_ref[p]
        yn = yn + seg((r_all[p] * k_all[p]) * rk_ref[p]) * v_all[p]
        o_ref[0, :, ps(p)] = (yn * g_all[p]).astype(BF16)


def _rwkv_mix(z, shift_prev, s0, mu, w0, w2p, a0, a2p, g2, k_k, k_a, ln_w, ln_b, r_k, e2):
    bsz, t, _ = z.shape
    tt = min(RWKV_TT, t)
    lc = min(RWKV_CHUNK, t)
    rg = min(RWKV_GROUP, t)
    npair = RWKV_PAIRS_PER_STEP
    wide = npair * V7X_LANES
    zpair = lambda blk: pl.BlockSpec((1, tt, wide), lambda i, p, j: (i, j, blk // npair + p))
    zshared = lambda blk: pl.BlockSpec((1, tt, V7X_LANES), lambda i, p, j: (i, j, blk))
    par_spec = pl.BlockSpec((npair, 1, V7X_LANES), lambda i, p, j: (p, 0, 0))
    lora_spec = pl.BlockSpec((V7X_LANES, wide), lambda i, p, j: (0, p))
    st_spec = pl.BlockSpec((1, 2 * npair, R_HD, R_HD), lambda i, p, j: (i, p, 0, 0))
    pair = lambda x: x.reshape(R_PAIRS, 1, V7X_LANES)

    def pieces(x):
        lead = x.shape[:-1]
        rkv = x[..., :3 * R_W].reshape(lead + (3, R_PAIRS, V7X_LANES))
        rkv = jnp.moveaxis(rkv, -3, -2)
        lora = jnp.broadcast_to(x[..., 3 * R_W:].reshape(lead + (1, 2, V7X_LANES)), lead + (R_PAIRS, 2, V7X_LANES))
        return jnp.concatenate([rkv, lora], axis=-2)

    return pl.pallas_call(
        functools.partial(_rwkv_mix_kernel, npair=npair, tt=tt, lc=lc, rg=rg),
        grid=(bsz, R_PAIRS // npair, t // tt),
        in_specs=[
            zpair(ZC_RW), zpair(ZC_RW + 8), zpair(ZC_RW + 16), zshared(ZC_RW + 24), zshared(ZC_RW + 25),
            pl.BlockSpec((1, npair, 5, V7X_LANES), lambda i, p, j: (i, p, 0, 0)),
            pl.BlockSpec((npair, 5, V7X_LANES), lambda i, p, j: (p, 0, 0)),
            par_spec, lora_spec, par_spec, lora_spec, lora_spec, par_spec, par_spec,
            st_spec, par_spec, par_spec, par_spec,
            pl.BlockSpec((V7X_LANES, V7X_LANES), lambda i, p, j: (0, 0)),
        ],
        out_specs=[
            pl.BlockSpec((1, tt, wide), lambda i, p, j: (i, j, p)),
            st_spec,
        ],
        out_shape=[
            jax.ShapeDtypeStruct((bsz, t, R_W), BF16),
            jax.ShapeDtypeStruct((bsz, R_HEADS, R_HD, R_HD), F32),
        ],
        scratch_shapes=[pltpu.VMEM((tt, wide), F32), pltpu.VMEM((npair, 5, V7X_LANES), F32)],
        compiler_params=_cparams(("parallel", "parallel", "arbitrary")),
        name="rwkv_mix",
    )(z, z, z, z, z, pieces(shift_prev.reshape(bsz, R_IN)), pieces(mu), pair(w0), w2p, pair(a0), a2p, g2,
      pair(k_k), pair(k_a), s0, pair(ln_w), pair(ln_b), pair(r_k), e2)


def _merge_kernel(x_ref, zg_ref, hl_ref, om_ref, or_ref, wb_ref, wo_ref, o_ref):
    acc = None
    for gidx, br in enumerate((hl_ref, om_ref, or_ref)):
        sl = slice(gidx * D_MODEL, (gidx + 1) * D_MODEL)
        term = _sigmoid(zg_ref[:, sl].astype(F32)) * _dot(br[...], wb_ref[gidx])
        acc = term if acc is None else acc + term
    o_ref[...] = x_ref[...] + _dot(acc.astype(BF16), wo_ref[...])


def _merge(x, z2d, hl, om, orw, wb, wo):
    m = x.shape[0]
    tm = min(512, m)
    tok = pl.BlockSpec((tm, D_MODEL), lambda i: (i, 0))
    return pl.pallas_call(
        _merge_kernel,
        grid=(m // tm,),
        in_specs=[
            tok,
            pl.BlockSpec((tm, 3 * D_MODEL), lambda i: (i, ZC_GATE * V7X_LANES // (3 * D_MODEL))),
            tok, tok, tok,
            pl.BlockSpec((3, D_MODEL, D_MODEL), lambda i: (0, 0, 0)),
            pl.BlockSpec((D_MODEL, D_MODEL), lambda i: (0, 0)),
        ],
        out_specs=tok,
        out_shape=jax.ShapeDtypeStruct((m, D_MODEL), F32),
        compiler_params=_cparams(("parallel",)),
        name="merge",
    )(x, z2d, hl, om, orw, wb, wo)


def _prep_layer_weights(p, l):
    w_in = p["w_in"][l]
    c0, c1, c2, c3, c4 = 3072, 4096, 5120, 5128, 8456
    w_if = jnp.pad(w_in[:, c2:c3], ((0, 0), (0, V7X_LANES - 2 * M_HEADS))).astype(BF16)
    w_all = jnp.concatenate([w_in[:, :c0], w_in[:, c4:], w_in[:, c0:c1], w_in[:, c1:c2], w_in[:, c3:c4]],
                            axis=1).astype(BF16)
    def pairs(w):
        z = jnp.zeros((LRU_BD, LRU_BD), F32)
        return jnp.stack([jnp.block([[w[2 * i], z], [z, w[2 * i + 1]]]) for i in range(LRU_BLOCKS // 2)])
    w2 = jnp.concatenate([pairs(p["lru_wa"][l]), pairs(p["lru_wx"][l])], axis=2).astype(BF16)
    bax = jnp.stack([p["lru_ba"][l], p["lru_bx"][l]])
    if_bias = jnp.pad(p["mlstm_if_bias"][l], (0, V7X_LANES - 2 * M_HEADS)).reshape(1, V7X_LANES)
    zpad = jnp.zeros((64, R_W), F32)
    w2p = jnp.concatenate([p["rwkv_w2"][l], zpad], axis=0).astype(BF16)
    a2p = jnp.concatenate([zpad, p["rwkv_a2"][l]], axis=0).astype(BF16)
    return dict(
        w_all=w_all, w_if=w_if, w2=w2, bax=bax, if_bias=if_bias, w2p=w2p, a2p=a2p,
        g2=p["rwkv_g2"][l].astype(BF16),
        wb=p["w_branch"][l].astype(BF16), wo=p["w_out"][l].astype(BF16),
        f1=(p["ffn1_w_gate"][l].astype(BF16), p["ffn1_w_up"][l].astype(BF16), p["ffn1_w_down"][l].astype(BF16)),
        f2=(p["ffn2_w_gate"][l].astype(BF16), p["ffn2_w_up"][l].astype(BF16), p["ffn2_w_down"][l].astype(BF16)),
    )


def _segment_matrix():
    half = jnp.arange(V7X_LANES) // R_HD
    return (half[:, None] == half[None, :]).astype(BF16)


def _run_group(x, states, p, lw_list, final_norm):
    bsz, t, _ = x.shape
    m = bsz * t
    e2 = _segment_matrix()
    xf = x.reshape(m, D_MODEL)
    new_states = []
    depth = p["w_in"].shape[0]
    for l in range(depth):
        lw = lw_list[l]
        conv_prev, lru_h, m_c, m_n, m_m, shift_prev, rwkv_s = states[l]
        xf = _ffn(xf, p["ffn1_norm"][l], *lw["f1"])
        z2d, zif = _inproj(xf, p["mix_norm"][l], lw["w_all"], lw["w_if"])
        z = z2d.reshape(bsz, t, Z_W)
        hl, qk, h_last = _conv_lru(z, conv_prev, lru_h, p["conv_w"][l], p["conv_b"][l], lw["w2"], lw["bax"],
                                   p["lru_lambda"][l])
        om, c_new, n_new, m_new = _mlstm(qk, z, zif.reshape(bsz, t, V7X_LANES), m_c, m_n, m_m, lw["if_bias"],
                                         p["mlstm_norm"][l])
        orw, s_new = _rwkv_mix(z, shift_prev, rwkv_s, p["rwkv_mu"][l], p["rwkv_w0"][l], lw["w2p"], p["rwkv_a0"][l],
                               lw["a2p"], lw["g2"], p["rwkv_k_k"][l], p["rwkv_k_a"][l], p["rwkv_ln_w"][l],
                               p["rwkv_ln_b"][l], p["rwkv_r_k"][l].reshape(-1), e2)
        xf = _merge(xf, z2d, hl.reshape(m, D_MODEL), om.reshape(m, D_MODEL), orw.reshape(m, D_MODEL),
                    lw["wb"], lw["wo"])
        xf = _ffn(xf, p["ffn2_norm"][l], *lw["f2"], final_g=final_norm if l == depth - 1 else None)
        conv_new = z[:, t - (CONV_W - 1):, ZC_CONV * V7X_LANES:ZC_CONV * V7X_LANES + CONV_CH].astype(F32)
        shift_new = z[:, t - 1:, ZC_RW * V7X_LANES:ZC_RW * V7X_LANES + R_IN].astype(F32)
        new_states.append((conv_new, h_last.reshape(bsz, LRU_W), c_new, n_new, m_new.reshape(bsz, M_HEADS),
                           shift_new, s_new))
    stacked = tuple(jnp.stack([st[i] for st in new_states]) for i in range(7))
    return xf.reshape(bsz, t, D_MODEL), stacked


def _zero_states(bsz):
    return (jnp.zeros((bsz, CONV_W - 1, CONV_CH), F32), jnp.zeros((bsz, LRU_W), F32),
            jnp.zeros((bsz, M_HEADS, M_HD, M_HD), F32), jnp.zeros((bsz, M_HEADS, M_HD), F32),
            jnp.zeros((bsz, M_HEADS), F32), jnp.zeros((bsz, 1, R_IN), F32),
            jnp.zeros((bsz, R_HEADS, R_HD, R_HD), F32))


def kernel(x_prompt, x_sample, state_conv, state_lru_h, state_mlstm_C, state_mlstm_n, state_mlstm_m, state_rwkv_shift, state_rwkv_S, ffn1_norm, ffn1_w_gate, ffn1_w_up, ffn1_w_down, mix_norm, w_in, conv_w, conv_b, lru_wa, lru_ba, lru_wx, lru_bx, lru_lambda, mlstm_if_bias, mlstm_norm, rwkv_mu, rwkv_w0, rwkv_w2, rwkv_a0, rwkv_a2, rwkv_g2, rwkv_k_k, rwkv_k_a, rwkv_r_k, rwkv_ln_w, rwkv_ln_b, w_branch, w_out, ffn2_norm, ffn2_w_gate, ffn2_w_up, ffn2_w_down, final_norm):
    p = dict(ffn1_norm=ffn1_norm, ffn1_w_gate=ffn1_w_gate, ffn1_w_up=ffn1_w_up, ffn1_w_down=ffn1_w_down,
             mix_norm=mix_norm, w_in=w_in, conv_w=conv_w, conv_b=conv_b,
             lru_wa=lru_wa, lru_ba=lru_ba, lru_wx=lru_wx, lru_bx=lru_bx, lru_lambda=lru_lambda,
             mlstm_if_bias=mlstm_if_bias, mlstm_norm=mlstm_norm,
             rwkv_mu=rwkv_mu, rwkv_w0=rwkv_w0, rwkv_w2=rwkv_w2, rwkv_a0=rwkv_a0, rwkv_a2=rwkv_a2,
             rwkv_g2=rwkv_g2, rwkv_k_k=rwkv_k_k, rwkv_k_a=rwkv_k_a, rwkv_r_k=rwkv_r_k,
             rwkv_ln_w=rwkv_ln_w, rwkv_ln_b=rwkv_ln_b, w_branch=w_branch, w_out=w_out,
             ffn2_norm=ffn2_norm, ffn2_w_gate=ffn2_w_gate, ffn2_w_up=ffn2_w_up, ffn2_w_down=ffn2_w_down)
    depth = w_in.shape[0]
    lw_list = [_prep_layer_weights(p, l) for l in range(depth)]
    states_p = [_zero_states(x_prompt.shape[0]) for _ in range(depth)]
    y_p, st_p = _run_group(x_prompt, states_p, p, lw_list, final_norm)
    states_s = [(state_conv[l], state_lru_h[l], state_mlstm_C[l], state_mlstm_n[l], state_mlstm_m[l],
                 state_rwkv_shift[l], state_rwkv_S[l]) for l in range(depth)]
    y_s, st_s = _run_group(x_sample, states_s, p, lw_list, final_norm)
    return (y_p, y_s) + st_p + st_s
```

```python
import functools

import jax
import jax.numpy as jnp
from jax import lax
from jax.experimental import pallas as pl
from jax.experimental.pallas import tpu as pltpu

F32 = jnp.float32
BF16 = jnp.bfloat16

V7X_LANES = 128
V7X_SUBLANES = 8
V7X_VMEM_BYTES = 64 * 1024 * 1024
VMEM_LIMIT = 56 * 1024 * 1024

D_MODEL = 1024
D_FF = 2816
CONV_W = 4
LRU_W = 1024
LRU_BLOCKS = 8
LRU_BD = LRU_W // LRU_BLOCKS
LRU_PAIR = 2 * LRU_BD
LRU_C = 8.0
M_HEADS = 4
M_HD = 256
R_HD = 64
R_HEADS = 16
R_PAIRS = R_HEADS // 2
R_W = 1024
R_IN = 3 * R_W + 256
CONV_CH = 3 * 1024
RMS_EPS = 1e-6
MH_EPS = 1e-6
RWKV_GN_EPS = 64e-5
RWKV_DECAY_SCALE = 0.6065306597126334

ZC_CONV = 0
ZC_GATE = 24
ZC_MV = 48
ZC_MO = 56
ZC_RW = 64
Z_BLOCKS = 90
Z_W = Z_BLOCKS * V7X_LANES
Z_TN = 18 * V7X_LANES

FFN_TF = D_FF // 2
FFN_SUB = ((0, 768), (768, FFN_TF))
CONV_ROWS = 128
RWKV_CHUNK = 64
RWKV_GROUP = 128
RWKV_TT = 256
RWKV_PAIRS_PER_STEP = 8

HI = lax.Precision.HIGHEST


def _cparams(sem):
    return pltpu.CompilerParams(dimension_semantics=sem, vmem_limit_bytes=VMEM_LIMIT)


def _rms(x, g):
    return (x * lax.rsqrt(jnp.mean(x * x, axis=-1, keepdims=True) + RMS_EPS)) * g


def _softplus(x):
    return jnp.maximum(x, 0.0) + jnp.log1p(jnp.exp(-jnp.abs(x)))


def _sigmoid(x):
    return jax.nn.sigmoid(x)


def _dot(a, b, prec=None):
    return jnp.dot(a, b, preferred_element_type=F32, precision=prec)


def _dot_nt(a, b, prec=None):
    return lax.dot_general(a, b, (((1,), (1,)), ((), ())), preferred_element_type=F32, precision=prec)


def _dot_tn(a, b, prec=None):
    return lax.dot_general(a, b, (((0,), (0,)), ((), ())), preferred_element_type=F32, precision=prec)


def _shift_rows(x, d, row):
    return jnp.where(row >= d, pltpu.roll(x, d, axis=0), 0.0)


def _cumsum_rows(x):
    n = x.shape[0]
    row = lax.broadcasted_iota(jnp.int32, x.shape, 0)
    d = 1
    while d < n:
        x = x + _shift_rows(x, d, row)
        d *= 2
    return x


def _ffn_kernel(x_ref, g_ref, wg_ref, wu_ref, wd_ref, *rest, final_norm):
    if final_norm:
        fg_ref, o_ref, xn_ref = rest
    else:
        o_ref, xn_ref = rest
    k = pl.program_id(1)

    @pl.when(k == 0)
    def _():
        xn_ref[...] = _rms(x_ref[...], g_ref[...]).astype(BF16)

    xn = xn_ref[...]
    part = None
    for c0, c1 in FFN_SUB:
        hg = _dot(xn, wg_ref[:, c0:c1])
        hu = _dot(xn, wu_ref[:, c0:c1])
        h = (hg * _sigmoid(hg)) * hu
        d = _dot(h.astype(BF16), wd_ref[c0:c1, :])
        part = d if part is None else part + d

    @pl.when(k == 0)
    def _():
        o_ref[...] = part

    @pl.when(k == 1)
    def _():
        y = x_ref[...] + 0.5 * (o_ref[...] + part)
        if final_norm:
            y = _rms(y, fg_ref[...])
        o_ref[...] = y


def _ffn(x, g, wg, wu, wd, final_g=None):
    m = x.shape[0]
    tm = min(1024, m)
    nk = D_FF // FFN_TF
    assert nk == 2
    in_specs = [
        pl.BlockSpec((tm, D_MODEL), lambda i, k: (i, 0)),
        pl.BlockSpec((1, D_MODEL), lambda i, k: (0, 0)),
        pl.BlockSpec((D_MODEL, FFN_TF), lambda i, k: (0, k)),
        pl.BlockSpec((D_MODEL, FFN_TF), lambda i, k: (0, k)),
        pl.BlockSpec((FFN_TF, D_MODEL), lambda i, k: (k, 0)),
    ]
    args = [x, g.reshape(1, D_MODEL), wg, wu, wd]
    if final_g is not None:
        in_specs.append(pl.BlockSpec((1, D_MODEL), lambda i, k: (0, 0)))
        args.append(final_g.reshape(1, D_MODEL))
    return pl.pallas_call(
        functools.partial(_ffn_kernel, final_norm=final_g is not None),
        grid=(m // tm, nk),
        in_specs=in_specs,
        out_specs=pl.BlockSpec((tm, D_MODEL), lambda i, k: (i, 0)),
        out_shape=jax.ShapeDtypeStruct((m, D_MODEL), F32),
        scratch_shapes=[pltpu.VMEM((tm, D_MODEL), BF16)],
        compiler_params=_cparams(("parallel", "arbitrary")),
        name="ffn",
    )(*args)


def _inproj_kernel(x_ref, g_ref, w_ref, wif_ref, o_ref, oif_ref, xn_ref):
    @pl.when(pl.program_id(1) == 0)
    def _():
        xn = _rms(x_ref[...], g_ref[...]).astype(BF16)
        xn_ref[...] = xn
        oif_ref[...] = _dot(xn, wif_ref[...])

    o_ref[...] = _dot(xn_ref[...], w_ref[...]).astype(BF16)


def _inproj(x, g, w_all, w_if):
    m = x.shape[0]
    tm = min(1024, m)
    return pl.pallas_call(
        _inproj_kernel,
        grid=(m // tm, Z_W // Z_TN),
        in_specs=[
            pl.BlockSpec((tm, D_MODEL), lambda i, j: (i, 0)),
            pl.BlockSpec((1, D_MODEL), lambda i, j: (0, 0)),
            pl.BlockSpec((D_MODEL, Z_TN), lambda i, j: (0, j)),
            pl.BlockSpec((D_MODEL, V7X_LANES), lambda i, j: (0, 0)),
        ],
        out_specs=[
            pl.BlockSpec((tm, Z_TN), lambda i, j: (i, j)),
            pl.BlockSpec((tm, V7X_LANES), lambda i, j: (i, 0)),
        ],
        out_shape=[
            jax.ShapeDtypeStruct((m, Z_W), BF16),
            jax.ShapeDtypeStruct((m, V7X_LANES), F32),
        ],
        scratch_shapes=[pltpu.VMEM((tm, D_MODEL), BF16)],
        compiler_params=_cparams(("parallel", "arbitrary")),
        name="in_proj",
    )(x, g.reshape(1, D_MODEL), w_all, w_if)


def _conv_lru_kernel(z_ref, cprev_ref, h0_ref, cw_ref, cb_ref, w2_ref, bax_ref, lam_ref,
                     hl_ref, qk_ref, hlast_ref, xp_ref, a_ref, b_ref, hs_ref, hc_ref, *, tt):
    pad = V7X_SUBLANES

    @pl.when(pl.program_id(1) == 0)
    def _():
        xp_ref[0:pad, :] = jnp.zeros((pad, CONV_CH), F32)
        xp_ref[pad - 3:pad, :] = cprev_ref[0]
        hc_ref[...] = h0_ref[0]

    xp_ref[pad:pad + tt, :] = z_ref[0].astype(F32)
    sp = _softplus(-lam_ref[...])
    rb = min(CONV_ROWS, tt)
    for r0 in range(0, tt, rb):
        rows = slice(r0, r0 + rb)
        for cblk in range(CONV_CH // LRU_PAIR):
            cols = slice(cblk * LRU_PAIR, (cblk + 1) * LRU_PAIR)
            xe = xp_ref[r0:pad + r0 + rb, cols]
            c = xe * cw_ref[0:1, cols] + cb_ref[:, cols]
            c = pltpu.roll(c, 1, axis=0) + xe * cw_ref[1:2, cols]
            c = pltpu.roll(c, 1, axis=0) + xe * cw_ref[2:3, cols]
            c = (pltpu.roll(c, 1, axis=0) + xe * cw_ref[3:4, cols])[pad:]
            if cblk < LRU_W // LRU_PAIR:
                gpre = _dot(c.astype(BF16), w2_ref[cblk])
                r = _sigmoid(gpre[:, :LRU_PAIR] + bax_ref[0:1, cols])
                i = _sigmoid(gpre[:, LRU_PAIR:] + bax_ref[1:2, cols])
                log_a = (-LRU_C * r) * sp[:, cols]
                a = jnp.exp(log_a)
                mult = jnp.sqrt(-jnp.tanh(log_a) * (a * a + 1.0))
                a_ref[rows, cols] = a
                b_ref[rows, cols] = mult * (i * c)
            elif cblk < 2 * LRU_W // LRU_PAIR:
                qk_ref[0, rows, cblk * LRU_PAIR - LRU_W:(cblk + 1) * LRU_PAIR - LRU_W] = (
                    c * _sigmoid(c)).astype(BF16)
            else:
                qk_ref[0, rows, cblk * LRU_PAIR - LRU_W:(cblk + 1) * LRU_PAIR - LRU_W] = (
                    (c * _sigmoid(c)) * (M_HD ** -0.5)).astype(BF16)
    xp_ref[pad - 3:pad, :] = xp_ref[pad + tt - 3:pad + tt, :]

    def body(t, h):
        h = a_ref[pl.ds(t, 1), :] * h + b_ref[pl.ds(t, 1), :]
        hs_ref[pl.ds(t, 1), :] = h
        return h

    h = lax.fori_loop(0, tt, body, hc_ref[...], unroll=8)
    hc_ref[...] = h
    hlast_ref[0] = h
    hl_ref[0] = hs_ref[...].astype(BF16)


def _conv_lru(z, conv_prev, h0, cw, cb, w2, bax, lam):
    b, t, _ = z.shape
    tt = min(256, t)
    return pl.pallas_call(
        functools.partial(_conv_lru_kernel, tt=tt),
        grid=(b, t // tt),
        in_specs=[
            pl.BlockSpec((1, tt, CONV_CH), lambda i, j: (i, j, ZC_CONV * V7X_LANES // CONV_CH)),
            pl.BlockSpec((1, CONV_W - 1, CONV_CH), lambda i, j: (i, 0, 0)),
            pl.BlockSpec((1, 1, LRU_W), lambda i, j: (i, 0, 0)),
            pl.BlockSpec((CONV_W, CONV_CH), lambda i, j: (0, 0)),
            pl.BlockSpec((1, CONV_CH), lambda i, j: (0, 0)),
            pl.BlockSpec((LRU_W // LRU_PAIR, LRU_PAIR, 2 * LRU_PAIR), lambda i, j: (0, 0, 0)),
            pl.BlockSpec((2, LRU_W), lambda i, j: (0, 0)),
            pl.BlockSpec((1, LRU_W), lambda i, j: (0, 0)),
        ],
        out_specs=[
            pl.BlockSpec((1, tt, LRU_W), lambda i, j: (i, j, 0)),
            pl.BlockSpec((1, tt, 2048), lambda i, j: (i, j, 0)),
            pl.BlockSpec((1, 1, LRU_W), lambda i, j: (i, 0, 0)),
        ],
        out_shape=[
            jax.ShapeDtypeStruct((b, t, LRU_W), BF16),
            jax.ShapeDtypeStruct((b, t, 2048), BF16),
            jax.ShapeDtypeStruct((b, 1, LRU_W), F32),
        ],
        scratch_shapes=[
            pltpu.VMEM((tt + V7X_SUBLANES, CONV_CH), F32),
            pltpu.VMEM((tt, LRU_W), F32),
            pltpu.VMEM((tt, LRU_W), F32),
            pltpu.VMEM((tt, LRU_W), F32),
            pltpu.VMEM((1, LRU_W), F32),
        ],
        compiler_params=_cparams(("parallel", "arbitrary")),
        name="conv_lru",
    )(z, conv_prev, h0.reshape(b, 1, LRU_W), cw, cb.reshape(1, CONV_CH), w2, bax, lam.reshape(1, LRU_W))


def _mlstm_kernel(qk_ref, v_ref, o_ref, if_ref, c0_ref, n0_ref, m0_ref, ifb_ref, nw_ref,
                  om_ref, c_ref, n_ref, m_ref, *, lc):
    @pl.when(pl.program_id(1) == 0)
    def _():
        c_ref[...] = c0_ref[...]
        n_ref[...] = n0_ref[...]
        m_ref[...] = m0_ref[...]

    gif = if_ref[0] + ifb_ref[...]
    lf = jnp.minimum(gif, 0.0) - jnp.log1p(jnp.exp(-jnp.abs(gif)))
    cum = _cumsum_rows(lf)
    src = gif - pltpu.roll(cum, V7X_LANES - M_HEADS, axis=1)
    src_t = src.T if lc % V7X_LANES == 0 else None
    lane = lax.broadcasted_iota(jnp.int32, (lc, V7X_LANES), 1)
    tpos = lax.broadcasted_iota(jnp.int32, (lc, lc), 0)
    spos = lax.broadcasted_iota(jnp.int32, (lc, lc), 1)
    causal = spos <= tpos
    m_prev_all = m_ref[0]
    mlane = lax.broadcasted_iota(jnp.int32, (1, M_HEADS), 1)
    m_new_all = m_prev_all

    heads = range(M_HEADS)
    sl = lambda h: slice(h * M_HD, (h + 1) * M_HD)
    q = {h: qk_ref[0, :, sl(h)] for h in heads}
    k = {h: qk_ref[0, :, M_HEADS * M_HD + h * M_HD:M_HEADS * M_HD + (h + 1) * M_HD] for h in heads}
    vb = {h: v_ref[0, :, sl(h)] for h in heads}
    bc = {h: cum[:, M_HEADS + h:M_HEADS + h + 1] for h in heads}
    ig = {h: gif[:, h:h + 1] for h in heads}
    m_prev = {h: m_prev_all[:, h:h + 1] for h in heads}
    c_prev = {h: c_ref[0, h] for h in heads}
    n_prev = {h: n_ref[0, h:h + 1, :] for h in heads}
    if src_t is not None:
        rowv = {h: src_t[h:h + 1, :] for h in heads}
    else:
        rowv = {h: _dot_nt(jnp.where(lane == h, 1.0, 0.0), src, HI) for h in heads}
    dmat = {h: jnp.where(causal, bc[h] + rowv[h], -jnp.inf) for h in heads}
    inter = {h: bc[h] + m_prev[h] for h in heads}
    m_t = {h: jnp.maximum(inter[h], jnp.max(dmat[h], axis=1, keepdims=True)) for h in heads}
    qk = {h: _dot_nt(q[h], k[h]) for h in heads}
    qc = {h: _dot_nt(q[h], c_prev[h].astype(BF16)) for h in heads}
    nrows = {h: jnp.broadcast_to(n_prev[h], (V7X_SUBLANES, M_HD)).astype(BF16) for h in heads}
    qn = {h: _dot_nt(q[h], nrows[h])[:, 0:1] for h in heads}
    w_inter = {h: jnp.exp(inter[h] - m_t[h]) for h in heads}
    s = {h: qk[h] * jnp.exp(dmat[h] - m_t[h]) for h in heads}
    num = {h: _dot(s[h].astype(BF16), vb[h]) + w_inter[h] * qc[h] for h in heads}
    den = {h: jnp.sum(s[h], axis=1, keepdims=True) + w_inter[h] * qn[h] for h in heads}
    hh = {h: num[h] / jnp.maximum(jnp.abs(den[h]), jnp.exp(-m_t[h])) for h in heads}
    mean = {h: jnp.mean(hh[h], axis=1, keepdims=True) for h in heads}
    d = {h: hh[h] - mean[h] for h in heads}
    var = {h: jnp.mean(d[h] * d[h], axis=1, keepdims=True) for h in heads}
    for h in heads:
        hn = d[h] * lax.rsqrt(var[h] + MH_EPS)
        om_ref[0, :, sl(h)] = (_sigmoid(o_ref[0, :, sl(h)].astype(F32)) * (hn * nw_ref[:, sl(h)])).astype(BF16)
    m_new = {h: m_t[h][lc - 1:lc, :] for h in heads}
    bl = {h: bc[h][lc - 1:lc, :] for h in heads}
    g_state = {h: jnp.exp(bl[h] + m_prev[h] - m_new[h]) for h in heads}
    g_src = {h: jnp.exp(bl[h] - bc[h] + ig[h] - m_new[h]) for h in heads}
    for h in heads:
        c_ref[0, h] = g_state[h] * c_prev[h] + _dot_tn((g_src[h] * vb[h].astype(F32)).astype(BF16), k[h])
        n_ref[0, h:h + 1, :] = g_state[h] * n_prev[h] + jnp.sum(g_src[h] * k[h].astype(F32), axis=0, keepdims=True)
        m_new_all = jnp.where(mlane == h, m_new[h], m_new_all)
    m_ref[0] = m_new_all


def _mlstm(qk, z, zif, c0, n0, m0, if_bias, norm_w):
    b, t, _ = z.shape
    lc = min(512, t)
    w = M_HEADS * M_HD
    return pl.pallas_call(
        functools.partial(_mlstm_kernel, lc=lc),
        grid=(b, t // lc),
        in_specs=[
            pl.BlockSpec((1, lc, 2 * w), lambda i, j: (i, j, 0)),
            pl.BlockSpec((1, lc, w), lambda i, j: (i, j, ZC_MV * V7X_LANES // w)),
            pl.BlockSpec((1, lc, w), lambda i, j: (i, j, ZC_MO * V7X_LANES // w)),
            pl.BlockSpec((1, lc, V7X_LANES), lambda i, j: (i, j, 0)),
            pl.BlockSpec((1, M_HEADS, M_HD, M_HD), lambda i, j: (i, 0, 0, 0)),
            pl.BlockSpec((1, M_HEADS, M_HD), lambda i, j: (i, 0, 0)),
            pl.BlockSpec((1, 1, M_HEADS), lambda i, j: (i, 0, 0)),
            pl.BlockSpec((1, V7X_LANES), lambda i, j: (0, 0)),
            pl.BlockSpec((1, w), lambda i, j: (0, 0)),
        ],
        out_specs=[
            pl.BlockSpec((1, lc, w), lambda i, j: (i, j, 0)),
            pl.BlockSpec((1, M_HEADS, M_HD, M_HD), lambda i, j: (i, 0, 0, 0)),
            pl.BlockSpec((1, M_HEADS, M_HD), lambda i, j: (i, 0, 0)),
            pl.BlockSpec((1, 1, M_HEADS), lambda i, j: (i, 0, 0)),
        ],
        out_shape=[
            jax.ShapeDtypeStruct((b, t, w), BF16),
            jax.ShapeDtypeStruct((b, M_HEADS, M_HD, M_HD), F32),
            jax.ShapeDtypeStruct((b, M_HEADS, M_HD), F32),
            jax.ShapeDtypeStruct((b, 1, M_HEADS), F32),
        ],
        compiler_params=_cparams(("parallel", "arbitrary")),
        name="mlstm",
    )(qk, z, z, zif, c0, n0, m0.reshape(b, 1, M_HEADS), if_bias, norm_w.reshape(1, w))


def _rwkv_mix_kernel(zr_ref, zk_ref, zv_ref, zwa_ref, zgd_ref, sh_ref, mu_ref, w0_ref, w2_ref, a0_ref, a2_ref, g2_ref,
                     kk_ref, ka_ref, s0_ref, lnw_ref, lnb_ref, rk_ref, e2_ref,
                     o_ref, s_ref, y_ref, car_ref, *, npair, tt, lc, rg):
    @pl.when(pl.program_id(2) == 0)
    def _():
        s_ref[...] = s0_ref[...]
        car_ref[...] = sh_ref[0]

    def seg(x):
        return _dot(x.astype(BF16), e2_ref[...])

    first_row = lax.broadcasted_iota(jnp.int32, (tt, 1), 0) == 0
    pairs = range(npair)
    ps = lambda p: slice(p * V7X_LANES, (p + 1) * V7X_LANES)
    hs = lambda h: slice(h * R_HD, (h + 1) * R_HD)

    def shifted(zp, p, i):
        zp = zp.astype(F32)
        prev = jnp.where(first_row, car_ref[p, i:i + 1, :], pltpu.roll(zp, 1, axis=0))
        car_ref[p, i:i + 1, :] = zp[tt - 1:tt, :]
        return zp + (prev - zp) * mu_ref[p, i:i + 1, :]

    xwa = shifted(zwa_ref[0], 0, 3)
    xgd = shifted(zgd_ref[0], 0, 4)
    twa_b = jnp.tanh(xwa).astype(BF16)
    xwa_b = xwa.astype(BF16)
    sgd_b = _sigmoid(xgd).astype(BF16)
    r_all, k_all, v_all, lw_all, a_all, b_all, g_all = {}, {}, {}, {}, {}, {}, {}

    def prologue(p):
        r_all[p] = shifted(zr_ref[0, :, ps(p)], p, 0)
        k_in = shifted(zk_ref[0, :, ps(p)], p, 1)
        v_all[p] = shifted(zv_ref[0, :, ps(p)], p, 2)
        lw_all[p] = -RWKV_DECAY_SCALE * _sigmoid(w0_ref[p] + _dot(twa_b, w2_ref[:, ps(p)]))
        a_gate = _sigmoid(a0_ref[p] + _dot(xwa_b, a2_ref[:, ps(p)]))
        g_all[p] = _dot(sgd_b, g2_ref[:, ps(p)])
        kk = k_in * kk_ref[p]
        kk = kk * lax.rsqrt(jnp.maximum(seg(kk * kk), 1e-24))
        k_all[p] = k_in * (1.0 + (a_gate - 1.0) * ka_ref[p])
        a_all[p] = -kk
        b_all[p] = kk * a_gate

    nb = rg // lc
    urow = lax.broadcasted_iota(jnp.int32, (2 * rg, 2 * rg), 0)
    scol = lax.broadcasted_iota(jnp.int32, (2 * rg, 2 * rg), 1)
    incl = urow >= rg
    trow = jnp.where(incl, urow - rg, urow)
    scol = jnp.where(scol >= rg, scol - rg, scol)
    dist = trow - scol
    gmask = (dist >= jnp.where(incl, 0, 1)) & (dist <= (trow & (lc - 1)))
    lsh = lc.bit_length() - 1
    qrow = lax.broadcasted_iota(jnp.int32, (rg, rg), 0)
    qcol = lax.broadcasted_iota(jnp.int32, (rg, rg), 1)
    same = (qrow >> lsh) == (qcol >> lsh)
    blk = jnp.where(same, 1.0, 0.0).astype(BF16)
    tril = jnp.where(qcol <= qrow, blk, jnp.zeros((), BF16))
    xrow = lax.broadcasted_iota(jnp.int32, (rg, nb * R_HD), 0)
    xcol = lax.broadcasted_iota(jnp.int32, (rg, nb * R_HD), 1)
    xmask = (xrow >> (lc.bit_length() - 1)) == (xcol >> (R_HD.bit_length() - 1))
    n_sq = max(lc.bit_length() - 1, 0)

    groups = range(tt // rg)
    ew = {}

    def prepare(p):
        for g in groups:
            rows = slice(g * rg, (g + 1) * rg)
            lw = lw_all[p][rows]
            k = k_all[p][rows]
            b = b_all[p][rows]
            lw_hi = lw.astype(BF16)
            lw_lo = (lw - lw_hi.astype(F32)).astype(BF16)
            cum = _dot(tril, lw_hi) + _dot(tril, lw_lo)
            ends = [cum[c * lc + lc - 1:c * lc + lc, :] for c in range(nb)]
            cum_end = ends[0] if nb == 1 else jnp.concatenate(
                [jnp.broadcast_to(e, (lc, V7X_LANES)) for e in ends], axis=0)
            e_neg = jnp.exp(-cum)
            at = a_all[p][rows] * jnp.exp(cum - lw)
            rt = r_all[p][rows] * jnp.exp(cum)
            e_end = jnp.exp(cum_end - cum)
            ew[p, g] = dict(at=at, rt=rt, lhs=jnp.concatenate([at, rt], axis=0).astype(BF16),
                            rhs=jnp.concatenate([b * e_neg, k * e_neg], axis=0).astype(BF16),
                            be=(b * e_end).astype(BF16), ke=(k * e_end).astype(BF16),
                            vb=v_all[p][rows].astype(BF16),
                            w_end=[jnp.exp(e) for e in ends])

    def expand(x):
        if nb == 1:
            return x
        return jnp.where(xmask, jnp.concatenate([x] * nb, axis=1), jnp.zeros((), x.dtype))

    piece = lambda c, name: ew[c[0], c[1]][name][:, hs(c[2])]
    pq2, xe, vk = {}, {}, {}

    def solve(chains):
        gm = {c: jnp.where(gmask, _dot_nt(piece(c, "lhs"), piece(c, "rhs")), 0.0) for c in chains}
        gv = {c: _dot(gm[c][:, rg:].astype(BF16), piece(c, "vb")) for c in chains}
        gr = {c: gm[c][rg:] for c in chains}
        gvr = {c: gv[c][rg:] for c in chains}
        x = {c: jnp.concatenate([piece(c, "at"), gv[c][:rg]], axis=1) for c in chains}
        pw = {c: gm[c][:rg, :rg] for c in chains}
        for i in range(n_sq):
            pwb = {c: pw[c].astype(BF16) for c in chains}
            x = {c: x[c] + _dot(pwb[c], x[c].astype(BF16)) for c in chains}
            if i + 1 < n_sq:
                pw = {c: _dot(pwb[c], pwb[c]) for c in chains}
        xb = {c: x[c].astype(BF16) for c in chains}
        for c in chains:
            pq2[c] = jnp.concatenate([piece(c, "rt"), gvr[c]], axis=1) + _dot(gr[c][:, :rg].astype(BF16), xb[c])
            xe[c] = _dot_tn(xb[c], expand(piece(c, "be")))
            vk[c] = _dot_tn(piece(c, "vb"), expand(piece(c, "ke")))

    for p in pairs:
        prologue(p)
    for p in pairs:
        prepare(p)
    solve([(p, g, h) for p in pairs for g in groups for h in range(2)])

    bonus = {p: seg((r_all[p] * k_all[p]) * rk_ref[p]) * v_all[p] for p in pairs}

    s = {(p, h): s_ref[0, 2 * p + h] for p in pairs for h in range(2)}
    for g in groups:
        for c in range(nb):
            rows = slice(c * lc, (c + 1) * lc)
            cols = slice(c * R_HD, (c + 1) * R_HD)
            sb = {ph: s[ph].astype(BF16) for ph in s}
            for p in pairs:
                ys = [_dot_nt(pq2[p, g, h][rows, :R_HD].astype(BF16), sb[p, h]) + pq2[p, g, h][rows, R_HD:]
                      for h in range(2)]
                y_ref[g * rg + c * lc:g * rg + (c + 1) * lc, ps(p)] = jnp.concatenate(ys, axis=1)
            s = {(p, h): (s[p, h] * ew[p, g]["w_end"][c][:, hs(h)]
                          + _dot(sb[p, h], xe[p, g, h][:R_HD, cols].astype(BF16))
                          + (xe[p, g, h][R_HD:, cols] + vk[p, g, h][:, cols])) for (p, h) in s}
    for (p, h) in s:
        s_ref[0, 2 * p + h] = s[p, h]

    y = {p: y_ref[:, ps(p)] for p in pairs}
    d = {p: y[p] - seg(y[p]) * (1.0 / R_HD) for p in pairs}
    var = {p: seg(d[p] * d[p]) * (1.0 / R_HD) for p in pairs}
    for p in pairs:
        yn = (d[p] * lax.rsqrt(var[p] + RWKV_GN_EPS)) * lnw_ref[p] + lnb_ref[p]
        o_ref[0, :, ps(p)] = ((yn + bonus[p]) * g_all[p]).astype(BF16)


def _rwkv_mix(z, shift_prev, s0, mu, w0, w2p, a0, a2p, g2, k_k, k_a, ln_w, ln_b, r_k, e2):
    bsz, t, _ = z.shape
    tt = min(RWKV_TT, t)
    lc = min(RWKV_CHUNK, t)
    rg = min(RWKV_GROUP, t)
    npair = RWKV_PAIRS_PER_STEP
    wide = npair * V7X_LANES
    zpair = lambda blk: pl.BlockSpec((1, tt, wide), lambda i, p, j: (i, j, blk // npair + p))
    zshared = lambda blk: pl.BlockSpec((1, tt, V7X_LANES), lambda i, p, j: (i, j, blk))
    par_spec = pl.BlockSpec((npair, 1, V7X_LANES), lambda i, p, j: (p, 0, 0))
    lora_spec = pl.BlockSpec((V7X_LANES, wide), lambda i, p, j: (0, p))
    st_spec = pl.BlockSpec((1, 2 * npair, R_HD, R_HD), lambda i, p, j: (i, p, 0, 0))
    pair = lambda x: x.reshape(R_PAIRS, 1, V7X_LANES)

    def pieces(x):
        lead = x.shape[:-1]
        rkv = x[..., :3 * R_W].reshape(lead + (3, R_PAIRS, V7X_LANES))
        rkv = jnp.moveaxis(rkv, -3, -2)
        lora = jnp.broadcast_to(x[..., 3 * R_W:].reshape(lead + (1, 2, V7X_LANES)), lead + (R_PAIRS, 2, V7X_LANES))
        return jnp.concatenate([rkv, lora], axis=-2)

    return pl.pallas_call(
        functools.partial(_rwkv_mix_kernel, npair=npair, tt=tt, lc=lc, rg=rg),
        grid=(bsz, R_PAIRS // npair, t // tt),
        in_specs=[
            zpair(ZC_RW), zpair(ZC_RW + 8), zpair(ZC_RW + 16), zshared(ZC_RW + 24), zshared(ZC_RW + 25),
            pl.BlockSpec((1, npair, 5, V7X_LANES), lambda i, p, j: (i, p, 0, 0)),
            pl.BlockSpec((npair, 5, V7X_LANES), lambda i, p, j: (p, 0, 0)),
            par_spec, lora_spec, par_spec, lora_spec, lora_spec, par_spec, par_spec,
            st_spec, par_spec, par_spec, par_spec,
            pl.BlockSpec((V7X_LANES, V7X_LANES), lambda i, p, j: (0, 0)),
        ],
        out_specs=[
            pl.BlockSpec((1, tt, wide), lambda i, p, j: (i, j, p)),
            st_spec,
        ],
        out_shape=[
            jax.ShapeDtypeStruct((bsz, t, R_W), BF16),
            jax.ShapeDtypeStruct((bsz, R_HEADS, R_HD, R_HD), F32),
        ],
        scratch_shapes=[pltpu.VMEM((tt, wide), F32), pltpu.VMEM((npair, 5, V7X_LANES), F32)],
        compiler_params=_cparams(("parallel", "parallel", "arbitrary")),
        name="rwkv_mix",
    )(z, z, z, z, z, pieces(shift_prev.reshape(bsz, R_IN)), pieces(mu), pair(w0), w2p, pair(a0), a2p, g2,
      pair(k_k), pair(k_a), s0, pair(ln_w), pair(ln_b), pair(r_k), e2)


def _merge_kernel(x_ref, zg_ref, hl_ref, om_ref, or_ref, wb_ref, wo_ref, o_ref):
    acc = None
    for gidx, br in enumerate((hl_ref, om_ref, or_ref)):
        sl = slice(gidx * D_MODEL, (gidx + 1) * D_MODEL)
        term = _sigmoid(zg_ref[:, sl].astype(F32)) * _dot(br[...], wb_ref[gidx])
        acc = term if acc is None else acc + term
    o_ref[...] = x_ref[...] + _dot(acc.astype(BF16), wo_ref[...])


def _merge(x, z2d, hl, om, orw, wb, wo):
    m = x.shape[0]
    tm = min(512, m)
    tok = pl.BlockSpec((tm, D_MODEL), lambda i: (i, 0))
    return pl.pallas_call(
        _merge_kernel,
        grid=(m // tm,),
        in_specs=[
            tok,
            pl.BlockSpec((tm, 3 * D_MODEL), lambda i: (i, ZC_GATE * V7X_LANES // (3 * D_MODEL))),
            tok, tok, tok,
            pl.BlockSpec((3, D_MODEL, D_MODEL), lambda i: (0, 0, 0)),
            pl.BlockSpec((D_MODEL, D_MODEL), lambda i: (0, 0)),
        ],
        out_specs=tok,
        out_shape=jax.ShapeDtypeStruct((m, D_MODEL), F32),
        compiler_params=_cparams(("parallel",)),
        name="merge",
    )(x, z2d, hl, om, orw, wb, wo)


def _prep_layer_weights(p, l):
    w_in = p["w_in"][l]
    c0, c1, c2, c3, c4 = 3072, 4096, 5120, 5128, 8456
    w_if = jnp.pad(w_in[:, c2:c3], ((0, 0), (0, V7X_LANES - 2 * M_HEADS))).astype(BF16)
    w_all = jnp.concatenate([w_in[:, :c0], w_in[:, c4:], w_in[:, c0:c1], w_in[:, c1:c2], w_in[:, c3:c4]],
                            axis=1).astype(BF16)
    def pairs(w):
        z = jnp.zeros((LRU_BD, LRU_BD), F32)
        return jnp.stack([jnp.block([[w[2 * i], z], [z, w[2 * i + 1]]]) for i in range(LRU_BLOCKS // 2)])
    w2 = jnp.concatenate([pairs(p["lru_wa"][l]), pairs(p["lru_wx"][l])], axis=2).astype(BF16)
    bax = jnp.stack([p["lru_ba"][l], p["lru_bx"][l]])
    if_bias = jnp.pad(p["mlstm_if_bias"][l], (0, V7X_LANES - 2 * M_HEADS)).reshape(1, V7X_LANES)
    zpad = jnp.zeros((64, R_W), F32)
    w2p = jnp.concatenate([p["rwkv_w2"][l], zpad], axis=0).astype(BF16)
    a2p = jnp.concatenate([zpad, p["rwkv_a2"][l]], axis=0).astype(BF16)
    return dict(
        w_all=w_all, w_if=w_if, w2=w2, bax=bax, if_bias=if_bias, w2p=w2p, a2p=a2p,
        g2=p["rwkv_g2"][l].astype(BF16),
        wb=p["w_branch"][l].astype(BF16), wo=p["w_out"][l].astype(BF16),
        f1=(p["ffn1_w_gate"][l].astype(BF16), p["ffn1_w_up"][l].astype(BF16), p["ffn1_w_down"][l].astype(BF16)),
        f2=(p["ffn2_w_gate"][l].astype(BF16), p["ffn2_w_up"][l].astype(BF16), p["ffn2_w_down"][l].astype(BF16)),
    )


def _segment_matrix():
    half = jnp.arange(V7X_LANES) // R_HD
    return (half[:, None] == half[None, :]).astype(BF16)


def _run_group(x, states, p, lw_list, final_norm):
    bsz, t, _ = x.shape
    m = bsz * t
    e2 = _segment_matrix()
    xf = x.reshape(m, D_MODEL)
    new_states = []
    depth = p["w_in"].shape[0]
    for l in range(depth):
        lw = lw_list[l]
        conv_prev, lru_h, m_c, m_n, m_m, shift_prev, rwkv_s = states[l]
        xf = _ffn(xf, p["ffn1_norm"][l], *lw["f1"])
        z2d, zif = _inproj(xf, p["mix_norm"][l], lw["w_all"], lw["w_if"])
        z = z2d.reshape(bsz, t, Z_W)
        hl, qk, h_last = _conv_lru(z, conv_prev, lru_h, p["conv_w"][l], p["conv_b"][l], lw["w2"], lw["bax"],
                                   p["lru_lambda"][l])
        om, c_new, n_new, m_new = _mlstm(qk, z, zif.reshape(bsz, t, V7X_LANES), m_c, m_n, m_m, lw["if_bias"],
                                         p["mlstm_norm"][l])
        orw, s_new = _rwkv_mix(z, shift_prev, rwkv_s, p["rwkv_mu"][l], p["rwkv_w0"][l], lw["w2p"], p["rwkv_a0"][l],
                               lw["a2p"], lw["g2"], p["rwkv_k_k"][l], p["rwkv_k_a"][l], p["rwkv_ln_w"][l],
                               p["rwkv_ln_b"][l], p["rwkv_r_k"][l].reshape(-1), e2)
        xf = _merge(xf, z2d, hl.reshape(m, D_MODEL), om.reshape(m, D_MODEL), orw.reshape(m, D_MODEL),
                    lw["wb"], lw["wo"])
        xf = _ffn(xf, p["ffn2_norm"][l], *lw["f2"], final_g=final_norm if l == depth - 1 else None)
        conv_new = z[:, t - (CONV_W - 1):, ZC_CONV * V7X_LANES:ZC_CONV * V7X_LANES + CONV_CH].astype(F32)
        shift_new = z[:, t - 1:, ZC_RW * V7X_LANES:ZC_RW * V7X_LANES + R_IN].astype(F32)
        new_states.append((conv_new, h_last.reshape(bsz, LRU_W), c_new, n_new, m_new.reshape(bsz, M_HEADS),
                           shift_new, s_new))
    stacked = tuple(jnp.stack([st[i] for st in new_states]) for i in range(7))
    return xf.reshape(bsz, t, D_MODEL), stacked


def _zero_states(bsz):
    return (jnp.zeros((bsz, CONV_W - 1, CONV_CH), F32), jnp.zeros((bsz, LRU_W), F32),
            jnp.zeros((bsz, M_HEADS, M_HD, M_HD), F32), jnp.zeros((bsz, M_HEADS, M_HD), F32),
            jnp.zeros((bsz, M_HEADS), F32), jnp.zeros((bsz, 1, R_IN), F32),
            jnp.zeros((bsz, R_HEADS, R_HD, R_HD), F32))


def kernel(x_prompt, x_sample, state_conv, state_lru_h, state_mlstm_C, state_mlstm_n, state_mlstm_m, state_rwkv_shift, state_rwkv_S, ffn1_norm, ffn1_w_gate, ffn1_w_up, ffn1_w_down, mix_norm, w_in, conv_w, conv_b, lru_wa, lru_ba, lru_wx, lru_bx, lru_lambda, mlstm_if_bias, mlstm_norm, rwkv_mu, rwkv_w0, rwkv_w2, rwkv_a0, rwkv_a2, rwkv_g2, rwkv_k_k, rwkv_k_a, rwkv_r_k, rwkv_ln_w, rwkv_ln_b, w_branch, w_out, ffn2_norm, ffn2_w_gate, ffn2_w_up, ffn2_w_down, final_norm):
    p = dict(ffn1_norm=ffn1_norm, ffn1_w_gate=ffn1_w_gate, ffn1_w_up=ffn1_w_up, ffn1_w_down=ffn1_w_down,
             mix_norm=mix_norm, w_in=w_in, conv_w=conv_w, conv_b=conv_b,
             lru_wa=lru_wa, lru_ba=lru_ba, lru_wx=lru_wx, lru_bx=lru_bx, lru_lambda=lru_lambda,
             mlstm_if_bias=mlstm_if_bias, mlstm_norm=mlstm_norm,
             rwkv_mu=rwkv_mu, rwkv_w0=rwkv_w0, rwkv_w2=rwkv_w2, rwkv_a0=rwkv_a0, rwkv_a2=rwkv_a2,
             rwkv_g2=rwkv_g2, rwkv_k_k=rwkv_k_k, rwkv_k_a=rwkv_k_a, rwkv_r_k=rwkv_r_k,
             rwkv_ln_w=rwkv_ln_w, rwkv_ln_b=rwkv_ln_b, w_branch=w_branch, w_out=w_out,
             ffn2_norm=ffn2_norm, ffn2_w_gate=ffn2_w_gate, ffn2_w_up=ffn2_w_up, ffn2_w_down=ffn2_w_down)
    depth = w_in.shape[0]
    lw_list = [_prep_layer_weights(p, l) for l in range(depth)]
    states_p = [_zero_states(x_prompt.shape[0]) for _ in range(depth)]
    y_p, st_p = _run_group(x_prompt, states_p, p, lw_list, final_norm)
    states_s = [(state_conv[l], state_lru_h[l], state_mlstm_C[l], state_mlstm_n[l], state_mlstm_m[l],
                 state_rwkv_shift[l], state_rwkv_S[l]) for l in range(depth)]
    y_s, st_s = _run_group(x_sample, states_s, p, lw_list, final_norm)
    return (y_p, y_s) + st_p + st_s
```

```python
import functools

import jax
import jax.numpy as jnp
from jax import lax
from jax.experimental import pallas as pl
from jax.experimental.pallas import tpu as pltpu

F32 = jnp.float32
BF16 = jnp.bfloat16

V7X_LANES = 128
V7X_SUBLANES = 8
V7X_VMEM_BYTES = 64 * 1024 * 1024
VMEM_LIMIT = 56 * 1024 * 1024

D_MODEL = 1024
D_FF = 2816
CONV_W = 4
LRU_W = 1024
LRU_BLOCKS = 8
LRU_BD = LRU_W // LRU_BLOCKS
LRU_PAIR = 2 * LRU_BD
LRU_C = 8.0
M_HEADS = 4
M_HD = 256
R_HD = 64
R_HEADS = 16
R_PAIRS = R_HEADS // 2
R_W = 1024
R_IN = 3 * R_W + 256
CONV_CH = 3 * 1024
RMS_EPS = 1e-6
MH_EPS = 1e-6
RWKV_GN_EPS = 64e-5
RWKV_DECAY_SCALE = 0.6065306597126334

ZC_CONV = 0
ZC_GATE = 24
ZC_MV = 48
ZC_MO = 56
ZC_RW = 64
Z_BLOCKS = 90
Z_W = Z_BLOCKS * V7X_LANES
Z_TN = 18 * V7X_LANES

FFN_SUB = 768
CONV_ROWS = 128
RWKV_CHUNK = 64
RWKV_GROUP = 128
RWKV_TT = 256
RWKV_PAIRS_PER_STEP = 8

HI = lax.Precision.HIGHEST


def _cparams(sem):
    return pltpu.CompilerParams(dimension_semantics=sem, vmem_limit_bytes=VMEM_LIMIT)


def _rms(x, g):
    return (x * lax.rsqrt(jnp.mean(x * x, axis=-1, keepdims=True) + RMS_EPS)) * g


def _softplus(x):
    return jnp.maximum(x, 0.0) + jnp.log1p(jnp.exp(-jnp.abs(x)))


def _sigmoid(x):
    return jax.nn.sigmoid(x)


def _dot(a, b, prec=None):
    return jnp.dot(a, b, preferred_element_type=F32, precision=prec)


def _dot_nt(a, b, prec=None):
    return lax.dot_general(a, b, (((1,), (1,)), ((), ())), preferred_element_type=F32, precision=prec)


def _dot_tn(a, b, prec=None):
    return lax.dot_general(a, b, (((0,), (0,)), ((), ())), preferred_element_type=F32, precision=prec)


def _cumsum_rows(x):
    n = x.shape[0]
    row = lax.broadcasted_iota(jnp.int32, x.shape, 0)
    d = 1
    while d < n:
        x = x + jnp.where(row >= d, pltpu.roll(x, d, axis=0), 0.0)
        d *= 2
    return x


def _ffn_kernel(x_ref, g_ref, wg_ref, wu_ref, wd_ref, *rest, final_norm):
    if final_norm:
        fg_ref, o_ref = rest
    else:
        (o_ref,) = rest
    x = x_ref[...]
    xn = _rms(x, g_ref[...]).astype(BF16)
    acc = None
    for c0 in range(0, D_FF, FFN_SUB):
        c1 = min(c0 + FFN_SUB, D_FF)
        hg = _dot(xn, wg_ref[:, c0:c1])
        hu = _dot(xn, wu_ref[:, c0:c1])
        h = (hg * _sigmoid(hg)) * hu
        d = _dot(h.astype(BF16), wd_ref[c0:c1, :])
        acc = d if acc is None else acc + d
    y = x + 0.5 * acc
    if final_norm:
        y = _rms(y, fg_ref[...])
    o_ref[...] = y


def _ffn(x, g, wg, wu, wd, layer, final_g=None):
    m = x.shape[0]
    tm = min(1024, m)
    resident = pl.Buffered(1)
    in_specs = [
        pl.BlockSpec((tm, D_MODEL), lambda i: (i, 0)),
        pl.BlockSpec((1, D_MODEL), lambda i: (0, 0)),
        pl.BlockSpec((None, D_MODEL, D_FF), lambda i: (layer, 0, 0), pipeline_mode=resident),
        pl.BlockSpec((None, D_MODEL, D_FF), lambda i: (layer, 0, 0), pipeline_mode=resident),
        pl.BlockSpec((None, D_FF, D_MODEL), lambda i: (layer, 0, 0), pipeline_mode=resident),
    ]
    args = [x, g.reshape(1, D_MODEL), wg, wu, wd]
    if final_g is not None:
        in_specs.append(pl.BlockSpec((1, D_MODEL), lambda i: (0, 0)))
        args.append(final_g.reshape(1, D_MODEL))
    return pl.pallas_call(
        functools.partial(_ffn_kernel, final_norm=final_g is not None),
        grid=(m // tm,),
        in_specs=in_specs,
        out_specs=pl.BlockSpec((tm, D_MODEL), lambda i: (i, 0)),
        out_shape=jax.ShapeDtypeStruct((m, D_MODEL), F32),
        compiler_params=_cparams(("parallel",)),
        name="ffn",
    )(*args)


def _inproj_kernel(x_ref, g_ref, w_ref, wif_ref, o_ref, oif_ref, xn_ref):
    @pl.when(pl.program_id(1) == 0)
    def _():
        xn = _rms(x_ref[...], g_ref[...]).astype(BF16)
        xn_ref[...] = xn
        oif_ref[...] = _dot(xn, wif_ref[...])

    o_ref[...] = _dot(xn_ref[...], w_ref[...]).astype(BF16)


def _inproj(x, g, w_all, w_if):
    m = x.shape[0]
    tm = min(1024, m)
    return pl.pallas_call(
        _inproj_kernel,
        grid=(m // tm, Z_W // Z_TN),
        in_specs=[
            pl.BlockSpec((tm, D_MODEL), lambda i, j: (i, 0)),
            pl.BlockSpec((1, D_MODEL), lambda i, j: (0, 0)),
            pl.BlockSpec((D_MODEL, Z_TN), lambda i, j: (0, j)),
            pl.BlockSpec((D_MODEL, V7X_LANES), lambda i, j: (0, 0)),
        ],
        out_specs=[
            pl.BlockSpec((tm, Z_TN), lambda i, j: (i, j)),
            pl.BlockSpec((tm, V7X_LANES), lambda i, j: (i, 0)),
        ],
        out_shape=[
            jax.ShapeDtypeStruct((m, Z_W), BF16),
            jax.ShapeDtypeStruct((m, V7X_LANES), F32),
        ],
        scratch_shapes=[pltpu.VMEM((tm, D_MODEL), BF16)],
        compiler_params=_cparams(("parallel", "arbitrary")),
        name="in_proj",
    )(x, g.reshape(1, D_MODEL), w_all, w_if)


def _conv_lru_kernel(z_ref, cprev_ref, h0_ref, cw_ref, cb_ref, w2_ref, bax_ref, lam_ref,
                     hl_ref, qk_ref, hlast_ref, xp_ref, a_ref, b_ref, hs_ref, hc_ref, *, tt):
    pad = V7X_SUBLANES

    @pl.when(pl.program_id(1) == 0)
    def _():
        xp_ref[0:pad, :] = jnp.zeros((pad, CONV_CH), F32)
        xp_ref[pad - 3:pad, :] = cprev_ref[0]
        hc_ref[...] = h0_ref[0]

    xp_ref[pad:pad + tt, :] = z_ref[0].astype(F32)
    sp = _softplus(-lam_ref[...])
    rb = min(CONV_ROWS, tt)
    for r0 in range(0, tt, rb):
        rows = slice(r0, r0 + rb)
        for cblk in range(CONV_CH // LRU_PAIR):
            cols = slice(cblk * LRU_PAIR, (cblk + 1) * LRU_PAIR)
            xe = xp_ref[r0:pad + r0 + rb, cols]
            c = xe * cw_ref[0:1, cols] + cb_ref[:, cols]
            c = pltpu.roll(c, 1, axis=0) + xe * cw_ref[1:2, cols]
            c = pltpu.roll(c, 1, axis=0) + xe * cw_ref[2:3, cols]
            c = (pltpu.roll(c, 1, axis=0) + xe * cw_ref[3:4, cols])[pad:]
            if cblk < LRU_W // LRU_PAIR:
                gpre = _dot(c.astype(BF16), w2_ref[cblk])
                r = _sigmoid(gpre[:, :LRU_PAIR] + bax_ref[0:1, cols])
                i = _sigmoid(gpre[:, LRU_PAIR:] + bax_ref[1:2, cols])
                log_a = (-LRU_C * r) * sp[:, cols]
                a = jnp.exp(log_a)
                mult = jnp.sqrt(-jnp.tanh(log_a) * (a * a + 1.0))
                a_ref[rows, cols] = a
                b_ref[rows, cols] = mult * (i * c)
            elif cblk < 2 * LRU_W // LRU_PAIR:
                qk_ref[0, rows, cblk * LRU_PAIR - LRU_W:(cblk + 1) * LRU_PAIR - LRU_W] = (
                    c * _sigmoid(c)).astype(BF16)
            else:
                qk_ref[0, rows, cblk * LRU_PAIR - LRU_W:(cblk + 1) * LRU_PAIR - LRU_W] = (
                    (c * _sigmoid(c)) * (M_HD ** -0.5)).astype(BF16)
    xp_ref[pad - 3:pad, :] = xp_ref[pad + tt - 3:pad + tt, :]

    def body(t, h):
        h = a_ref[pl.ds(t, 1), :] * h + b_ref[pl.ds(t, 1), :]
        hs_ref[pl.ds(t, 1), :] = h
        return h

    h = lax.fori_loop(0, tt, body, hc_ref[...], unroll=8)
    hc_ref[...] = h
    hlast_ref[0] = h
    hl_ref[0] = hs_ref[...].astype(BF16)


def _conv_lru(z, conv_prev, h0, cw, cb, w2, bax, lam):
    b, t, _ = z.shape
    tt = min(512, t)
    return pl.pallas_call(
        functools.partial(_conv_lru_kernel, tt=tt),
        grid=(b, t // tt),
        in_specs=[
            pl.BlockSpec((1, tt, CONV_CH), lambda i, j: (i, j, ZC_CONV * V7X_LANES // CONV_CH)),
            pl.BlockSpec((1, CONV_W - 1, CONV_CH), lambda i, j: (i, 0, 0)),
            pl.BlockSpec((1, 1, LRU_W), lambda i, j: (i, 0, 0)),
            pl.BlockSpec((CONV_W, CONV_CH), lambda i, j: (0, 0)),
            pl.BlockSpec((1, CONV_CH), lambda i, j: (0, 0)),
            pl.BlockSpec((LRU_W // LRU_PAIR, LRU_PAIR, 2 * LRU_PAIR), lambda i, j: (0, 0, 0)),
            pl.BlockSpec((2, LRU_W), lambda i, j: (0, 0)),
            pl.BlockSpec((1, LRU_W), lambda i, j: (0, 0)),
        ],
        out_specs=[
            pl.BlockSpec((1, tt, LRU_W), lambda i, j: (i, j, 0)),
            pl.BlockSpec((1, tt, 2048), lambda i, j: (i, j, 0)),
            pl.BlockSpec((1, 1, LRU_W), lambda i, j: (i, 0, 0)),
        ],
        out_shape=[
            jax.ShapeDtypeStruct((b, t, LRU_W), BF16),
            jax.ShapeDtypeStruct((b, t, 2048), BF16),
            jax.ShapeDtypeStruct((b, 1, LRU_W), F32),
        ],
        scratch_shapes=[
            pltpu.VMEM((tt + V7X_SUBLANES, CONV_CH), F32),
            pltpu.VMEM((tt, LRU_W), F32),
            pltpu.VMEM((tt, LRU_W), F32),
            pltpu.VMEM((tt, LRU_W), F32),
            pltpu.VMEM((1, LRU_W), F32),
        ],
        compiler_params=_cparams(("parallel", "arbitrary")),
        name="conv_lru",
    )(z, conv_prev, h0.reshape(b, 1, LRU_W), cw, cb.reshape(1, CONV_CH), w2, bax, lam.reshape(1, LRU_W))


def _mlstm_kernel(qk_ref, v_ref, o_ref, if_ref, c0_ref, n0_ref, m0_ref, ifb_ref, nw_ref,
                  om_ref, c_ref, n_ref, m_ref, *, lc):
    @pl.when(pl.program_id(1) == 0)
    def _():
        c_ref[...] = c0_ref[...]
        n_ref[...] = n0_ref[...]
        m_ref[...] = m0_ref[...]

    gif = if_ref[0] + ifb_ref[...]
    lf = jnp.minimum(gif, 0.0) - jnp.log1p(jnp.exp(-jnp.abs(gif)))
    cum = _cumsum_rows(lf)
    src = gif - pltpu.roll(cum, V7X_LANES - M_HEADS, axis=1)
    src_t = src.T if lc % V7X_LANES == 0 else None
    lane = lax.broadcasted_iota(jnp.int32, (lc, V7X_LANES), 1)
    tpos = lax.broadcasted_iota(jnp.int32, (lc, lc), 0)
    spos = lax.broadcasted_iota(jnp.int32, (lc, lc), 1)
    causal = spos <= tpos
    m_prev_all = m_ref[0]
    mlane = lax.broadcasted_iota(jnp.int32, (1, M_HEADS), 1)
    m_new_all = m_prev_all

    heads = range(M_HEADS)
    sl = lambda h: slice(h * M_HD, (h + 1) * M_HD)
    q = {h: qk_ref[0, :, sl(h)] for h in heads}
    k = {h: qk_ref[0, :, M_HEADS * M_HD + h * M_HD:M_HEADS * M_HD + (h + 1) * M_HD] for h in heads}
    vb = {h: v_ref[0, :, sl(h)] for h in heads}
    bc = {h: cum[:, M_HEADS + h:M_HEADS + h + 1] for h in heads}
    ig = {h: gif[:, h:h + 1] for h in heads}
    m_prev = {h: m_prev_all[:, h:h + 1] for h in heads}
    c_prev = {h: c_ref[0, h] for h in heads}
    n_prev = {h: n_ref[0, h:h + 1, :] for h in heads}
    if src_t is not None:
        rowv = {h: src_t[h:h + 1, :] for h in heads}
    else:
        rowv = {h: _dot_nt(jnp.where(lane == h, 1.0, 0.0), src, HI) for h in heads}
    dmat = {h: jnp.where(causal, bc[h] + rowv[h], -jnp.inf) for h in heads}
    inter = {h: bc[h] + m_prev[h] for h in heads}
    m_t = {h: jnp.maximum(inter[h], jnp.max(dmat[h], axis=1, keepdims=True)) for h in heads}
    qk = {h: _dot_nt(q[h], k[h]) for h in heads}
    qc = {h: _dot_nt(q[h], c_prev[h].astype(BF16)) for h in heads}
    nrows = {h: jnp.broadcast_to(n_prev[h], (V7X_SUBLANES, M_HD)).astype(BF16) for h in heads}
    qn = {h: _dot_nt(q[h], nrows[h])[:, 0:1] for h in heads}
    w_inter = {h: jnp.exp(inter[h] - m_t[h]) for h in heads}
    s = {h: qk[h] * jnp.exp(dmat[h] - m_t[h]) for h in heads}
    num = {h: _dot(s[h].astype(BF16), vb[h]) + w_inter[h] * qc[h] for h in heads}
    den = {h: jnp.sum(s[h], axis=1, keepdims=True) + w_inter[h] * qn[h] for h in heads}
    hh = {h: num[h] / jnp.maximum(jnp.abs(den[h]), jnp.exp(-m_t[h])) for h in heads}
    mean = {h: jnp.mean(hh[h], axis=1, keepdims=True) for h in heads}
    d = {h: hh[h] - mean[h] for h in heads}
    var = {h: jnp.mean(d[h] * d[h], axis=1, keepdims=True) for h in heads}
    for h in heads:
        hn = d[h] * lax.rsqrt(var[h] + MH_EPS)
        om_ref[0, :, sl(h)] = (_sigmoid(o_ref[0, :, sl(h)].astype(F32)) * (hn * nw_ref[:, sl(h)])).astype(BF16)
    m_new = {h: m_t[h][lc - 1:lc, :] for h in heads}
    bl = {h: bc[h][lc - 1:lc, :] for h in heads}
    g_state = {h: jnp.exp(bl[h] + m_prev[h] - m_new[h]) for h in heads}
    g_src = {h: jnp.exp(bl[h] - bc[h] + ig[h] - m_new[h]) for h in heads}
    for h in heads:
        c_ref[0, h] = g_state[h] * c_prev[h] + _dot_tn((g_src[h] * vb[h].astype(F32)).astype(BF16), k[h])
        n_ref[0, h:h + 1, :] = g_state[h] * n_prev[h] + jnp.sum(g_src[h] * k[h].astype(F32), axis=0, keepdims=True)
        m_new_all = jnp.where(mlane == h, m_new[h], m_new_all)
    m_ref[0] = m_new_all


def _mlstm(qk, z, zif, c0, n0, m0, if_bias, norm_w):
    b, t, _ = z.shape
    lc = min(512, t)
    w = M_HEADS * M_HD
    return pl.pallas_call(
        functools.partial(_mlstm_kernel, lc=lc),
        grid=(b, t // lc),
        in_specs=[
            pl.BlockSpec((1, lc, 2 * w), lambda i, j: (i, j, 0)),
            pl.BlockSpec((1, lc, w), lambda i, j: (i, j, ZC_MV * V7X_LANES // w)),
            pl.BlockSpec((1, lc, w), lambda i, j: (i, j, ZC_MO * V7X_LANES // w)),
            pl.BlockSpec((1, lc, V7X_LANES), lambda i, j: (i, j, 0)),
            pl.BlockSpec((1, M_HEADS, M_HD, M_HD), lambda i, j: (i, 0, 0, 0)),
            pl.BlockSpec((1, M_HEADS, M_HD), lambda i, j: (i, 0, 0)),
            pl.BlockSpec((1, 1, M_HEADS), lambda i, j: (i, 0, 0)),
            pl.BlockSpec((1, V7X_LANES), lambda i, j: (0, 0)),
            pl.BlockSpec((1, w), lambda i, j: (0, 0)),
        ],
        out_specs=[
            pl.BlockSpec((1, lc, w), lambda i, j: (i, j, 0)),
            pl.BlockSpec((1, M_HEADS, M_HD, M_HD), lambda i, j: (i, 0, 0, 0)),
            pl.BlockSpec((1, M_HEADS, M_HD), lambda i, j: (i, 0, 0)),
            pl.BlockSpec((1, 1, M_HEADS), lambda i, j: (i, 0, 0)),
        ],
        out_shape=[
            jax.ShapeDtypeStruct((b, t, w), BF16),
            jax.ShapeDtypeStruct((b, M_HEADS, M_HD, M_HD), F32),
            jax.ShapeDtypeStruct((b, M_HEADS, M_HD), F32),
            jax.ShapeDtypeStruct((b, 1, M_HEADS), F32),
        ],
        compiler_params=_cparams(("parallel", "arbitrary")),
        name="mlstm",
    )(qk, z, z, zif, c0, n0, m0.reshape(b, 1, M_HEADS), if_bias, norm_w.reshape(1, w))


def _rwkv_mix_kernel(zr_ref, zk_ref, zv_ref, zwa_ref, zgd_ref, sh_ref, mu_ref, w0_ref, w2_ref, a0_ref, a2_ref, g2_ref,
                     kk_ref, ka_ref, s0_ref, lnw_ref, lnb_ref, rk_ref, e2_ref,
                     o_ref, s_ref, y_ref, car_ref, *, npair, tt, lc, rg):
    @pl.when(pl.program_id(2) == 0)
    def _():
        s_ref[...] = s0_ref[...]
        car_ref[...] = sh_ref[0]

    def seg(x):
        return _dot(x.astype(BF16), e2_ref[...])

    first_row = lax.broadcasted_iota(jnp.int32, (tt, 1), 0) == 0
    pairs = range(npair)
    ps = lambda p: slice(p * V7X_LANES, (p + 1) * V7X_LANES)
    hs = lambda h: slice(h * R_HD, (h + 1) * R_HD)

    def shifted(zp, p, i):
        zp = zp.astype(F32)
        prev = jnp.where(first_row, car_ref[p, i:i + 1, :], pltpu.roll(zp, 1, axis=0))
        car_ref[p, i:i + 1, :] = zp[tt - 1:tt, :]
        return zp + (prev - zp) * mu_ref[p, i:i + 1, :]

    xwa = shifted(zwa_ref[0], 0, 3)
    xgd = shifted(zgd_ref[0], 0, 4)
    twa_b = jnp.tanh(xwa).astype(BF16)
    xwa_b = xwa.astype(BF16)
    sgd_b = _sigmoid(xgd).astype(BF16)
    r_all, k_all, v_all, lw_all, a_all, b_all, g_all = {}, {}, {}, {}, {}, {}, {}

    def prologue(p):
        r_all[p] = shifted(zr_ref[0, :, ps(p)], p, 0)
        k_in = shifted(zk_ref[0, :, ps(p)], p, 1)
        v_all[p] = shifted(zv_ref[0, :, ps(p)], p, 2)
        lw_all[p] = -RWKV_DECAY_SCALE * _sigmoid(w0_ref[p] + _dot(twa_b, w2_ref[:, ps(p)]))
        a_gate = _sigmoid(a0_ref[p] + _dot(xwa_b, a2_ref[:, ps(p)]))
        g_all[p] = _dot(sgd_b, g2_ref[:, ps(p)])
        kk = k_in * kk_ref[p]
        kk = kk * lax.rsqrt(jnp.maximum(seg(kk * kk), 1e-24))
        k_all[p] = k_in * (1.0 + (a_gate - 1.0) * ka_ref[p])
        a_all[p] = -kk
        b_all[p] = kk * a_gate

    nb = rg // lc
    urow = lax.broadcasted_iota(jnp.int32, (2 * rg, 2 * rg), 0)
    scol = lax.broadcasted_iota(jnp.int32, (2 * rg, 2 * rg), 1)
    incl = urow >= rg
    trow = jnp.where(incl, urow - rg, urow)
    scol = jnp.where(scol >= rg, scol - rg, scol)
    dist = trow - scol
    gmask = (dist >= jnp.where(incl, 0, 1)) & (dist <= (trow & (lc - 1)))
    lsh = lc.bit_length() - 1
    qrow = lax.broadcasted_iota(jnp.int32, (rg, rg), 0)
    qcol = lax.broadcasted_iota(jnp.int32, (rg, rg), 1)
    same = (qrow >> lsh) == (qcol >> lsh)
    blk = jnp.where(same, 1.0, 0.0).astype(BF16)
    tril = jnp.where(qcol <= qrow, blk, jnp.zeros((), BF16))
    xrow = lax.broadcasted_iota(jnp.int32, (rg, nb * R_HD), 0)
    xcol = lax.broadcasted_iota(jnp.int32, (rg, nb * R_HD), 1)
    xmask = (xrow >> (lc.bit_length() - 1)) == (xcol >> (R_HD.bit_length() - 1))
    n_sq = max(lc.bit_length() - 1, 0)

    groups = range(tt // rg)
    ew = {}

    def prepare(p):
        for g in groups:
            rows = slice(g * rg, (g + 1) * rg)
            lw = lw_all[p][rows]
            k = k_all[p][rows]
            b = b_all[p][rows]
            lw_hi = lw.astype(BF16)
            lw_lo = (lw - lw_hi.astype(F32)).astype(BF16)
            cum = _dot(tril, lw_hi) + _dot(tril, lw_lo)
            ends = [cum[c * lc + lc - 1:c * lc + lc, :] for c in range(nb)]
            cum_end = ends[0] if nb == 1 else jnp.concatenate(
                [jnp.broadcast_to(e, (lc, V7X_LANES)) for e in ends], axis=0)
            e_neg = jnp.exp(-cum)
            at = a_all[p][rows] * jnp.exp(cum - lw)
            rt = r_all[p][rows] * jnp.exp(cum)
            e_end = jnp.exp(cum_end - cum)
            ew[p, g] = dict(at=at, rt=rt, lhs=jnp.concatenate([at, rt], axis=0).astype(BF16),
                            rhs=jnp.concatenate([b * e_neg, k * e_neg], axis=0).astype(BF16),
                            be=(b * e_end).astype(BF16), ke=(k * e_end).astype(BF16),
                            vb=v_all[p][rows].astype(BF16),
                            w_end=[jnp.exp(e) for e in ends])

    def expand(x):
        if nb == 1:
            return x
        return jnp.where(xmask, jnp.concatenate([x] * nb, axis=1), jnp.zeros((), x.dtype))

    piece = lambda c, name: ew[c[0], c[1]][name][:, hs(c[2])]
    pq2, xe, vk = {}, {}, {}

    def solve(chains):
        gm = {c: jnp.where(gmask, _dot_nt(piece(c, "lhs"), piece(c, "rhs")), 0.0) for c in chains}
        gv = {c: _dot(gm[c][:, rg:].astype(BF16), piece(c, "vb")) for c in chains}
        gr = {c: gm[c][rg:] for c in chains}
        gvr = {c: gv[c][rg:] for c in chains}
        x = {c: jnp.concatenate([piece(c, "at"), gv[c][:rg]], axis=1) for c in chains}
        pw = {c: gm[c][:rg, :rg] for c in chains}
        for i in range(n_sq):
            pwb = {c: pw[c].astype(BF16) for c in chains}
            x = {c: x[c] + _dot(pwb[c], x[c].astype(BF16)) for c in chains}
            if i + 1 < n_sq:
                pw = {c: _dot(pwb[c], pwb[c]) for c in chains}
        xb = {c: x[c].astype(BF16) for c in chains}
        for c in chains:
            pq2[c] = jnp.concatenate([piece(c, "rt"), gvr[c]], axis=1) + _dot(gr[c][:, :rg].astype(BF16), xb[c])
            xe[c] = _dot_tn(xb[c], expand(piece(c, "be")))
            vk[c] = _dot_tn(piece(c, "vb"), expand(piece(c, "ke")))

    for p in pairs:
        prologue(p)
    for p in pairs:
        prepare(p)
    solve([(p, g, h) for p in pairs for g in groups for h in range(2)])

    bonus = {p: seg((r_all[p] * k_all[p]) * rk_ref[p]) * v_all[p] for p in pairs}

    s = {(p, h): s_ref[0, 2 * p + h] for p in pairs for h in range(2)}
    for g in groups:
        for c in range(nb):
            rows = slice(c * lc, (c + 1) * lc)
            cols = slice(c * R_HD, (c + 1) * R_HD)
            sb = {ph: s[ph].astype(BF16) for ph in s}
            for p in pairs:
                ys = [_dot_nt(pq2[p, g, h][rows, :R_HD].astype(BF16), sb[p, h]) + pq2[p, g, h][rows, R_HD:]
                      for h in range(2)]
                y_ref[g * rg + c * lc:g * rg + (c + 1) * lc, ps(p)] = jnp.concatenate(ys, axis=1)
            s = {(p, h): (s[p, h] * ew[p, g]["w_end"][c][:, hs(h)]
                          + _dot(sb[p, h], xe[p, g, h][:R_HD, cols].astype(BF16))
                          + (xe[p, g, h][R_HD:, cols] + vk[p, g, h][:, cols])) for (p, h) in s}
    for (p, h) in s:
        s_ref[0, 2 * p + h] = s[p, h]

    y = {p: y_ref[:, ps(p)] for p in pairs}
    d = {p: y[p] - seg(y[p]) * (1.0 / R_HD) for p in pairs}
    var = {p: seg(d[p] * d[p]) * (1.0 / R_HD) for p in pairs}
    for p in pairs:
        yn = (d[p] * lax.rsqrt(var[p] + RWKV_GN_EPS)) * lnw_ref[p] + lnb_ref[p]
        o_ref[0, :, ps(p)] = ((yn + bonus[p]) * g_all[p]).astype(BF16)


def _rwkv_mix(z, shift_prev, s0, mu, w0, w2p, a0, a2p, g2, k_k, k_a, ln_w, ln_b, r_k, e2):
    bsz, t, _ = z.shape
    tt = min(RWKV_TT, t)
    lc = min(RWKV_CHUNK, t)
    rg = min(RWKV_GROUP, t)
    npair = RWKV_PAIRS_PER_STEP
    wide = npair * V7X_LANES
    zpair = lambda blk: pl.BlockSpec((1, tt, wide), lambda i, p, j: (i, j, blk // npair + p))
    zshared = lambda blk: pl.BlockSpec((1, tt, V7X_LANES), lambda i, p, j: (i, j, blk))
    par_spec = pl.BlockSpec((npair, 1, V7X_LANES), lambda i, p, j: (p, 0, 0))
    lora_spec = pl.BlockSpec((V7X_LANES, wide), lambda i, p, j: (0, p))
    st_spec = pl.BlockSpec((1, 2 * npair, R_HD, R_HD), lambda i, p, j: (i, p, 0, 0))
    pair = lambda x: x.reshape(R_PAIRS, 1, V7X_LANES)

    def pieces(x):
        lead = x.shape[:-1]
        rkv = x[..., :3 * R_W].reshape(lead + (3, R_PAIRS, V7X_LANES))
        rkv = jnp.moveaxis(rkv, -3, -2)
        lora = jnp.broadcast_to(x[..., 3 * R_W:].reshape(lead + (1, 2, V7X_LANES)), lead + (R_PAIRS, 2, V7X_LANES))
        return jnp.concatenate([rkv, lora], axis=-2)

    return pl.pallas_call(
        functools.partial(_rwkv_mix_kernel, npair=npair, tt=tt, lc=lc, rg=rg),
        grid=(bsz, R_PAIRS // npair, t // tt),
        in_specs=[
            zpair(ZC_RW), zpair(ZC_RW + 8), zpair(ZC_RW + 16), zshared(ZC_RW + 24), zshared(ZC_RW + 25),
            pl.BlockSpec((1, npair, 5, V7X_LANES), lambda i, p, j: (i, p, 0, 0)),
            pl.BlockSpec((npair, 5, V7X_LANES), lambda i, p, j: (p, 0, 0)),
            par_spec, lora_spec, par_spec, lora_spec, lora_spec, par_spec, par_spec,
            st_spec, par_spec, par_spec, par_spec,
            pl.BlockSpec((V7X_LANES, V7X_LANES), lambda i, p, j: (0, 0)),
        ],
        out_specs=[
            pl.BlockSpec((1, tt, wide), lambda i, p, j: (i, j, p)),
            st_spec,
        ],
        out_shape=[
            jax.ShapeDtypeStruct((bsz, t, R_W), BF16),
            jax.ShapeDtypeStruct((bsz, R_HEADS, R_HD, R_HD), F32),
        ],
        scratch_shapes=[pltpu.VMEM((tt, wide), F32), pltpu.VMEM((npair, 5, V7X_LANES), F32)],
        compiler_params=_cparams(("parallel", "parallel", "arbitrary")),
        name="rwkv_mix",
    )(z, z, z, z, z, pieces(shift_prev.reshape(bsz, R_IN)), pieces(mu), pair(w0), w2p, pair(a0), a2p, g2,
      pair(k_k), pair(k_a), s0, pair(ln_w), pair(ln_b), pair(r_k), e2)


def _merge_kernel(x_ref, zg_ref, hl_ref, om_ref, or_ref, wb_ref, wo_ref, o_ref):
    acc = None
    for gidx, br in enumerate((hl_ref, om_ref, or_ref)):
        sl = slice(gidx * D_MODEL, (gidx + 1) * D_MODEL)
        term = _sigmoid(zg_ref[:, sl].astype(F32)) * _dot(br[...], wb_ref[gidx])
        acc = term if acc is None else acc + term
    o_ref[...] = x_ref[...] + _dot(acc.astype(BF16), wo_ref[...])


def _merge(x, z2d, hl, om, orw, wb, wo):
    m = x.shape[0]
    tm = min(512, m)
    tok = pl.BlockSpec((tm, D_MODEL), lambda i: (i, 0))
    return pl.pallas_call(
        _merge_kernel,
        grid=(m // tm,),
        in_specs=[
            tok,
            pl.BlockSpec((tm, 3 * D_MODEL), lambda i: (i, ZC_GATE * V7X_LANES // (3 * D_MODEL))),
            tok, tok, tok,
            pl.BlockSpec((3, D_MODEL, D_MODEL), lambda i: (0, 0, 0)),
            pl.BlockSpec((D_MODEL, D_MODEL), lambda i: (0, 0)),
        ],
        out_specs=tok,
        out_shape=jax.ShapeDtypeStruct((m, D_MODEL), F32),
        compiler_params=_cparams(("parallel",)),
        name="merge",
    )(x, z2d, hl, om, orw, wb, wo)


def _prep_layer_weights(p, l):
    w_in = p["w_in"][l]
    c0, c1, c2, c3, c4 = 3072, 4096, 5120, 5128, 8456
    w_if = jnp.pad(w_in[:, c2:c3], ((0, 0), (0, V7X_LANES - 2 * M_HEADS))).astype(BF16)
    w_all = jnp.concatenate([w_in[:, :c0], w_in[:, c4:], w_in[:, c0:c1], w_in[:, c1:c2], w_in[:, c3:c4]],
                            axis=1).astype(BF16)
    def pairs(w):
        z = jnp.zeros((LRU_BD, LRU_BD), F32)
        return jnp.stack([jnp.block([[w[2 * i], z], [z, w[2 * i + 1]]]) for i in range(LRU_BLOCKS // 2)])
    w2 = jnp.concatenate([pairs(p["lru_wa"][l]), pairs(p["lru_wx"][l])], axis=2).astype(BF16)
    bax = jnp.stack([p["lru_ba"][l], p["lru_bx"][l]])
    if_bias = jnp.pad(p["mlstm_if_bias"][l], (0, V7X_LANES - 2 * M_HEADS)).reshape(1, V7X_LANES)
    zpad = jnp.zeros((64, R_W), F32)
    w2p = jnp.concatenate([p["rwkv_w2"][l], zpad], axis=0).astype(BF16)
    a2p = jnp.concatenate([zpad, p["rwkv_a2"][l]], axis=0).astype(BF16)
    return dict(
        w_all=w_all, w_if=w_if, w2=w2, bax=bax, if_bias=if_bias, w2p=w2p, a2p=a2p,
        g2=p["rwkv_g2"][l].astype(BF16),
        wb=p["w_branch"][l].astype(BF16), wo=p["w_out"][l].astype(BF16),
    )


def _segment_matrix():
    half = jnp.arange(V7X_LANES) // R_HD
    return (half[:, None] == half[None, :]).astype(BF16)


def _run_group(x, states, p, lw_list, ffn_w, final_norm):
    bsz, t, _ = x.shape
    m = bsz * t
    e2 = _segment_matrix()
    xf = x.reshape(m, D_MODEL)
    new_states = []
    depth = p["w_in"].shape[0]
    for l in range(depth):
        lw = lw_list[l]
        conv_prev, lru_h, m_c, m_n, m_m, shift_prev, rwkv_s = states[l]
        xf = _ffn(xf, p["ffn1_norm"][l], *ffn_w[0], layer=l)
        z2d, zif = _inproj(xf, p["mix_norm"][l], lw["w_all"], lw["w_if"])
        z = z2d.reshape(bsz, t, Z_W)
        hl, qk, h_last = _conv_lru(z, conv_prev, lru_h, p["conv_w"][l], p["conv_b"][l], lw["w2"], lw["bax"],
                                   p["lru_lambda"][l])
        om, c_new, n_new, m_new = _mlstm(qk, z, zif.reshape(bsz, t, V7X_LANES), m_c, m_n, m_m, lw["if_bias"],
                                         p["mlstm_norm"][l])
        orw, s_new = _rwkv_mix(z, shift_prev, rwkv_s, p["rwkv_mu"][l], p["rwkv_w0"][l], lw["w2p"], p["rwkv_a0"][l],
                               lw["a2p"], lw["g2"], p["rwkv_k_k"][l], p["rwkv_k_a"][l], p["rwkv_ln_w"][l],
                               p["rwkv_ln_b"][l], p["rwkv_r_k"][l].reshape(-1), e2)
        xf = _merge(xf, z2d, hl.reshape(m, D_MODEL), om.reshape(m, D_MODEL), orw.reshape(m, D_MODEL),
                    lw["wb"], lw["wo"])
        xf = _ffn(xf, p["ffn2_norm"][l], *ffn_w[1], layer=l, final_g=final_norm if l == depth - 1 else None)
        conv_new = z[:, t - (CONV_W - 1):, ZC_CONV * V7X_LANES:ZC_CONV * V7X_LANES + CONV_CH].astype(F32)
        shift_new = z[:, t - 1:, ZC_RW * V7X_LANES:ZC_RW * V7X_LANES + R_IN].astype(F32)
        new_states.append((conv_new, h_last.reshape(bsz, LRU_W), c_new, n_new, m_new.reshape(bsz, M_HEADS),
                           shift_new, s_new))
    stacked = tuple(jnp.stack([st[i] for st in new_states]) for i in range(7))
    return xf.reshape(bsz, t, D_MODEL), stacked


def _zero_states(bsz):
    return (jnp.zeros((bsz, CONV_W - 1, CONV_CH), F32), jnp.zeros((bsz, LRU_W), F32),
            jnp.zeros((bsz, M_HEADS, M_HD, M_HD), F32), jnp.zeros((bsz, M_HEADS, M_HD), F32),
            jnp.zeros((bsz, M_HEADS), F32), jnp.zeros((bsz, 1, R_IN), F32),
            jnp.zeros((bsz, R_HEADS, R_HD, R_HD), F32))


def kernel(x_prompt, x_sample, state_conv, state_lru_h, state_mlstm_C, state_mlstm_n, state_mlstm_m, state_rwkv_shift, state_rwkv_S, ffn1_norm, ffn1_w_gate, ffn1_w_up, ffn1_w_down, mix_norm, w_in, conv_w, conv_b, lru_wa, lru_ba, lru_wx, lru_bx, lru_lambda, mlstm_if_bias, mlstm_norm, rwkv_mu, rwkv_w0, rwkv_w2, rwkv_a0, rwkv_a2, rwkv_g2, rwkv_k_k, rwkv_k_a, rwkv_r_k, rwkv_ln_w, rwkv_ln_b, w_branch, w_out, ffn2_norm, ffn2_w_gate, ffn2_w_up, ffn2_w_down, final_norm):
    p = dict(ffn1_norm=ffn1_norm, ffn1_w_gate=ffn1_w_gate, ffn1_w_up=ffn1_w_up, ffn1_w_down=ffn1_w_down,
             mix_norm=mix_norm, w_in=w_in, conv_w=conv_w, conv_b=conv_b,
             lru_wa=lru_wa, lru_ba=lru_ba, lru_wx=lru_wx, lru_bx=lru_bx, lru_lambda=lru_lambda,
             mlstm_if_bias=mlstm_if_bias, mlstm_norm=mlstm_norm,
             rwkv_mu=rwkv_mu, rwkv_w0=rwkv_w0, rwkv_w2=rwkv_w2, rwkv_a0=rwkv_a0, rwkv_a2=rwkv_a2,
             rwkv_g2=rwkv_g2, rwkv_k_k=rwkv_k_k, rwkv_k_a=rwkv_k_a, rwkv_r_k=rwkv_r_k,
             rwkv_ln_w=rwkv_ln_w, rwkv_ln_b=rwkv_ln_b, w_branch=w_branch, w_out=w_out,
             ffn2_norm=ffn2_norm, ffn2_w_gate=ffn2_w_gate, ffn2_w_up=ffn2_w_up, ffn2_w_down=ffn2_w_down)
    depth = w_in.shape[0]
    lw_list = [_prep_layer_weights(p, l) for l in range(depth)]
    ffn_w = [tuple(p[f"ffn{n}_w_{part}"].astype(BF16) for part in ("gate", "up", "down")) for n in (1, 2)]
    states_p = [_zero_states(x_prompt.shape[0]) for _ in range(depth)]
    y_p, st_p = _run_group(x_prompt, states_p, p, lw_list, ffn_w, final_norm)
    states_s = [(state_conv[l], state_lru_h[l], state_mlstm_C[l], state_mlstm_n[l], state_mlstm_m[l],
                 state_rwkv_shift[l], state_rwkv_S[l]) for l in range(depth)]
    y_s, st_s = _run_group(x_sample, states_s, p, lw_list, ffn_w, final_norm)
    return (y_p, y_s) + st_p + st_s
```

```python
import functools

import jax
import jax.numpy as jnp
from jax import lax
from jax.experimental import pallas as pl
from jax.experimental.pallas import tpu as pltpu

F32 = jnp.float32
BF16 = jnp.bfloat16

V7X_LANES = 128
V7X_SUBLANES = 8
V7X_MXU_WIDTH = 256
V7X_VMEM_BYTES = 64 * 1024 * 1024
VMEM_LIMIT = V7X_VMEM_BYTES * 7 // 8

D_MODEL = 1024
D_FF = 2816
CONV_W = 4
LRU_W = D_MODEL
LRU_BLOCKS = 8
LRU_BD = LRU_W // LRU_BLOCKS
LRU_PAIR = 2 * LRU_BD
LRU_C = 8.0
M_HEADS = 4
M_W = D_MODEL
M_HD = M_W // M_HEADS
R_HD = 64
R_W = D_MODEL
R_HEADS = R_W // R_HD
R_PAIRS = R_HEADS // 2
R_LORA = 64
R_IN = 3 * R_W + 2 * V7X_LANES
CONV_CH = LRU_W + 2 * M_W
RMS_EPS = 1e-6
MH_EPS = 1e-6
RWKV_GN_EPS = 64e-5
RWKV_DECAY_SCALE = 0.6065306597126334

ZC_CONV = 0
ZC_GATE = 24
ZC_MV = 48
ZC_MO = 56
ZC_RW = 64
Z_BLOCKS = 90
Z_W = Z_BLOCKS * V7X_LANES
Z_TN = 18 * V7X_LANES

TOKEN_TILE = 1024
MERGE_TILE = 512
FFN_SUB = 3 * V7X_MXU_WIDTH
CONV_TT = 512
CONV_ROWS = 128
MLSTM_CHUNK = 512
RWKV_CHUNK = 64
RWKV_GROUP = 128
RWKV_TT = 256
RWKV_PAIRS_PER_STEP = R_PAIRS

HI = lax.Precision.HIGHEST


def _cparams(sem):
    return pltpu.CompilerParams(dimension_semantics=sem, vmem_limit_bytes=VMEM_LIMIT)


def _rms(x, g):
    return (x * lax.rsqrt(jnp.mean(x * x, axis=-1, keepdims=True) + RMS_EPS)) * g


def _softplus(x):
    return jnp.maximum(x, 0.0) + jnp.log1p(jnp.exp(-jnp.abs(x)))


def _sigmoid(x):
    return jax.nn.sigmoid(x)


def _dot(a, b, prec=None):
    return jnp.dot(a, b, preferred_element_type=F32, precision=prec)


def _dot_nt(a, b, prec=None):
    return lax.dot_general(a, b, (((1,), (1,)), ((), ())), preferred_element_type=F32, precision=prec)


def _dot_tn(a, b, prec=None):
    return lax.dot_general(a, b, (((0,), (0,)), ((), ())), preferred_element_type=F32, precision=prec)


def _cumsum_rows(x):
    n = x.shape[0]
    row = lax.broadcasted_iota(jnp.int32, x.shape, 0)
    d = 1
    while d < n:
        x = x + jnp.where(row >= d, pltpu.roll(x, d, axis=0), 0.0)
        d *= 2
    return x


def _ffn_kernel(x_ref, g_ref, wg_ref, wu_ref, wd_ref, *rest, final_norm):
    if final_norm:
        fg_ref, o_ref = rest
    else:
        (o_ref,) = rest
    x = x_ref[...]
    xn = _rms(x, g_ref[...]).astype(BF16)
    acc = None
    for c0 in range(0, D_FF, FFN_SUB):
        c1 = min(c0 + FFN_SUB, D_FF)
        hg = _dot(xn, wg_ref[:, c0:c1])
        hu = _dot(xn, wu_ref[:, c0:c1])
        h = (hg * _sigmoid(hg)) * hu
        d = _dot(h.astype(BF16), wd_ref[c0:c1, :])
        acc = d if acc is None else acc + d
    y = x + 0.5 * acc
    if final_norm:
        y = _rms(y, fg_ref[...])
    o_ref[...] = y


def _ffn(x, g, wg, wu, wd, layer, final_g=None):
    m = x.shape[0]
    tm = min(TOKEN_TILE, m)
    resident = pl.Buffered(1)
    in_specs = [
        pl.BlockSpec((tm, D_MODEL), lambda i: (i, 0)),
        pl.BlockSpec((1, D_MODEL), lambda i: (0, 0)),
        pl.BlockSpec((None, D_MODEL, D_FF), lambda i: (layer, 0, 0), pipeline_mode=resident),
        pl.BlockSpec((None, D_MODEL, D_FF), lambda i: (layer, 0, 0), pipeline_mode=resident),
        pl.BlockSpec((None, D_FF, D_MODEL), lambda i: (layer, 0, 0), pipeline_mode=resident),
    ]
    args = [x, g.reshape(1, D_MODEL), wg, wu, wd]
    if final_g is not None:
        in_specs.append(pl.BlockSpec((1, D_MODEL), lambda i: (0, 0)))
        args.append(final_g.reshape(1, D_MODEL))
    return pl.pallas_call(
        functools.partial(_ffn_kernel, final_norm=final_g is not None),
        grid=(m // tm,),
        in_specs=in_specs,
        out_specs=pl.BlockSpec((tm, D_MODEL), lambda i: (i, 0)),
        out_shape=jax.ShapeDtypeStruct((m, D_MODEL), F32),
        compiler_params=_cparams(("parallel",)),
        name="ffn",
    )(*args)


def _inproj_kernel(x_ref, g_ref, w_ref, wif_ref, o_ref, oif_ref, xn_ref):
    @pl.when(pl.program_id(1) == 0)
    def _():
        xn = _rms(x_ref[...], g_ref[...]).astype(BF16)
        xn_ref[...] = xn
        oif_ref[...] = _dot(xn, wif_ref[...])

    o_ref[...] = _dot(xn_ref[...], w_ref[...]).astype(BF16)


def _inproj(x, g, w_all, w_if):
    m = x.shape[0]
    tm = min(TOKEN_TILE, m)
    return pl.pallas_call(
        _inproj_kernel,
        grid=(m // tm, Z_W // Z_TN),
        in_specs=[
            pl.BlockSpec((tm, D_MODEL), lambda i, j: (i, 0)),
            pl.BlockSpec((1, D_MODEL), lambda i, j: (0, 0)),
            pl.BlockSpec((D_MODEL, Z_TN), lambda i, j: (0, j)),
            pl.BlockSpec((D_MODEL, V7X_LANES), lambda i, j: (0, 0)),
        ],
        out_specs=[
            pl.BlockSpec((tm, Z_TN), lambda i, j: (i, j)),
            pl.BlockSpec((tm, V7X_LANES), lambda i, j: (i, 0)),
        ],
        out_shape=[
            jax.ShapeDtypeStruct((m, Z_W), BF16),
            jax.ShapeDtypeStruct((m, V7X_LANES), F32),
        ],
        scratch_shapes=[pltpu.VMEM((tm, D_MODEL), BF16)],
        compiler_params=_cparams(("parallel", "arbitrary")),
        name="in_proj",
    )(x, g.reshape(1, D_MODEL), w_all, w_if)


def _conv_lru_kernel(z_ref, cprev_ref, h0_ref, cw_ref, cb_ref, w2_ref, bax_ref, lam_ref,
                     hl_ref, qk_ref, hlast_ref, xp_ref, a_ref, b_ref, hs_ref, hc_ref, *, tt):
    pad = V7X_SUBLANES

    @pl.when(pl.program_id(1) == 0)
    def _():
        xp_ref[0:pad, :] = jnp.zeros((pad, CONV_CH), F32)
        xp_ref[pad - 3:pad, :] = cprev_ref[0]
        hc_ref[...] = h0_ref[0]

    xp_ref[pad:pad + tt, :] = z_ref[0].astype(F32)
    sp = _softplus(-lam_ref[...])
    rb = min(CONV_ROWS, tt)
    for r0 in range(0, tt, rb):
        rows = slice(r0, r0 + rb)
        for cblk in range(CONV_CH // LRU_PAIR):
            cols = slice(cblk * LRU_PAIR, (cblk + 1) * LRU_PAIR)
            xe = xp_ref[r0:pad + r0 + rb, cols]
            c = xe * cw_ref[0:1, cols] + cb_ref[:, cols]
            c = pltpu.roll(c, 1, axis=0) + xe * cw_ref[1:2, cols]
            c = pltpu.roll(c, 1, axis=0) + xe * cw_ref[2:3, cols]
            c = (pltpu.roll(c, 1, axis=0) + xe * cw_ref[3:4, cols])[pad:]
            if cblk < LRU_W // LRU_PAIR:
                gpre = _dot(c.astype(BF16), w2_ref[cblk])
                r = _sigmoid(gpre[:, :LRU_PAIR] + bax_ref[0:1, cols])
                i = _sigmoid(gpre[:, LRU_PAIR:] + bax_ref[1:2, cols])
                log_a = (-LRU_C * r) * sp[:, cols]
                a = jnp.exp(log_a)
                mult = jnp.sqrt(-jnp.tanh(log_a) * (a * a + 1.0))
                a_ref[rows, cols] = a
                b_ref[rows, cols] = mult * (i * c)
            elif cblk < 2 * LRU_W // LRU_PAIR:
                qk_ref[0, rows, cblk * LRU_PAIR - LRU_W:(cblk + 1) * LRU_PAIR - LRU_W] = (
                    c * _sigmoid(c)).astype(BF16)
            else:
                qk_ref[0, rows, cblk * LRU_PAIR - LRU_W:(cblk + 1) * LRU_PAIR - LRU_W] = (
                    (c * _sigmoid(c)) * (M_HD ** -0.5)).astype(BF16)
    xp_ref[pad - 3:pad, :] = xp_ref[pad + tt - 3:pad + tt, :]

    def body(t, h):
        h = a_ref[pl.ds(t, 1), :] * h + b_ref[pl.ds(t, 1), :]
        hs_ref[pl.ds(t, 1), :] = h
        return h

    h = lax.fori_loop(0, tt, body, hc_ref[...], unroll=8)
    hc_ref[...] = h
    hlast_ref[0] = h
    hl_ref[0] = hs_ref[...].astype(BF16)


def _conv_lru(z, conv_prev, h0, cw, cb, w2, bax, lam):
    b, t, _ = z.shape
    tt = min(CONV_TT, t)
    return pl.pallas_call(
        functools.partial(_conv_lru_kernel, tt=tt),
        grid=(b, t // tt),
        in_specs=[
            pl.BlockSpec((1, tt, CONV_CH), lambda i, j: (i, j, ZC_CONV * V7X_LANES // CONV_CH)),
            pl.BlockSpec((1, CONV_W - 1, CONV_CH), lambda i, j: (i, 0, 0)),
            pl.BlockSpec((1, 1, LRU_W), lambda i, j: (i, 0, 0)),
            pl.BlockSpec((CONV_W, CONV_CH), lambda i, j: (0, 0)),
            pl.BlockSpec((1, CONV_CH), lambda i, j: (0, 0)),
            pl.BlockSpec((LRU_W // LRU_PAIR, LRU_PAIR, 2 * LRU_PAIR), lambda i, j: (0, 0, 0)),
            pl.BlockSpec((2, LRU_W), lambda i, j: (0, 0)),
            pl.BlockSpec((1, LRU_W), lambda i, j: (0, 0)),
        ],
        out_specs=[
            pl.BlockSpec((1, tt, LRU_W), lambda i, j: (i, j, 0)),
            pl.BlockSpec((1, tt, 2 * M_W), lambda i, j: (i, j, 0)),
            pl.BlockSpec((1, 1, LRU_W), lambda i, j: (i, 0, 0)),
        ],
        out_shape=[
            jax.ShapeDtypeStruct((b, t, LRU_W), BF16),
            jax.ShapeDtypeStruct((b, t, 2 * M_W), BF16),
            jax.ShapeDtypeStruct((b, 1, LRU_W), F32),
        ],
        scratch_shapes=[
            pltpu.VMEM((tt + V7X_SUBLANES, CONV_CH), F32),
            pltpu.VMEM((tt, LRU_W), F32),
            pltpu.VMEM((tt, LRU_W), F32),
            pltpu.VMEM((tt, LRU_W), F32),
            pltpu.VMEM((1, LRU_W), F32),
        ],
        compiler_params=_cparams(("parallel", "arbitrary")),
        name="conv_lru",
    )(z, conv_prev, h0.reshape(b, 1, LRU_W), cw, cb.reshape(1, CONV_CH), w2, bax, lam.reshape(1, LRU_W))


def _mlstm_kernel(qk_ref, v_ref, o_ref, if_ref, c0_ref, n0_ref, m0_ref, ifb_ref, nw_ref,
                  om_ref, c_ref, n_ref, m_ref, *, lc):
    @pl.when(pl.program_id(1) == 0)
    def _():
        c_ref[...] = c0_ref[...]
        n_ref[...] = n0_ref[...]
        m_ref[...] = m0_ref[...]

    gif = if_ref[0] + ifb_ref[...]
    lf = jnp.minimum(gif, 0.0) - jnp.log1p(jnp.exp(-jnp.abs(gif)))
    cum = _cumsum_rows(lf)
    src = gif - pltpu.roll(cum, V7X_LANES - M_HEADS, axis=1)
    src_t = src.T if lc % V7X_LANES == 0 else None
    lane = lax.broadcasted_iota(jnp.int32, (lc, V7X_LANES), 1)
    tpos = lax.broadcasted_iota(jnp.int32, (lc, lc), 0)
    spos = lax.broadcasted_iota(jnp.int32, (lc, lc), 1)
    causal = spos <= tpos
    m_prev_all = m_ref[0]
    mlane = lax.broadcasted_iota(jnp.int32, (1, M_HEADS), 1)
    m_new_all = m_prev_all

    heads = range(M_HEADS)
    sl = lambda h: slice(h * M_HD, (h + 1) * M_HD)
    q = {h: qk_ref[0, :, sl(h)] for h in heads}
    k = {h: qk_ref[0, :, M_HEADS * M_HD + h * M_HD:M_HEADS * M_HD + (h + 1) * M_HD] for h in heads}
    vb = {h: v_ref[0, :, sl(h)] for h in heads}
    bc = {h: cum[:, M_HEADS + h:M_HEADS + h + 1] for h in heads}
    ig = {h: gif[:, h:h + 1] for h in heads}
    m_prev = {h: m_prev_all[:, h:h + 1] for h in heads}
    c_prev = {h: c_ref[0, h] for h in heads}
    n_prev = {h: n_ref[0, h:h + 1, :] for h in heads}
    if src_t is not None:
        rowv = {h: src_t[h:h + 1, :] for h in heads}
    else:
        rowv = {h: _dot_nt(jnp.where(lane == h, 1.0, 0.0), src, HI) for h in heads}
    dmat = {h: jnp.where(causal, bc[h] + rowv[h], -jnp.inf) for h in heads}
    inter = {h: bc[h] + m_prev[h] for h in heads}
    m_t = {h: jnp.maximum(inter[h], jnp.max(dmat[h], axis=1, keepdims=True)) for h in heads}
    qk = {h: _dot_nt(q[h], k[h]) for h in heads}
    qc = {h: _dot_nt(q[h], c_prev[h].astype(BF16)) for h in heads}
    nrows = {h: jnp.broadcast_to(n_prev[h], (V7X_SUBLANES, M_HD)).astype(BF16) for h in heads}
    qn = {h: _dot_nt(q[h], nrows[h])[:, 0:1] for h in heads}
    w_inter = {h: jnp.exp(inter[h] - m_t[h]) for h in heads}
    s = {h: qk[h] * jnp.exp(dmat[h] - m_t[h]) for h in heads}
    num = {h: _dot(s[h].astype(BF16), vb[h]) + w_inter[h] * qc[h] for h in heads}
    den = {h: jnp.sum(s[h], axis=1, keepdims=True) + w_inter[h] * qn[h] for h in heads}
    hh = {h: num[h] / jnp.maximum(jnp.abs(den[h]), jnp.exp(-m_t[h])) for h in heads}
    mean = {h: jnp.mean(hh[h], axis=1, keepdims=True) for h in heads}
    d = {h: hh[h] - mean[h] for h in heads}
    var = {h: jnp.mean(d[h] * d[h], axis=1, keepdims=True) for h in heads}
    for h in heads:
        hn = d[h] * lax.rsqrt(var[h] + MH_EPS)
        om_ref[0, :, sl(h)] = (_sigmoid(o_ref[0, :, sl(h)].astype(F32)) * (hn * nw_ref[:, sl(h)])).astype(BF16)
    m_new = {h: m_t[h][lc - 1:lc, :] for h in heads}
    bl = {h: bc[h][lc - 1:lc, :] for h in heads}
    g_state = {h: jnp.exp(bl[h] + m_prev[h] - m_new[h]) for h in heads}
    g_src = {h: jnp.exp(bl[h] - bc[h] + ig[h] - m_new[h]) for h in heads}
    for h in heads:
        c_ref[0, h] = g_state[h] * c_prev[h] + _dot_tn((g_src[h] * vb[h].astype(F32)).astype(BF16), k[h])
        n_ref[0, h:h + 1, :] = g_state[h] * n_prev[h] + jnp.sum(g_src[h] * k[h].astype(F32), axis=0, keepdims=True)
        m_new_all = jnp.where(mlane == h, m_new[h], m_new_all)
    m_ref[0] = m_new_all


def _mlstm(qk, z, zif, c0, n0, m0, if_bias, norm_w):
    b, t, _ = z.shape
    lc = min(MLSTM_CHUNK, t)
    w = M_W
    return pl.pallas_call(
        functools.partial(_mlstm_kernel, lc=lc),
        grid=(b, t // lc),
        in_specs=[
            pl.BlockSpec((1, lc, 2 * w), lambda i, j: (i, j, 0)),
            pl.BlockSpec((1, lc, w), lambda i, j: (i, j, ZC_MV * V7X_LANES // w)),
            pl.BlockSpec((1, lc, w), lambda i, j: (i, j, ZC_MO * V7X_LANES // w)),
            pl.BlockSpec((1, lc, V7X_LANES), lambda i, j: (i, j, 0)),
            pl.BlockSpec((1, M_HEADS, M_HD, M_HD), lambda i, j: (i, 0, 0, 0)),
            pl.BlockSpec((1, M_HEADS, M_HD), lambda i, j: (i, 0, 0)),
            pl.BlockSpec((1, 1, M_HEADS), lambda i, j: (i, 0, 0)),
            pl.BlockSpec((1, V7X_LANES), lambda i, j: (0, 0)),
            pl.BlockSpec((1, w), lambda i, j: (0, 0)),
        ],
        out_specs=[
            pl.BlockSpec((1, lc, w), lambda i, j: (i, j, 0)),
            pl.BlockSpec((1, M_HEADS, M_HD, M_HD), lambda i, j: (i, 0, 0, 0)),
            pl.BlockSpec((1, M_HEADS, M_HD), lambda i, j: (i, 0, 0)),
            pl.BlockSpec((1, 1, M_HEADS), lambda i, j: (i, 0, 0)),
        ],
        out_shape=[
            jax.ShapeDtypeStruct((b, t, w), BF16),
            jax.ShapeDtypeStruct((b, M_HEADS, M_HD, M_HD), F32),
            jax.ShapeDtypeStruct((b, M_HEADS, M_HD), F32),
            jax.ShapeDtypeStruct((b, 1, M_HEADS), F32),
        ],
        compiler_params=_cparams(("parallel", "arbitrary")),
        name="mlstm",
    )(qk, z, z, zif, c0, n0, m0.reshape(b, 1, M_HEADS), if_bias, norm_w.reshape(1, w))


def _rwkv_mix_kernel(zr_ref, zk_ref, zv_ref, zwa_ref, zgd_ref, sh_ref, mu_ref, w0_ref, w2_ref, a0_ref, a2_ref, g2_ref,
                     kk_ref, ka_ref, s0_ref, lnw_ref, lnb_ref, rk_ref, e2_ref,
                     o_ref, s_ref, y_ref, car_ref, *, npair, tt, lc, rg):
    @pl.when(pl.program_id(2) == 0)
    def _():
        s_ref[...] = s0_ref[...]
        car_ref[...] = sh_ref[0]

    def seg(x):
        return _dot(x.astype(BF16), e2_ref[...])

    first_row = lax.broadcasted_iota(jnp.int32, (tt, 1), 0) == 0
    pairs = range(npair)
    ps = lambda p: slice(p * V7X_LANES, (p + 1) * V7X_LANES)
    hs = lambda h: slice(h * R_HD, (h + 1) * R_HD)

    def shifted(zp, p, i):
        zp = zp.astype(F32)
        prev = jnp.where(first_row, car_ref[p, i:i + 1, :], pltpu.roll(zp, 1, axis=0))
        car_ref[p, i:i + 1, :] = zp[tt - 1:tt, :]
        return zp + (prev - zp) * mu_ref[p, i:i + 1, :]

    xwa = shifted(zwa_ref[0], 0, 3)
    xgd = shifted(zgd_ref[0], 0, 4)
    twa_b = jnp.tanh(xwa).astype(BF16)
    xwa_b = xwa.astype(BF16)
    sgd_b = _sigmoid(xgd).astype(BF16)
    r_all, k_all, v_all, lw_all, a_all, b_all, g_all = {}, {}, {}, {}, {}, {}, {}

    def prologue(p):
        r_all[p] = shifted(zr_ref[0, :, ps(p)], p, 0)
        k_in = shifted(zk_ref[0, :, ps(p)], p, 1)
        v_all[p] = shifted(zv_ref[0, :, ps(p)], p, 2)
        lw_all[p] = -RWKV_DECAY_SCALE * _sigmoid(w0_ref[p] + _dot(twa_b, w2_ref[:, ps(p)]))
        a_gate = _sigmoid(a0_ref[p] + _dot(xwa_b, a2_ref[:, ps(p)]))
        g_all[p] = _dot(sgd_b, g2_ref[:, ps(p)])
        kk = k_in * kk_ref[p]
        kk = kk * lax.rsqrt(jnp.maximum(seg(kk * kk), 1e-24))
        k_all[p] = k_in * (1.0 + (a_gate - 1.0) * ka_ref[p])
        a_all[p] = -kk
        b_all[p] = kk * a_gate

    nb = rg // lc
    urow = lax.broadcasted_iota(jnp.int32, (2 * rg, 2 * rg), 0)
    scol = lax.broadcasted_iota(jnp.int32, (2 * rg, 2 * rg), 1)
    incl = urow >= rg
    trow = jnp.where(incl, urow - rg, urow)
    scol = jnp.where(scol >= rg, scol - rg, scol)
    dist = trow - scol
    gmask = (dist >= jnp.where(incl, 0, 1)) & (dist <= (trow & (lc - 1)))
    lsh = lc.bit_length() - 1
    qrow = lax.broadcasted_iota(jnp.int32, (rg, rg), 0)
    qcol = lax.broadcasted_iota(jnp.int32, (rg, rg), 1)
    same = (qrow >> lsh) == (qcol >> lsh)
    blk = jnp.where(same, 1.0, 0.0).astype(BF16)
    tril = jnp.where(qcol <= qrow, blk, jnp.zeros((), BF16))
    xrow = lax.broadcasted_iota(jnp.int32, (rg, nb * R_HD), 0)
    xcol = lax.broadcasted_iota(jnp.int32, (rg, nb * R_HD), 1)
    xmask = (xrow >> (lc.bit_length() - 1)) == (xcol >> (R_HD.bit_length() - 1))
    n_sq = max(lc.bit_length() - 1, 0)

    groups = range(tt // rg)
    ew = {}

    def prepare(p):
        for g in groups:
            rows = slice(g * rg, (g + 1) * rg)
            lw = lw_all[p][rows]
            k = k_all[p][rows]
            b = b_all[p][rows]
            lw_hi = lw.astype(BF16)
            lw_lo = (lw - lw_hi.astype(F32)).astype(BF16)
            cum = _dot(tril, lw_hi) + _dot(tril, lw_lo)
            ends = [cum[c * lc + lc - 1:c * lc + lc, :] for c in range(nb)]
            cum_end = ends[0] if nb == 1 else jnp.concatenate(
                [jnp.broadcast_to(e, (lc, V7X_LANES)) for e in ends], axis=0)
            e_neg = jnp.exp(-cum)
            at = a_all[p][rows] * jnp.exp(cum - lw)
            rt = r_all[p][rows] * jnp.exp(cum)
            e_end = jnp.exp(cum_end - cum)
            ew[p, g] = dict(at=at, rt=rt, lhs=jnp.concatenate([at, rt], axis=0).astype(BF16),
                            rhs=jnp.concatenate([b * e_neg, k * e_neg], axis=0).astype(BF16),
                            be=(b * e_end).astype(BF16), ke=(k * e_end).astype(BF16),
                            vb=v_all[p][rows].astype(BF16),
                            w_end=[jnp.exp(e) for e in ends])

    def expand(x):
        if nb == 1:
            return x
        return jnp.where(xmask, jnp.concatenate([x] * nb, axis=1), jnp.zeros((), x.dtype))

    piece = lambda c, name: ew[c[0], c[1]][name][:, hs(c[2])]
    pq2, xe, vk = {}, {}, {}

    def solve(chains):
        gm = {c: jnp.where(gmask, _dot_nt(piece(c, "lhs"), piece(c, "rhs")), 0.0) for c in chains}
        gv = {c: _dot(gm[c][:, rg:].astype(BF16), piece(c, "vb")) for c in chains}
        gr = {c: gm[c][rg:] for c in chains}
        gvr = {c: gv[c][rg:] for c in chains}
        x = {c: jnp.concatenate([piece(c, "at"), gv[c][:rg]], axis=1) for c in chains}
        pw = {c: gm[c][:rg, :rg] for c in chains}
        for i in range(n_sq):
            pwb = {c: pw[c].astype(BF16) for c in chains}
            x = {c: x[c] + _dot(pwb[c], x[c].astype(BF16)) for c in chains}
            if i + 1 < n_sq:
                pw = {c: _dot(pwb[c], pwb[c]) for c in chains}
        xb = {c: x[c].astype(BF16) for c in chains}
        for c in chains:
            pq2[c] = jnp.concatenate([piece(c, "rt"), gvr[c]], axis=1) + _dot(gr[c][:, :rg].astype(BF16), xb[c])
            xe[c] = _dot_tn(xb[c], expand(piece(c, "be")))
            vk[c] = _dot_tn(piece(c, "vb"), expand(piece(c, "ke")))

    for p in pairs:
        prologue(p)
    for p in pairs:
        prepare(p)
    solve([(p, g, h) for p in pairs for g in groups for h in range(2)])

    bonus = {p: seg((r_all[p] * k_all[p]) * rk_ref[p]) * v_all[p] for p in pairs}

    s = {(p, h): s_ref[0, 2 * p + h] for p in pairs for h in range(2)}
    for g in groups:
        for c in range(nb):
            rows = slice(c * lc, (c + 1) * lc)
            cols = slice(c * R_HD, (c + 1) * R_HD)
            sb = {ph: s[ph].astype(BF16) for ph in s}
            for p in pairs:
                ys = [_dot_nt(pq2[p, g, h][rows, :R_HD].astype(BF16), sb[p, h]) + pq2[p, g, h][rows, R_HD:]
                      for h in range(2)]
                y_ref[g * rg + c * lc:g * rg + (c + 1) * lc, ps(p)] = jnp.concatenate(ys, axis=1)
            s = {(p, h): (s[p, h] * ew[p, g]["w_end"][c][:, hs(h)]
                          + _dot(sb[p, h], xe[p, g, h][:R_HD, cols].astype(BF16))
                          + (xe[p, g, h][R_HD:, cols] + vk[p, g, h][:, cols])) for (p, h) in s}
    for (p, h) in s:
        s_ref[0, 2 * p + h] = s[p, h]

    y = {p: y_ref[:, ps(p)] for p in pairs}
    d = {p: y[p] - seg(y[p]) * (1.0 / R_HD) for p in pairs}
    var = {p: seg(d[p] * d[p]) * (1.0 / R_HD) for p in pairs}
    for p in pairs:
        yn = (d[p] * lax.rsqrt(var[p] + RWKV_GN_EPS)) * lnw_ref[p] + lnb_ref[p]
        o_ref[0, :, ps(p)] = ((yn + bonus[p]) * g_all[p]).astype(BF16)


def _rwkv_mix(z, shift_prev, s0, mu, w0, w2p, a0, a2p, g2, k_k, k_a, ln_w, ln_b, r_k, e2):
    bsz, t, _ = z.shape
    tt = min(RWKV_TT, t)
    lc = min(RWKV_CHUNK, t)
    rg = min(RWKV_GROUP, t)
    npair = RWKV_PAIRS_PER_STEP
    wide = npair * V7X_LANES
    zpair = lambda blk: pl.BlockSpec((1, tt, wide), lambda i, p, j: (i, j, blk // npair + p))
    zshared = lambda blk: pl.BlockSpec((1, tt, V7X_LANES), lambda i, p, j: (i, j, blk))
    par_spec = pl.BlockSpec((npair, 1, V7X_LANES), lambda i, p, j: (p, 0, 0))
    lora_spec = pl.BlockSpec((V7X_LANES, wide), lambda i, p, j: (0, p))
    st_spec = pl.BlockSpec((1, 2 * npair, R_HD, R_HD), lambda i, p, j: (i, p, 0, 0))
    pair = lambda x: x.reshape(R_PAIRS, 1, V7X_LANES)

    def pieces(x):
        lead = x.shape[:-1]
        rkv = x[..., :3 * R_W].reshape(lead + (3, R_PAIRS, V7X_LANES))
        rkv = jnp.moveaxis(rkv, -3, -2)
        lora = jnp.broadcast_to(x[..., 3 * R_W:].reshape(lead + (1, 2, V7X_LANES)), lead + (R_PAIRS, 2, V7X_LANES))
        return jnp.concatenate([rkv, lora], axis=-2)

    return pl.pallas_call(
        functools.partial(_rwkv_mix_kernel, npair=npair, tt=tt, lc=lc, rg=rg),
        grid=(bsz, R_PAIRS // npair, t // tt),
        in_specs=[
            zpair(ZC_RW), zpair(ZC_RW + 8), zpair(ZC_RW + 16), zshared(ZC_RW + 24), zshared(ZC_RW + 25),
            pl.BlockSpec((1, npair, 5, V7X_LANES), lambda i, p, j: (i, p, 0, 0)),
            pl.BlockSpec((npair, 5, V7X_LANES), lambda i, p, j: (p, 0, 0)),
            par_spec, lora_spec, par_spec, lora_spec, lora_spec, par_spec, par_spec,
            st_spec, par_spec, par_spec, par_spec,
            pl.BlockSpec((V7X_LANES, V7X_LANES), lambda i, p, j: (0, 0)),
        ],
        out_specs=[
            pl.BlockSpec((1, tt, wide), lambda i, p, j: (i, j, p)),
            st_spec,
        ],
        out_shape=[
            jax.ShapeDtypeStruct((bsz, t, R_W), BF16),
            jax.ShapeDtypeStruct((bsz, R_HEADS, R_HD, R_HD), F32),
        ],
        scratch_shapes=[pltpu.VMEM((tt, wide), F32), pltpu.VMEM((npair, 5, V7X_LANES), F32)],
        compiler_params=_cparams(("parallel", "parallel", "arbitrary")),
        name="rwkv_mix",
    )(z, z, z, z, z, pieces(shift_prev.reshape(bsz, R_IN)), pieces(mu), pair(w0), w2p, pair(a0), a2p, g2,
      pair(k_k), pair(k_a), s0, pair(ln_w), pair(ln_b), pair(r_k), e2)


def _merge_kernel(x_ref, zg_ref, hl_ref, om_ref, or_ref, wb_ref, wo_ref, o_ref):
    acc = None
    for gidx, br in enumerate((hl_ref, om_ref, or_ref)):
        sl = slice(gidx * D_MODEL, (gidx + 1) * D_MODEL)
        term = _sigmoid(zg_ref[:, sl].astype(F32)) * _dot(br[...], wb_ref[gidx])
        acc = term if acc is None else acc + term
    o_ref[...] = x_ref[...] + _dot(acc.astype(BF16), wo_ref[...])


def _merge(x, z2d, hl, om, orw, wb, wo):
    m = x.shape[0]
    tm = min(MERGE_TILE, m)
    tok = pl.BlockSpec((tm, D_MODEL), lambda i: (i, 0))
    return pl.pallas_call(
        _merge_kernel,
        grid=(m // tm,),
        in_specs=[
            tok,
            pl.BlockSpec((tm, 3 * D_MODEL), lambda i: (i, ZC_GATE * V7X_LANES // (3 * D_MODEL))),
            tok, tok, tok,
            pl.BlockSpec((3, D_MODEL, D_MODEL), lambda i: (0, 0, 0)),
            pl.BlockSpec((D_MODEL, D_MODEL), lambda i: (0, 0)),
        ],
        out_specs=tok,
        out_shape=jax.ShapeDtypeStruct((m, D_MODEL), F32),
        compiler_params=_cparams(("parallel",)),
        name="merge",
    )(x, z2d, hl, om, orw, wb, wo)


def _prep_layer_weights(p, l):
    w_in = p["w_in_bf16"][l]
    c0 = CONV_CH
    c1 = c0 + M_W
    c2 = c1 + M_W
    c3 = c2 + 2 * M_HEADS
    c4 = c3 + 3 * R_W + 2 * R_LORA + V7X_LANES
    w_if = jnp.pad(w_in[:, c2:c3], ((0, 0), (0, V7X_LANES - 2 * M_HEADS)))
    w_all = jnp.concatenate([w_in[:, :c0], w_in[:, c4:], w_in[:, c0:c1], w_in[:, c1:c2], w_in[:, c3:c4]], axis=1)
    def pairs(w):
        z = jnp.zeros((LRU_BD, LRU_BD), F32)
        return jnp.stack([jnp.block([[w[2 * i], z], [z, w[2 * i + 1]]]) for i in range(LRU_BLOCKS // 2)])
    w2 = jnp.concatenate([pairs(p["lru_wa"][l]), pairs(p["lru_wx"][l])], axis=2).astype(BF16)
    bax = jnp.stack([p["lru_ba"][l], p["lru_bx"][l]])
    if_bias = jnp.pad(p["mlstm_if_bias"][l], (0, V7X_LANES - 2 * M_HEADS)).reshape(1, V7X_LANES)
    zpad = jnp.zeros((R_LORA, R_W), F32)
    w2p = jnp.concatenate([p["rwkv_w2"][l], zpad], axis=0).astype(BF16)
    a2p = jnp.concatenate([zpad, p["rwkv_a2"][l]], axis=0).astype(BF16)
    return dict(
        w_all=w_all, w_if=w_if, w2=w2, bax=bax, if_bias=if_bias, w2p=w2p, a2p=a2p,
        g2=p["rwkv_g2"][l].astype(BF16),
        wb=p["w_branch"][l].astype(BF16), wo=p["w_out"][l].astype(BF16),
    )


def _segment_matrix():
    half = jnp.arange(V7X_LANES) // R_HD
    return (half[:, None] == half[None, :]).astype(BF16)


def _run_group(x, states, p, lw_list, ffn_w, final_norm):
    bsz, t, d_model = x.shape
    m = bsz * t
    assert d_model == D_MODEL and t >= CONV_W - 1
    assert all(t % min(tile, t) == 0 for tile in (CONV_TT, MLSTM_CHUNK, RWKV_TT, RWKV_GROUP, RWKV_CHUNK))
    assert all(m % min(tile, m) == 0 for tile in (TOKEN_TILE, MERGE_TILE))
    e2 = _segment_matrix()
    xf = x.reshape(m, D_MODEL)
    new_states = []
    depth = p["w_in"].shape[0]
    for l in range(depth):
        lw = lw_list[l]
        conv_prev, lru_h, m_c, m_n, m_m, shift_prev, rwkv_s = states[l]
        xf = _ffn(xf, p["ffn1_norm"][l], *ffn_w[0], layer=l)
        z2d, zif = _inproj(xf, p["mix_norm"][l], lw["w_all"], lw["w_if"])
        z = z2d.reshape(bsz, t, Z_W)
        hl, qk, h_last = _conv_lru(z, conv_prev, lru_h, p["conv_w"][l], p["conv_b"][l], lw["w2"], lw["bax"],
                                   p["lru_lambda"][l])
        om, c_new, n_new, m_new = _mlstm(qk, z, zif.reshape(bsz, t, V7X_LANES), m_c, m_n, m_m, lw["if_bias"],
                                         p["mlstm_norm"][l])
        orw, s_new = _rwkv_mix(z, shift_prev, rwkv_s, p["rwkv_mu"][l], p["rwkv_w0"][l], lw["w2p"], p["rwkv_a0"][l],
                               lw["a2p"], lw["g2"], p["rwkv_k_k"][l], p["rwkv_k_a"][l], p["rwkv_ln_w"][l],
                               p["rwkv_ln_b"][l], p["rwkv_r_k"][l].reshape(-1), e2)
        xf = _merge(xf, z2d, hl.reshape(m, D_MODEL), om.reshape(m, D_MODEL), orw.reshape(m, D_MODEL),
                    lw["wb"], lw["wo"])
        xf = _ffn(xf, p["ffn2_norm"][l], *ffn_w[1], layer=l, final_g=final_norm if l == depth - 1 else None)
        conv_new = z[:, t - (CONV_W - 1):, ZC_CONV * V7X_LANES:ZC_CONV * V7X_LANES + CONV_CH].astype(F32)
        shift_new = z[:, t - 1:, ZC_RW * V7X_LANES:ZC_RW * V7X_LANES + R_IN].astype(F32)
        new_states.append((conv_new, h_last.reshape(bsz, LRU_W), c_new, n_new, m_new.reshape(bsz, M_HEADS),
                           shift_new, s_new))
    stacked = tuple(jnp.stack([st[i] for st in new_states]) for i in range(7))
    return xf.reshape(bsz, t, D_MODEL), stacked


def _zero_states(bsz):
    return (jnp.zeros((bsz, CONV_W - 1, CONV_CH), F32), jnp.zeros((bsz, LRU_W), F32),
            jnp.zeros((bsz, M_HEADS, M_HD, M_HD), F32), jnp.zeros((bsz, M_HEADS, M_HD), F32),
            jnp.zeros((bsz, M_HEADS), F32), jnp.zeros((bsz, 1, R_IN), F32),
            jnp.zeros((bsz, R_HEADS, R_HD, R_HD), F32))


def kernel(x_prompt, x_sample, state_conv, state_lru_h, state_mlstm_C, state_mlstm_n, state_mlstm_m, state_rwkv_shift, state_rwkv_S, ffn1_norm, ffn1_w_gate, ffn1_w_up, ffn1_w_down, mix_norm, w_in, conv_w, conv_b, lru_wa, lru_ba, lru_wx, lru_bx, lru_lambda, mlstm_if_bias, mlstm_norm, rwkv_mu, rwkv_w0, rwkv_w2, rwkv_a0, rwkv_a2, rwkv_g2, rwkv_k_k, rwkv_k_a, rwkv_r_k, rwkv_ln_w, rwkv_ln_b, w_branch, w_out, ffn2_norm, ffn2_w_gate, ffn2_w_up, ffn2_w_down, final_norm):
    p = dict(ffn1_norm=ffn1_norm, ffn1_w_gate=ffn1_w_gate, ffn1_w_up=ffn1_w_up, ffn1_w_down=ffn1_w_down,
             mix_norm=mix_norm, w_in=w_in, conv_w=conv_w, conv_b=conv_b,
             lru_wa=lru_wa, lru_ba=lru_ba, lru_wx=lru_wx, lru_bx=lru_bx, lru_lambda=lru_lambda,
             mlstm_if_bias=mlstm_if_bias, mlstm_norm=mlstm_norm,
             rwkv_mu=rwkv_mu, rwkv_w0=rwkv_w0, rwkv_w2=rwkv_w2, rwkv_a0=rwkv_a0, rwkv_a2=rwkv_a2,
             rwkv_g2=rwkv_g2, rwkv_k_k=rwkv_k_k, rwkv_k_a=rwkv_k_a, rwkv_r_k=rwkv_r_k,
             rwkv_ln_w=rwkv_ln_w, rwkv_ln_b=rwkv_ln_b, w_branch=w_branch, w_out=w_out,
             ffn2_norm=ffn2_norm, ffn2_w_gate=ffn2_w_gate, ffn2_w_up=ffn2_w_up, ffn2_w_down=ffn2_w_down)
    depth = w_in.shape[0]
    p["w_in_bf16"] = w_in.astype(BF16)
    lw_list = [_prep_layer_weights(p, l) for l in range(depth)]
    ffn_w = [tuple(p[f"ffn{n}_w_{part}"].astype(BF16) for part in ("gate", "up", "down")) for n in (1, 2)]
    states_p = [_zero_states(x_prompt.shape[0]) for _ in range(depth)]
    y_p, st_p = _run_group(x_prompt, states_p, p, lw_list, ffn_w, final_norm)
    states_s = [(state_conv[l], state_lru_h[l], state_mlstm_C[l], state_mlstm_n[l], state_mlstm_m[l],
                 state_rwkv_shift[l], state_rwkv_S[l]) for l in range(depth)]
    y_s, st_s = _run_group(x_sample, states_s, p, lw_list, ffn_w, final_norm)
    return (y_p, y_s) + st_p + st_s
```

```python
import functools

import jax
import jax.numpy as jnp
from jax import lax
from jax.experimental import pallas as pl
from jax.experimental.pallas import tpu as pltpu

F32 = jnp.float32
BF16 = jnp.bfloat16

V7X_LANES = 128
V7X_SUBLANES = 8
V7X_MXU_WIDTH = 256
V7X_VMEM_BYTES = 64 * 1024 * 1024
VMEM_LIMIT = V7X_VMEM_BYTES * 7 // 8

D_MODEL = 1024
D_FF = 2816
CONV_W = 4
LRU_W = D_MODEL
LRU_BLOCKS = 8
LRU_BD = LRU_W // LRU_BLOCKS
LRU_PAIR = 2 * LRU_BD
LRU_C = 8.0
M_HEADS = 4
M_W = D_MODEL
M_HD = M_W // M_HEADS
R_HD = 64
R_W = D_MODEL
R_HEADS = R_W // R_HD
R_PAIRS = R_HEADS // 2
R_LORA = 64
R_IN = 3 * R_W + 2 * V7X_LANES
CONV_CH = LRU_W + 2 * M_W
RMS_EPS = 1e-6
MH_EPS = 1e-6
RWKV_GN_EPS = 64e-5
RWKV_DECAY_SCALE = 0.6065306597126334

ZC_CONV = 0
ZC_GATE = 24
ZC_MV = 48
ZC_MO = 56
ZC_RW = 64
Z_BLOCKS = 90
Z_W = Z_BLOCKS * V7X_LANES
Z_TN = 18 * V7X_LANES

TOKEN_TILE = 1024
MERGE_TILE = 512
FFN_SUB = 3 * V7X_MXU_WIDTH
CONV_TT = 512
CONV_ROWS = 128
MLSTM_CHUNK = 512
RWKV_CHUNK = 64
RWKV_GROUP = 128
RWKV_TT = 256
RWKV_PAIRS_PER_STEP = R_PAIRS

HI = lax.Precision.HIGHEST


def _cparams(sem):
    return pltpu.CompilerParams(dimension_semantics=sem, vmem_limit_bytes=VMEM_LIMIT)


def _rms(x, g):
    return (x * lax.rsqrt(jnp.mean(x * x, axis=-1, keepdims=True) + RMS_EPS)) * g


def _softplus(x):
    return jnp.maximum(x, 0.0) + jnp.log1p(jnp.exp(-jnp.abs(x)))


def _sigmoid(x):
    return jax.nn.sigmoid(x)


def _dot(a, b, prec=None):
    return jnp.dot(a, b, preferred_element_type=F32, precision=prec)


def _dot_nt(a, b, prec=None):
    return lax.dot_general(a, b, (((1,), (1,)), ((), ())), preferred_element_type=F32, precision=prec)


def _dot_tn(a, b, prec=None):
    return lax.dot_general(a, b, (((0,), (0,)), ((), ())), preferred_element_type=F32, precision=prec)


def _cumsum_rows(x):
    n = x.shape[0]
    row = lax.broadcasted_iota(jnp.int32, x.shape, 0)
    d = 1
    while d < n:
        x = x + jnp.where(row >= d, pltpu.roll(x, d, axis=0), 0.0)
        d *= 2
    return x


def _ffn_kernel(x_ref, g_ref, wg_ref, wu_ref, wd_ref, *rest, final_norm):
    if final_norm:
        fg_ref, o_ref = rest
    else:
        (o_ref,) = rest
    x = x_ref[...]
    xn = _rms(x, g_ref[...]).astype(BF16)
    acc = None
    for c0 in range(0, D_FF, FFN_SUB):
        c1 = min(c0 + FFN_SUB, D_FF)
        hg = _dot(xn, wg_ref[:, c0:c1])
        hu = _dot(xn, wu_ref[:, c0:c1])
        h = (hg * _sigmoid(hg)) * hu
        d = _dot(h.astype(BF16), wd_ref[c0:c1, :])
        acc = d if acc is None else acc + d
    y = x + 0.5 * acc
    if final_norm:
        y = _rms(y, fg_ref[...])
    o_ref[...] = y


def _ffn(x, g, wg, wu, wd, layer, final_g=None):
    m = x.shape[0]
    tm = min(TOKEN_TILE, m)
    resident = pl.Buffered(1)
    in_specs = [
        pl.BlockSpec((tm, D_MODEL), lambda i: (i, 0)),
        pl.BlockSpec((1, D_MODEL), lambda i: (0, 0)),
        pl.BlockSpec((None, D_MODEL, D_FF), lambda i: (layer, 0, 0), pipeline_mode=resident),
        pl.BlockSpec((None, D_MODEL, D_FF), lambda i: (layer, 0, 0), pipeline_mode=resident),
        pl.BlockSpec((None, D_FF, D_MODEL), lambda i: (layer, 0, 0), pipeline_mode=resident),
    ]
    args = [x, g.reshape(1, D_MODEL), wg, wu, wd]
    if final_g is not None:
        in_specs.append(pl.BlockSpec((1, D_MODEL), lambda i: (0, 0)))
        args.append(final_g.reshape(1, D_MODEL))
    return pl.pallas_call(
        functools.partial(_ffn_kernel, final_norm=final_g is not None),
        grid=(m // tm,),
        in_specs=in_specs,
        out_specs=pl.BlockSpec((tm, D_MODEL), lambda i: (i, 0)),
        out_shape=jax.ShapeDtypeStruct((m, D_MODEL), F32),
        compiler_params=_cparams(("parallel",)),
        name="ffn",
    )(*args)


def _inproj_kernel(x_ref, g_ref, w_ref, wif_ref, o_ref, oif_ref, xn_ref):
    @pl.when(pl.program_id(1) == 0)
    def _():
        xn = _rms(x_ref[...], g_ref[...]).astype(BF16)
        xn_ref[...] = xn
        oif_ref[...] = _dot(xn, wif_ref[...])

    o_ref[...] = _dot(xn_ref[...], w_ref[...]).astype(BF16)


def _inproj(x, g, w_all, w_if):
    m = x.shape[0]
    tm = min(TOKEN_TILE, m)
    return pl.pallas_call(
        _inproj_kernel,
        grid=(m // tm, Z_W // Z_TN),
        in_specs=[
            pl.BlockSpec((tm, D_MODEL), lambda i, j: (i, 0)),
            pl.BlockSpec((1, D_MODEL), lambda i, j: (0, 0)),
            pl.BlockSpec((D_MODEL, Z_TN), lambda i, j: (0, j)),
            pl.BlockSpec((D_MODEL, V7X_LANES), lambda i, j: (0, 0)),
        ],
        out_specs=[
            pl.BlockSpec((tm, Z_TN), lambda i, j: (i, j)),
            pl.BlockSpec((tm, V7X_LANES), lambda i, j: (i, 0)),
        ],
        out_shape=[
            jax.ShapeDtypeStruct((m, Z_W), BF16),
            jax.ShapeDtypeStruct((m, V7X_LANES), F32),
        ],
        scratch_shapes=[pltpu.VMEM((tm, D_MODEL), BF16)],
        compiler_params=_cparams(("parallel", "arbitrary")),
        name="in_proj",
    )(x, g.reshape(1, D_MODEL), w_all, w_if)


def _conv_lru_kernel(z_ref, cprev_ref, h0_ref, cw_ref, cb_ref, w2_ref, bax_ref, lam_ref,
                     hl_ref, qk_ref, hlast_ref, tail_ref, xp_ref, a_ref, b_ref, hs_ref, hc_ref, *, tt):
    pad = V7X_SUBLANES

    @pl.when(pl.program_id(1) == 0)
    def _():
        xp_ref[0:pad, :] = jnp.zeros((pad, CONV_CH), F32)
        xp_ref[pad - 3:pad, :] = cprev_ref[0]
        hc_ref[...] = h0_ref[0]

    xp_ref[pad:pad + tt, :] = z_ref[0].astype(F32)
    sp = _softplus(-lam_ref[...])
    rb = min(CONV_ROWS, tt)
    for r0 in range(0, tt, rb):
        rows = slice(r0, r0 + rb)
        for cblk in range(CONV_CH // LRU_PAIR):
            cols = slice(cblk * LRU_PAIR, (cblk + 1) * LRU_PAIR)
            xe = xp_ref[r0:pad + r0 + rb, cols]
            c = xe * cw_ref[0:1, cols] + cb_ref[:, cols]
            c = pltpu.roll(c, 1, axis=0) + xe * cw_ref[1:2, cols]
            c = pltpu.roll(c, 1, axis=0) + xe * cw_ref[2:3, cols]
            c = (pltpu.roll(c, 1, axis=0) + xe * cw_ref[3:4, cols])[pad:]
            if cblk < LRU_W // LRU_PAIR:
                cb16 = c.astype(BF16)
                nblk = LRU_PAIR // LRU_BD
                gate = lambda g: jnp.concatenate(
                    [_dot(cb16[:, n * LRU_BD:(n + 1) * LRU_BD], w2_ref[g, nblk * cblk + n]) for n in range(nblk)],
                    axis=1)
                r = _sigmoid(gate(0) + bax_ref[0:1, cols])
                i = _sigmoid(gate(1) + bax_ref[1:2, cols])
                log_a = (-LRU_C * r) * sp[:, cols]
                a = jnp.exp(log_a)
                mult = jnp.sqrt(-jnp.tanh(log_a) * (a * a + 1.0))
                a_ref[rows, cols] = a
                b_ref[rows, cols] = mult * (i * c)
            elif cblk < 2 * LRU_W // LRU_PAIR:
                qk_ref[0, rows, cblk * LRU_PAIR - LRU_W:(cblk + 1) * LRU_PAIR - LRU_W] = (
                    c * _sigmoid(c)).astype(BF16)
            else:
                qk_ref[0, rows, cblk * LRU_PAIR - LRU_W:(cblk + 1) * LRU_PAIR - LRU_W] = (
                    (c * _sigmoid(c)) * (M_HD ** -0.5)).astype(BF16)
    tail = xp_ref[pad + tt - 3:pad + tt, :]
    xp_ref[pad - 3:pad, :] = tail
    tail_ref[0] = tail

    def body(t, h):
        h = a_ref[pl.ds(t, 1), :] * h + b_ref[pl.ds(t, 1), :]
        hs_ref[pl.ds(t, 1), :] = h
        return h

    h = lax.fori_loop(0, tt, body, hc_ref[...], unroll=8)
    hc_ref[...] = h
    hlast_ref[0] = h
    hl_ref[0] = hs_ref[...].astype(BF16)


def _conv_lru(z, conv_prev, h0, cw, cb, w2, bax, lam):
    b, t, _ = z.shape
    tt = min(CONV_TT, t)
    return pl.pallas_call(
        functools.partial(_conv_lru_kernel, tt=tt),
        grid=(b, t // tt),
        in_specs=[
            pl.BlockSpec((1, tt, CONV_CH), lambda i, j: (i, j, ZC_CONV * V7X_LANES // CONV_CH)),
            pl.BlockSpec((1, CONV_W - 1, CONV_CH), lambda i, j: (i, 0, 0)),
            pl.BlockSpec((1, 1, LRU_W), lambda i, j: (i, 0, 0)),
            pl.BlockSpec((CONV_W, CONV_CH), lambda i, j: (0, 0)),
            pl.BlockSpec((1, CONV_CH), lambda i, j: (0, 0)),
            pl.BlockSpec((2, LRU_BLOCKS, LRU_BD, LRU_BD), lambda i, j: (0, 0, 0, 0)),
            pl.BlockSpec((2, LRU_W), lambda i, j: (0, 0)),
            pl.BlockSpec((1, LRU_W), lambda i, j: (0, 0)),
        ],
        out_specs=[
            pl.BlockSpec((1, tt, LRU_W), lambda i, j: (i, j, 0)),
            pl.BlockSpec((1, tt, 2 * M_W), lambda i, j: (i, j, 0)),
            pl.BlockSpec((1, 1, LRU_W), lambda i, j: (i, 0, 0)),
            pl.BlockSpec((1, CONV_W - 1, CONV_CH), lambda i, j: (i, 0, 0)),
        ],
        out_shape=[
            jax.ShapeDtypeStruct((b, t, LRU_W), BF16),
            jax.ShapeDtypeStruct((b, t, 2 * M_W), BF16),
            jax.ShapeDtypeStruct((b, 1, LRU_W), F32),
            jax.ShapeDtypeStruct((b, CONV_W - 1, CONV_CH), F32),
        ],
        scratch_shapes=[
            pltpu.VMEM((tt + V7X_SUBLANES, CONV_CH), F32),
            pltpu.VMEM((tt, LRU_W), F32),
            pltpu.VMEM((tt, LRU_W), F32),
            pltpu.VMEM((tt, LRU_W), F32),
            pltpu.VMEM((1, LRU_W), F32),
        ],
        compiler_params=_cparams(("parallel", "arbitrary")),
        name="conv_lru",
    )(z, conv_prev, h0.reshape(b, 1, LRU_W), cw, cb.reshape(1, CONV_CH), w2, bax, lam.reshape(1, LRU_W))


def _mlstm_kernel(qk_ref, v_ref, o_ref, if_ref, c0_ref, n0_ref, m0_ref, ifb_ref, nw_ref,
                  om_ref, c_ref, n_ref, m_ref, *, lc):
    @pl.when(pl.program_id(1) == 0)
    def _():
        c_ref[...] = c0_ref[...]
        n_ref[...] = n0_ref[...]
        m_ref[...] = m0_ref[...]

    gif = if_ref[0] + ifb_ref[...]
    lf = jnp.minimum(gif, 0.0) - jnp.log1p(jnp.exp(-jnp.abs(gif)))
    cum = _cumsum_rows(lf)
    src = gif - pltpu.roll(cum, V7X_LANES - M_HEADS, axis=1)
    src_t = src.T if lc % V7X_LANES == 0 else None
    lane = lax.broadcasted_iota(jnp.int32, (lc, V7X_LANES), 1)
    tpos = lax.broadcasted_iota(jnp.int32, (lc, lc), 0)
    spos = lax.broadcasted_iota(jnp.int32, (lc, lc), 1)
    causal = spos <= tpos
    m_prev_all = m_ref[0]
    mlane = lax.broadcasted_iota(jnp.int32, (1, M_HEADS), 1)
    m_new_all = m_prev_all

    heads = range(M_HEADS)
    sl = lambda h: slice(h * M_HD, (h + 1) * M_HD)
    q = {h: qk_ref[0, :, sl(h)] for h in heads}
    k = {h: qk_ref[0, :, M_HEADS * M_HD + h * M_HD:M_HEADS * M_HD + (h + 1) * M_HD] for h in heads}
    vb = {h: v_ref[0, :, sl(h)] for h in heads}
    bc = {h: cum[:, M_HEADS + h:M_HEADS + h + 1] for h in heads}
    ig = {h: gif[:, h:h + 1] for h in heads}
    m_prev = {h: m_prev_all[:, h:h + 1] for h in heads}
    c_prev = {h: c_ref[0, h] for h in heads}
    n_prev = {h: n_ref[0, h:h + 1, :] for h in heads}
    if src_t is not None:
        rowv = {h: src_t[h:h + 1, :] for h in heads}
    else:
        rowv = {h: _dot_nt(jnp.where(lane == h, 1.0, 0.0), src, HI) for h in heads}
    dmat = {h: jnp.where(causal, bc[h] + rowv[h], -jnp.inf) for h in heads}
    inter = {h: bc[h] + m_prev[h] for h in heads}
    m_t = {h: jnp.maximum(inter[h], jnp.max(dmat[h], axis=1, keepdims=True)) for h in heads}
    qk = {h: _dot_nt(q[h], k[h]) for h in heads}
    qc = {h: _dot_nt(q[h], c_prev[h].astype(BF16)) for h in heads}
    nrows = {h: jnp.broadcast_to(n_prev[h], (V7X_SUBLANES, M_HD)).astype(BF16) for h in heads}
    qn = {h: _dot_nt(q[h], nrows[h])[:, 0:1] for h in heads}
    w_inter = {h: jnp.exp(inter[h] - m_t[h]) for h in heads}
    s = {h: qk[h] * jnp.exp(dmat[h] - m_t[h]) for h in heads}
    num = {h: _dot(s[h].astype(BF16), vb[h]) + w_inter[h] * qc[h] for h in heads}
    den = {h: jnp.sum(s[h], axis=1, keepdims=True) + w_inter[h] * qn[h] for h in heads}
    hh = {h: num[h] / jnp.maximum(jnp.abs(den[h]), jnp.exp(-m_t[h])) for h in heads}
    mean = {h: jnp.mean(hh[h], axis=1, keepdims=True) for h in heads}
    d = {h: hh[h] - mean[h] for h in heads}
    var = {h: jnp.mean(d[h] * d[h], axis=1, keepdims=True) for h in heads}
    for h in heads:
        hn = d[h] * lax.rsqrt(var[h] + MH_EPS)
        om_ref[0, :, sl(h)] = (_sigmoid(o_ref[0, :, sl(h)].astype(F32)) * (hn * nw_ref[:, sl(h)])).astype(BF16)
    m_new = {h: m_t[h][lc - 1:lc, :] for h in heads}
    bl = {h: bc[h][lc - 1:lc, :] for h in heads}
    g_state = {h: jnp.exp(bl[h] + m_prev[h] - m_new[h]) for h in heads}
    g_src = {h: jnp.exp(bl[h] - bc[h] + ig[h] - m_new[h]) for h in heads}
    for h in heads:
        c_ref[0, h] = g_state[h] * c_prev[h] + _dot_tn((g_src[h] * vb[h].astype(F32)).astype(BF16), k[h])
        n_ref[0, h:h + 1, :] = g_state[h] * n_prev[h] + jnp.sum(g_src[h] * k[h].astype(F32), axis=0, keepdims=True)
        m_new_all = jnp.where(mlane == h, m_new[h], m_new_all)
    m_ref[0] = m_new_all


def _mlstm(qk, z, zif, c0, n0, m0, if_bias, norm_w):
    b, t, _ = z.shape
    lc = min(MLSTM_CHUNK, t)
    w = M_W
    return pl.pallas_call(
        functools.partial(_mlstm_kernel, lc=lc),
        grid=(b, t // lc),
        in_specs=[
            pl.BlockSpec((1, lc, 2 * w), lambda i, j: (i, j, 0)),
            pl.BlockSpec((1, lc, w), lambda i, j: (i, j, ZC_MV * V7X_LANES // w)),
            pl.BlockSpec((1, lc, w), lambda i, j: (i, j, ZC_MO * V7X_LANES // w)),
            pl.BlockSpec((1, lc, V7X_LANES), lambda i, j: (i, j, 0)),
            pl.BlockSpec((1, M_HEADS, M_HD, M_HD), lambda i, j: (i, 0, 0, 0)),
            pl.BlockSpec((1, M_HEADS, M_HD), lambda i, j: (i, 0, 0)),
            pl.BlockSpec((1, 1, M_HEADS), lambda i, j: (i, 0, 0)),
            pl.BlockSpec((1, V7X_LANES), lambda i, j: (0, 0)),
            pl.BlockSpec((1, w), lambda i, j: (0, 0)),
        ],
        out_specs=[
            pl.BlockSpec((1, lc, w), lambda i, j: (i, j, 0)),
            pl.BlockSpec((1, M_HEADS, M_HD, M_HD), lambda i, j: (i, 0, 0, 0)),
            pl.BlockSpec((1, M_HEADS, M_HD), lambda i, j: (i, 0, 0)),
            pl.BlockSpec((1, 1, M_HEADS), lambda i, j: (i, 0, 0)),
        ],
        out_shape=[
            jax.ShapeDtypeStruct((b, t, w), BF16),
            jax.ShapeDtypeStruct((b, M_HEADS, M_HD, M_HD), F32),
            jax.ShapeDtypeStruct((b, M_HEADS, M_HD), F32),
            jax.ShapeDtypeStruct((b, 1, M_HEADS), F32),
        ],
        compiler_params=_cparams(("parallel", "arbitrary")),
        name="mlstm",
    )(qk, z, z, zif, c0, n0, m0.reshape(b, 1, M_HEADS), if_bias, norm_w.reshape(1, w))


def _rwkv_mix_kernel(zr_ref, zk_ref, zv_ref, zwa_ref, zgd_ref, sh_ref, mu_ref, w0_ref, w2_ref, a0_ref, a2_ref, g2_ref,
                     kk_ref, ka_ref, s0_ref, lnw_ref, lnb_ref, rk_ref, e2_ref,
                     o_ref, s_ref, last_ref, y_ref, car_ref, *, npair, tt, lc, rg):
    @pl.when(pl.program_id(2) == 0)
    def _():
        s_ref[...] = s0_ref[...]
        car_ref[...] = sh_ref[0]

    def seg(x):
        return _dot(x.astype(BF16), e2_ref[...])

    first_row = lax.broadcasted_iota(jnp.int32, (tt, 1), 0) == 0
    pairs = range(npair)
    ps = lambda p: slice(p * V7X_LANES, (p + 1) * V7X_LANES)
    hs = lambda h: slice(h * R_HD, (h + 1) * R_HD)

    def shifted(zp, p, i):
        zp = zp.astype(F32)
        prev = jnp.where(first_row, car_ref[p, i:i + 1, :], pltpu.roll(zp, 1, axis=0))
        car_ref[p, i:i + 1, :] = zp[tt - 1:tt, :]
        return zp + (prev - zp) * mu_ref[p, i:i + 1, :]

    xwa = shifted(zwa_ref[0], 0, 3)
    xgd = shifted(zgd_ref[0], 0, 4)
    twa_b = jnp.tanh(xwa).astype(BF16)
    xwa_b = xwa.astype(BF16)
    sgd_b = _sigmoid(xgd).astype(BF16)
    r_all, k_all, v_all, lw_all, a_all, b_all, g_all = {}, {}, {}, {}, {}, {}, {}

    def prologue(p):
        r_all[p] = shifted(zr_ref[0, :, ps(p)], p, 0)
        k_in = shifted(zk_ref[0, :, ps(p)], p, 1)
        v_all[p] = shifted(zv_ref[0, :, ps(p)], p, 2)
        lw_all[p] = -RWKV_DECAY_SCALE * _sigmoid(w0_ref[p] + _dot(twa_b, w2_ref[:, ps(p)]))
        a_gate = _sigmoid(a0_ref[p] + _dot(xwa_b, a2_ref[:, ps(p)]))
        g_all[p] = _dot(sgd_b, g2_ref[:, ps(p)])
        kk = k_in * kk_ref[p]
        kk = kk * lax.rsqrt(jnp.maximum(seg(kk * kk), 1e-24))
        k_all[p] = k_in * (1.0 + (a_gate - 1.0) * ka_ref[p])
        a_all[p] = -kk
        b_all[p] = kk * a_gate

    nb = rg // lc
    urow = lax.broadcasted_iota(jnp.int32, (2 * rg, 2 * rg), 0)
    scol = lax.broadcasted_iota(jnp.int32, (2 * rg, 2 * rg), 1)
    incl = urow >= rg
    trow = jnp.where(incl, urow - rg, urow)
    scol = jnp.where(scol >= rg, scol - rg, scol)
    dist = trow - scol
    gmask = (dist >= jnp.where(incl, 0, 1)) & (dist <= (trow & (lc - 1)))
    lsh = lc.bit_length() - 1
    qrow = lax.broadcasted_iota(jnp.int32, (rg, rg), 0)
    qcol = lax.broadcasted_iota(jnp.int32, (rg, rg), 1)
    same = (qrow >> lsh) == (qcol >> lsh)
    blk = jnp.where(same, 1.0, 0.0).astype(BF16)
    tril = jnp.where(qcol <= qrow, blk, jnp.zeros((), BF16))
    xrow = lax.broadcasted_iota(jnp.int32, (rg, nb * R_HD), 0)
    xcol = lax.broadcasted_iota(jnp.int32, (rg, nb * R_HD), 1)
    xmask = (xrow >> (lc.bit_length() - 1)) == (xcol >> (R_HD.bit_length() - 1))
    n_sq = max(lc.bit_length() - 1, 0)

    groups = range(tt // rg)
    ew = {}

    def prepare(p):
        for g in groups:
            rows = slice(g * rg, (g + 1) * rg)
            lw = lw_all[p][rows]
            k = k_all[p][rows]
            b = b_all[p][rows]
            lw_hi = lw.astype(BF16)
            lw_lo = (lw - lw_hi.astype(F32)).astype(BF16)
            cum = _dot(tril, lw_hi) + _dot(tril, lw_lo)
            ends = [cum[c * lc + lc - 1:c * lc + lc, :] for c in range(nb)]
            cum_end = ends[0] if nb == 1 else jnp.concatenate(
                [jnp.broadcast_to(e, (lc, V7X_LANES)) for e in ends], axis=0)
            e_neg = jnp.exp(-cum)
            at = a_all[p][rows] * jnp.exp(cum - lw)
            rt = r_all[p][rows] * jnp.exp(cum)
            e_end = jnp.exp(cum_end - cum)
            ew[p, g] = dict(at=at, rt=rt, lhs=jnp.concatenate([at, rt], axis=0).astype(BF16),
                            rhs=jnp.concatenate([b * e_neg, k * e_neg], axis=0).astype(BF16),
                            be=(b * e_end).astype(BF16), ke=(k * e_end).astype(BF16),
                            vb=v_all[p][rows].astype(BF16),
                            w_end=[jnp.exp(e) for e in ends])

    def expand(x):
        if nb == 1:
            return x
        return jnp.where(xmask, jnp.concatenate([x] * nb, axis=1), jnp.zeros((), x.dtype))

    piece = lambda c, name: ew[c[0], c[1]][name][:, hs(c[2])]
    pq2, xe, vk = {}, {}, {}

    def solve(chains):
        gm = {c: jnp.where(gmask, _dot_nt(piece(c, "lhs"), piece(c, "rhs")), 0.0) for c in chains}
        gv = {c: _dot(gm[c][:, rg:].astype(BF16), piece(c, "vb")) for c in chains}
        gr = {c: gm[c][rg:] for c in chains}
        gvr = {c: gv[c][rg:] for c in chains}
        x = {c: jnp.concatenate([piece(c, "at"), gv[c][:rg]], axis=1) for c in chains}
        pw = {c: gm[c][:rg, :rg] for c in chains}
        for i in range(n_sq):
            pwb = {c: pw[c].astype(BF16) for c in chains}
            x = {c: x[c] + _dot(pwb[c], x[c].astype(BF16)) for c in chains}
            if i + 1 < n_sq:
                pw = {c: _dot(pwb[c], pwb[c]) for c in chains}
        xb = {c: x[c].astype(BF16) for c in chains}
        for c in chains:
            pq2[c] = jnp.concatenate([piece(c, "rt"), gvr[c]], axis=1) + _dot(gr[c][:, :rg].astype(BF16), xb[c])
            xe[c] = _dot_tn(xb[c], expand(piece(c, "be")))
            vk[c] = _dot_tn(piece(c, "vb"), expand(piece(c, "ke")))

    for p in pairs:
        prologue(p)
    for p in pairs:
        prepare(p)
    solve([(p, g, h) for p in pairs for g in groups for h in range(2)])

    bonus = {p: seg((r_all[p] * k_all[p]) * rk_ref[p]) * v_all[p] for p in pairs}

    s = {(p, h): s_ref[0, 2 * p + h] for p in pairs for h in range(2)}
    for g in groups:
        for c in range(nb):
            rows = slice(c * lc, (c + 1) * lc)
            cols = slice(c * R_HD, (c + 1) * R_HD)
            sb = {ph: s[ph].astype(BF16) for ph in s}
            for p in pairs:
                ys = [_dot_nt(pq2[p, g, h][rows, :R_HD].astype(BF16), sb[p, h]) + pq2[p, g, h][rows, R_HD:]
                      for h in range(2)]
                y_ref[g * rg + c * lc:g * rg + (c + 1) * lc, ps(p)] = jnp.concatenate(ys, axis=1)
            s = {(p, h): (s[p, h] * ew[p, g]["w_end"][c][:, hs(h)]
                          + _dot(sb[p, h], xe[p, g, h][:R_HD, cols].astype(BF16))
                          + (xe[p, g, h][R_HD:, cols] + vk[p, g, h][:, cols])) for (p, h) in s}
    for (p, h) in s:
        s_ref[0, 2 * p + h] = s[p, h]

    y = {p: y_ref[:, ps(p)] for p in pairs}
    d = {p: y[p] - seg(y[p]) * (1.0 / R_HD) for p in pairs}
    var = {p: seg(d[p] * d[p]) * (1.0 / R_HD) for p in pairs}
    for p in pairs:
        yn = (d[p] * lax.rsqrt(var[p] + RWKV_GN_EPS)) * lnw_ref[p] + lnb_ref[p]
        o_ref[0, :, ps(p)] = ((yn + bonus[p]) * g_all[p]).astype(BF16)
    last_ref[0] = car_ref[...]


def _rwkv_mix(z, shift_prev, s0, mu, w0, w2p, a0, a2p, g2, k_k, k_a, ln_w, ln_b, r_k, e2):
    bsz, t, _ = z.shape
    tt = min(RWKV_TT, t)
    lc = min(RWKV_CHUNK, t)
    rg = min(RWKV_GROUP, t)
    npair = RWKV_PAIRS_PER_STEP
    wide = npair * V7X_LANES
    zpair = lambda blk: pl.BlockSpec((1, tt, wide), lambda i, p, j: (i, j, blk // npair + p))
    zshared = lambda blk: pl.BlockSpec((1, tt, V7X_LANES), lambda i, p, j: (i, j, blk))
    par_spec = pl.BlockSpec((npair, 1, V7X_LANES), lambda i, p, j: (p, 0, 0))
    lora_spec = pl.BlockSpec((V7X_LANES, wide), lambda i, p, j: (0, p))
    st_spec = pl.BlockSpec((1, 2 * npair, R_HD, R_HD), lambda i, p, j: (i, p, 0, 0))
    pair = lambda x: x.reshape(R_PAIRS, 1, V7X_LANES)

    def pieces(x):
        lead = x.shape[:-1]
        rkv = x[..., :3 * R_W].reshape(lead + (3, R_PAIRS, V7X_LANES))
        rkv = jnp.moveaxis(rkv, -3, -2)
        lora = jnp.broadcast_to(x[..., 3 * R_W:].reshape(lead + (1, 2, V7X_LANES)), lead + (R_PAIRS, 2, V7X_LANES))
        return jnp.concatenate([rkv, lora], axis=-2)

    out, s_new, last = pl.pallas_call(
        functools.partial(_rwkv_mix_kernel, npair=npair, tt=tt, lc=lc, rg=rg),
        grid=(bsz, R_PAIRS // npair, t // tt),
        in_specs=[
            zpair(ZC_RW), zpair(ZC_RW + 8), zpair(ZC_RW + 16), zshared(ZC_RW + 24), zshared(ZC_RW + 25),
            pl.BlockSpec((1, npair, 5, V7X_LANES), lambda i, p, j: (i, p, 0, 0)),
            pl.BlockSpec((npair, 5, V7X_LANES), lambda i, p, j: (p, 0, 0)),
            par_spec, lora_spec, par_spec, lora_spec, lora_spec, par_spec, par_spec,
            st_spec, par_spec, par_spec, par_spec,
            pl.BlockSpec((V7X_LANES, V7X_LANES), lambda i, p, j: (0, 0)),
        ],
        out_specs=[
            pl.BlockSpec((1, tt, wide), lambda i, p, j: (i, j, p)),
            st_spec,
            pl.BlockSpec((1, npair, 5, V7X_LANES), lambda i, p, j: (i, p, 0, 0)),
        ],
        out_shape=[
            jax.ShapeDtypeStruct((bsz, t, R_W), BF16),
            jax.ShapeDtypeStruct((bsz, R_HEADS, R_HD, R_HD), F32),
            jax.ShapeDtypeStruct((bsz, R_PAIRS, 5, V7X_LANES), F32),
        ],
        scratch_shapes=[pltpu.VMEM((tt, wide), F32), pltpu.VMEM((npair, 5, V7X_LANES), F32)],
        compiler_params=_cparams(("parallel", "parallel", "arbitrary")),
        name="rwkv_mix",
    )(z, z, z, z, z, pieces(shift_prev.reshape(bsz, R_IN)), pieces(mu), pair(w0), w2p, pair(a0), a2p, g2,
      pair(k_k), pair(k_a), s0, pair(ln_w), pair(ln_b), pair(r_k), e2)
    rkv = jnp.moveaxis(last[:, :, :3, :], 2, 1).reshape(bsz, 3 * R_W)
    shift_new = jnp.concatenate([rkv, last[:, 0, 3:, :].reshape(bsz, 2 * V7X_LANES)], axis=-1)
    return out, s_new, shift_new.reshape(bsz, 1, R_IN)


def _merge_kernel(x_ref, zg_ref, hl_ref, om_ref, or_ref, wb_ref, wo_ref, o_ref):
    acc = None
    for gidx, br in enumerate((hl_ref, om_ref, or_ref)):
        sl = slice(gidx * D_MODEL, (gidx + 1) * D_MODEL)
        term = _sigmoid(zg_ref[:, sl].astype(F32)) * _dot(br[...], wb_ref[gidx])
        acc = term if acc is None else acc + term
    o_ref[...] = x_ref[...] + _dot(acc.astype(BF16), wo_ref[...])


def _merge(x, z2d, hl, om, orw, wb, wo):
    m = x.shape[0]
    tm = min(MERGE_TILE, m)
    tok = pl.BlockSpec((tm, D_MODEL), lambda i: (i, 0))
    return pl.pallas_call(
        _merge_kernel,
        grid=(m // tm,),
        in_specs=[
            tok,
            pl.BlockSpec((tm, 3 * D_MODEL), lambda i: (i, ZC_GATE * V7X_LANES // (3 * D_MODEL))),
            tok, tok, tok,
            pl.BlockSpec((3, D_MODEL, D_MODEL), lambda i: (0, 0, 0)),
            pl.BlockSpec((D_MODEL, D_MODEL), lambda i: (0, 0)),
        ],
        out_specs=tok,
        out_shape=jax.ShapeDtypeStruct((m, D_MODEL), F32),
        compiler_params=_cparams(("parallel",)),
        name="merge",
    )(x, z2d, hl, om, orw, wb, wo)


def _prep_layer_weights(p, l):
    w_in = p["w_in"][l]
    c0 = CONV_CH
    c1 = c0 + M_W
    c2 = c1 + M_W
    c3 = c2 + 2 * M_HEADS
    c4 = c3 + 3 * R_W + 2 * R_LORA + V7X_LANES
    w_if = jnp.pad(w_in[:, c2:c3], ((0, 0), (0, V7X_LANES - 2 * M_HEADS))).astype(BF16)
    w_all = jnp.concatenate([w_in[:, :c0], w_in[:, c4:], w_in[:, c0:c1], w_in[:, c1:c2], w_in[:, c3:c4]],
                            axis=1).astype(BF16)
    w2 = jnp.stack([p["lru_wa"][l], p["lru_wx"][l]]).astype(BF16)
    bax = jnp.stack([p["lru_ba"][l], p["lru_bx"][l]])
    if_bias = jnp.pad(p["mlstm_if_bias"][l], (0, V7X_LANES - 2 * M_HEADS)).reshape(1, V7X_LANES)
    zpad = jnp.zeros((R_LORA, R_W), F32)
    w2p = jnp.concatenate([p["rwkv_w2"][l], zpad], axis=0).astype(BF16)
    a2p = jnp.concatenate([zpad, p["rwkv_a2"][l]], axis=0).astype(BF16)
    return dict(
        w_all=w_all, w_if=w_if, w2=w2, bax=bax, if_bias=if_bias, w2p=w2p, a2p=a2p,
        g2=p["rwkv_g2"][l].astype(BF16),
        wb=p["w_branch"][l].astype(BF16), wo=p["w_out"][l].astype(BF16),
    )


def _segment_matrix():
    half = jnp.arange(V7X_LANES) // R_HD
    return (half[:, None] == half[None, :]).astype(BF16)


def _run_group(x, states, p, lw_list, ffn_w, final_norm):
    bsz, t, d_model = x.shape
    m = bsz * t
    assert d_model == D_MODEL and t >= CONV_W - 1
    assert all(t % min(tile, t) == 0 for tile in (CONV_TT, MLSTM_CHUNK, RWKV_TT, RWKV_GROUP, RWKV_CHUNK))
    assert all(m % min(tile, m) == 0 for tile in (TOKEN_TILE, MERGE_TILE))
    e2 = _segment_matrix()
    xf = x.reshape(m, D_MODEL)
    new_states = []
    depth = p["w_in"].shape[0]
    for l in range(depth):
        lw = lw_list[l]
        conv_prev, lru_h, m_c, m_n, m_m, shift_prev, rwkv_s = states[l]
        xf = _ffn(xf, p["ffn1_norm"][l], *ffn_w[0], layer=l)
        z2d, zif = _inproj(xf, p["mix_norm"][l], lw["w_all"], lw["w_if"])
        z = z2d.reshape(bsz, t, Z_W)
        hl, qk, h_last, conv_new = _conv_lru(z, conv_prev, lru_h, p["conv_w"][l], p["conv_b"][l], lw["w2"], lw["bax"],
                                   p["lru_lambda"][l])
        om, c_new, n_new, m_new = _mlstm(qk, z, zif.reshape(bsz, t, V7X_LANES), m_c, m_n, m_m, lw["if_bias"],
                                         p["mlstm_norm"][l])
        orw, s_new, shift_new = _rwkv_mix(z, shift_prev, rwkv_s, p["rwkv_mu"][l], p["rwkv_w0"][l], lw["w2p"], p["rwkv_a0"][l],
                               lw["a2p"], lw["g2"], p["rwkv_k_k"][l], p["rwkv_k_a"][l], p["rwkv_ln_w"][l],
                               p["rwkv_ln_b"][l], p["rwkv_r_k"][l].reshape(-1), e2)
        xf = _merge(xf, z2d, hl.reshape(m, D_MODEL), om.reshape(m, D_MODEL), orw.reshape(m, D_MODEL),
                    lw["wb"], lw["wo"])
        xf = _ffn(xf, p["ffn2_norm"][l], *ffn_w[1], layer=l, final_g=final_norm if l == depth - 1 else None)
        new_states.append((conv_new, h_last.reshape(bsz, LRU_W), c_new, n_new, m_new.reshape(bsz, M_HEADS),
                           shift_new, s_new))
    stacked = tuple(jnp.stack([st[i] for st in new_states]) for i in range(7))
    return xf.reshape(bsz, t, D_MODEL), stacked


def _zero_states(bsz):
    return (jnp.zeros((bsz, CONV_W - 1, CONV_CH), F32), jnp.zeros((bsz, LRU_W), F32),
            jnp.zeros((bsz, M_HEADS, M_HD, M_HD), F32), jnp.zeros((bsz, M_HEADS, M_HD), F32),
            jnp.zeros((bsz, M_HEADS), F32), jnp.zeros((bsz, 1, R_IN), F32),
            jnp.zeros((bsz, R_HEADS, R_HD, R_HD), F32))


def kernel(x_prompt, x_sample, state_conv, state_lru_h, state_mlstm_C, state_mlstm_n, state_mlstm_m, state_rwkv_shift, state_rwkv_S, ffn1_norm, ffn1_w_gate, ffn1_w_up, ffn1_w_down, mix_norm, w_in, conv_w, conv_b, lru_wa, lru_ba, lru_wx, lru_bx, lru_lambda, mlstm_if_bias, mlstm_norm, rwkv_mu, rwkv_w0, rwkv_w2, rwkv_a0, rwkv_a2, rwkv_g2, rwkv_k_k, rwkv_k_a, rwkv_r_k, rwkv_ln_w, rwkv_ln_b, w_branch, w_out, ffn2_norm, ffn2_w_gate, ffn2_w_up, ffn2_w_down, final_norm):
    p = dict(ffn1_norm=ffn1_norm, ffn1_w_gate=ffn1_w_gate, ffn1_w_up=ffn1_w_up, ffn1_w_down=ffn1_w_down,
             mix_norm=mix_norm, w_in=w_in, conv_w=conv_w, conv_b=conv_b,
             lru_wa=lru_wa, lru_ba=lru_ba, lru_wx=lru_wx, lru_bx=lru_bx, lru_lambda=lru_lambda,
             mlstm_if_bias=mlstm_if_bias, mlstm_norm=mlstm_norm,
             rwkv_mu=rwkv_mu, rwkv_w0=rwkv_w0, rwkv_w2=rwkv_w2, rwkv_a0=rwkv_a0, rwkv_a2=rwkv_a2,
             rwkv_g2=rwkv_g2, rwkv_k_k=rwkv_k_k, rwkv_k_a=rwkv_k_a, rwkv_r_k=rwkv_r_k,
             rwkv_ln_w=rwkv_ln_w, rwkv_ln_b=rwkv_ln_b, w_branch=w_branch, w_out=w_out,
             ffn2_norm=ffn2_norm, ffn2_w_gate=ffn2_w_gate, ffn2_w_up=ffn2_w_up, ffn2_w_down=ffn2_w_down)
    depth = w_in.shape[0]
    lw_list = [_prep_layer_weights(p, l) for l in range(depth)]
    ffn_w = [tuple(p[f"ffn{n}_w_{part}"].astype(BF16) for part in ("gate", "up", "down")) for n in (1, 2)]
    states_p = [_zero_states(x_prompt.shape[0]) for _ in range(depth)]
    y_p, st_p = _run_group(x_prompt, states_p, p, lw_list, ffn_w, final_norm)
    states_s = [(state_conv[l], state_lru_h[l], state_mlstm_C[l], state_mlstm_n[l], state_mlstm_m[l],
                 state_rwkv_shift[l], state_rwkv_S[l]) for l in range(depth)]
    y_s, st_s = _run_group(x_sample, states_s, p, lw_list, ffn_w, final_norm)
    return (y_p, y_s) + st_p + st_s
```

```python
import functools

import jax
import jax.numpy as jnp
from jax import lax
from jax.experimental import pallas as pl
from jax.experimental.pallas import tpu as pltpu

F32 = jnp.float32
BF16 = jnp.bfloat16

V7X_LANES = 128
V7X_SUBLANES = 8
V7X_MXU_WIDTH = 256
V7X_VMEM_BYTES = 64 * 1024 * 1024
VMEM_LIMIT = V7X_VMEM_BYTES * 7 // 8

D_MODEL = 1024
D_FF = 2816
CONV_W = 4
LRU_W = D_MODEL
LRU_BLOCKS = 8
LRU_BD = LRU_W // LRU_BLOCKS
LRU_PAIR = 2 * LRU_BD
LRU_C = 8.0
M_HEADS = 4
M_W = D_MODEL
M_HD = M_W // M_HEADS
R_HD = 64
R_W = D_MODEL
R_HEADS = R_W // R_HD
R_PAIRS = R_HEADS // 2
R_LORA = 64
R_IN = 3 * R_W + 2 * V7X_LANES
CONV_CH = LRU_W + 2 * M_W
RMS_EPS = 1e-6
MH_EPS = 1e-6
RWKV_GN_EPS = 64e-5
RWKV_DECAY_SCALE = 0.6065306597126334

ZC_CONV = 0
ZC_GATE = 24
ZC_MV = 48
ZC_MO = 56
ZC_RW = 64
Z_BLOCKS = 90
Z_W = Z_BLOCKS * V7X_LANES
Z_TN = 18 * V7X_LANES

TOKEN_TILE = 1024
MERGE_TILE = 512
FFN_SUB = 3 * V7X_MXU_WIDTH
CONV_TT = 512
CONV_ROWS = 128
MLSTM_CHUNK = 512
RWKV_CHUNK = 64
RWKV_GROUP = 128
RWKV_TT = 256
RWKV_PAIRS_PER_STEP = R_PAIRS

HI = lax.Precision.HIGHEST


def _cparams(sem):
    return pltpu.CompilerParams(dimension_semantics=sem, vmem_limit_bytes=VMEM_LIMIT)


def _rms(x, g):
    return (x * lax.rsqrt(jnp.mean(x * x, axis=-1, keepdims=True) + RMS_EPS)) * g


def _softplus(x):
    return jnp.maximum(x, 0.0) + jnp.log1p(jnp.exp(-jnp.abs(x)))


def _sigmoid(x):
    return jax.nn.sigmoid(x)


def _dot(a, b, prec=None):
    return jnp.dot(a, b, preferred_element_type=F32, precision=prec)


def _dot_nt(a, b, prec=None):
    return lax.dot_general(a, b, (((1,), (1,)), ((), ())), preferred_element_type=F32, precision=prec)


def _dot_tn(a, b, prec=None):
    return lax.dot_general(a, b, (((0,), (0,)), ((), ())), preferred_element_type=F32, precision=prec)


def _cumsum_rows(x):
    n = x.shape[0]
    row = lax.broadcasted_iota(jnp.int32, x.shape, 0)
    d = 1
    while d < n:
        x = x + jnp.where(row >= d, pltpu.roll(x, d, axis=0), 0.0)
        d *= 2
    return x


def _ffn_kernel(x_ref, g_ref, wg_ref, wu_ref, wd_ref, *rest, final_norm):
    if final_norm:
        fg_ref, o_ref = rest
    else:
        (o_ref,) = rest
    x = x_ref[...]
    xn = _rms(x, g_ref[...]).astype(BF16)
    acc = None
    for c0 in range(0, D_FF, FFN_SUB):
        c1 = min(c0 + FFN_SUB, D_FF)
        hg = _dot(xn, wg_ref[:, c0:c1])
        hu = _dot(xn, wu_ref[:, c0:c1])
        h = (hg * _sigmoid(hg)) * hu
        d = _dot(h.astype(BF16), wd_ref[c0:c1, :])
        acc = d if acc is None else acc + d
    y = x + 0.5 * acc
    if final_norm:
        y = _rms(y, fg_ref[...])
    o_ref[...] = y


def _ffn(x, g, wg, wu, wd, layer, final_g=None):
    m = x.shape[0]
    tm = min(TOKEN_TILE, m)
    resident = pl.Buffered(1)
    in_specs = [
        pl.BlockSpec((tm, D_MODEL), lambda i: (i, 0)),
        pl.BlockSpec((1, D_MODEL), lambda i: (0, 0)),
        pl.BlockSpec((None, D_MODEL, D_FF), lambda i: (layer, 0, 0), pipeline_mode=resident),
        pl.BlockSpec((None, D_MODEL, D_FF), lambda i: (layer, 0, 0), pipeline_mode=resident),
        pl.BlockSpec((None, D_FF, D_MODEL), lambda i: (layer, 0, 0), pipeline_mode=resident),
    ]
    args = [x, g.reshape(1, D_MODEL), wg, wu, wd]
    if final_g is not None:
        in_specs.append(pl.BlockSpec((1, D_MODEL), lambda i: (0, 0)))
        args.append(final_g.reshape(1, D_MODEL))
    return pl.pallas_call(
        functools.partial(_ffn_kernel, final_norm=final_g is not None),
        grid=(m // tm,),
        in_specs=in_specs,
        out_specs=pl.BlockSpec((tm, D_MODEL), lambda i: (i, 0)),
        out_shape=jax.ShapeDtypeStruct((m, D_MODEL), F32),
        compiler_params=_cparams(("parallel",)),
        name="ffn",
    )(*args)


def _inproj_kernel(x_ref, g_ref, w_ref, wif_ref, o_ref, oif_ref, xn_ref):
    @pl.when(pl.program_id(1) == 0)
    def _():
        xn = _rms(x_ref[...], g_ref[...]).astype(BF16)
        xn_ref[...] = xn
        oif_ref[...] = _dot(xn, wif_ref[...])

    o_ref[...] = _dot(xn_ref[...], w_ref[...]).astype(BF16)


def _inproj(x, g, w_all, w_if):
    m = x.shape[0]
    tm = min(TOKEN_TILE, m)
    return pl.pallas_call(
        _inproj_kernel,
        grid=(m // tm, Z_W // Z_TN),
        in_specs=[
            pl.BlockSpec((tm, D_MODEL), lambda i, j: (i, 0)),
            pl.BlockSpec((1, D_MODEL), lambda i, j: (0, 0)),
            pl.BlockSpec((D_MODEL, Z_TN), lambda i, j: (0, j)),
            pl.BlockSpec((D_MODEL, V7X_LANES), lambda i, j: (0, 0)),
        ],
        out_specs=[
            pl.BlockSpec((tm, Z_TN), lambda i, j: (i, j)),
            pl.BlockSpec((tm, V7X_LANES), lambda i, j: (i, 0)),
        ],
        out_shape=[
            jax.ShapeDtypeStruct((m, Z_W), BF16),
            jax.ShapeDtypeStruct((m, V7X_LANES), F32),
        ],
        scratch_shapes=[pltpu.VMEM((tm, D_MODEL), BF16)],
        compiler_params=_cparams(("parallel", "arbitrary")),
        name="in_proj",
    )(x, g.reshape(1, D_MODEL), w_all, w_if)


def _conv_lru_kernel(z_ref, cprev_ref, h0_ref, cw_ref, cb_ref, w2_ref, bax_ref, lam_ref,
                     hl_ref, qk_ref, hlast_ref, xp_ref, a_ref, b_ref, hs_ref, hc_ref, *, tt):
    pad = V7X_SUBLANES

    @pl.when(pl.program_id(1) == 0)
    def _():
        xp_ref[0:pad, :] = jnp.zeros((pad, CONV_CH), F32)
        xp_ref[pad - 3:pad, :] = cprev_ref[0]
        hc_ref[...] = h0_ref[0]

    xp_ref[pad:pad + tt, :] = z_ref[0].astype(F32)
    sp = _softplus(-lam_ref[...])
    rb = min(CONV_ROWS, tt)
    for r0 in range(0, tt, rb):
        rows = slice(r0, r0 + rb)
        for cblk in range(CONV_CH // LRU_PAIR):
            cols = slice(cblk * LRU_PAIR, (cblk + 1) * LRU_PAIR)
            xe = xp_ref[r0:pad + r0 + rb, cols]
            c = xe * cw_ref[0:1, cols] + cb_ref[:, cols]
            c = pltpu.roll(c, 1, axis=0) + xe * cw_ref[1:2, cols]
            c = pltpu.roll(c, 1, axis=0) + xe * cw_ref[2:3, cols]
            c = (pltpu.roll(c, 1, axis=0) + xe * cw_ref[3:4, cols])[pad:]
            if cblk < LRU_W // LRU_PAIR:
                gpre = _dot(c.astype(BF16), w2_ref[cblk])
                r = _sigmoid(gpre[:, :LRU_PAIR] + bax_ref[0:1, cols])
                i = _sigmoid(gpre[:, LRU_PAIR:] + bax_ref[1:2, cols])
                log_a = (-LRU_C * r) * sp[:, cols]
                a = jnp.exp(log_a)
                mult = jnp.sqrt(-jnp.tanh(log_a) * (a * a + 1.0))
                a_ref[rows, cols] = a
                b_ref[rows, cols] = mult * (i * c)
            elif cblk < 2 * LRU_W // LRU_PAIR:
                qk_ref[0, rows, cblk * LRU_PAIR - LRU_W:(cblk + 1) * LRU_PAIR - LRU_W] = (
                    c * _sigmoid(c)).astype(BF16)
            else:
                qk_ref[0, rows, cblk * LRU_PAIR - LRU_W:(cblk + 1) * LRU_PAIR - LRU_W] = (
                    (c * _sigmoid(c)) * (M_HD ** -0.5)).astype(BF16)
    xp_ref[pad - 3:pad, :] = xp_ref[pad + tt - 3:pad + tt, :]

    def body(t, h):
        h = a_ref[pl.ds(t, 1), :] * h + b_ref[pl.ds(t, 1), :]
        hs_ref[pl.ds(t, 1), :] = h
        return h

    h = lax.fori_loop(0, tt, body, hc_ref[...], unroll=8)
    hc_ref[...] = h
    hlast_ref[0] = h
    hl_ref[0] = hs_ref[...].astype(BF16)


def _conv_lru(z, conv_prev, h0, cw, cb, w2, bax, lam):
    b, t, _ = z.shape
    tt = min(CONV_TT, t)
    return pl.pallas_call(
        functools.partial(_conv_lru_kernel, tt=tt),
        grid=(b, t // tt),
        in_specs=[
            pl.BlockSpec((1, tt, CONV_CH), lambda i, j: (i, j, ZC_CONV * V7X_LANES // CONV_CH)),
            pl.BlockSpec((1, CONV_W - 1, CONV_CH), lambda i, j: (i, 0, 0)),
            pl.BlockSpec((1, 1, LRU_W), lambda i, j: (i, 0, 0)),
            pl.BlockSpec((CONV_W, CONV_CH), lambda i, j: (0, 0)),
            pl.BlockSpec((1, CONV_CH), lambda i, j: (0, 0)),
            pl.BlockSpec((LRU_W // LRU_PAIR, LRU_PAIR, 2 * LRU_PAIR), lambda i, j: (0, 0, 0)),
            pl.BlockSpec((2, LRU_W), lambda i, j: (0, 0)),
            pl.BlockSpec((1, LRU_W), lambda i, j: (0, 0)),
        ],
        out_specs=[
            pl.BlockSpec((1, tt, LRU_W), lambda i, j: (i, j, 0)),
            pl.BlockSpec((1, tt, 2 * M_W), lambda i, j: (i, j, 0)),
            pl.BlockSpec((1, 1, LRU_W), lambda i, j: (i, 0, 0)),
        ],
        out_shape=[
            jax.ShapeDtypeStruct((b, t, LRU_W), BF16),
            jax.ShapeDtypeStruct((b, t, 2 * M_W), BF16),
            jax.ShapeDtypeStruct((b, 1, LRU_W), F32),
        ],
        scratch_shapes=[
            pltpu.VMEM((tt + V7X_SUBLANES, CONV_CH), F32),
            pltpu.VMEM((tt, LRU_W), F32),
            pltpu.VMEM((tt, LRU_W), F32),
            pltpu.VMEM((tt, LRU_W), F32),
            pltpu.VMEM((1, LRU_W), F32),
        ],
        compiler_params=_cparams(("parallel", "arbitrary")),
        name="conv_lru",
    )(z, conv_prev, h0.reshape(b, 1, LRU_W), cw, cb.reshape(1, CONV_CH), w2, bax, lam.reshape(1, LRU_W))


def _mlstm_kernel(qk_ref, v_ref, o_ref, if_ref, c0_ref, n0_ref, m0_ref, ifb_ref, nw_ref,
                  om_ref, c_ref, n_ref, m_ref, *, lc):
    @pl.when(pl.program_id(1) == 0)
    def _():
        c_ref[...] = c0_ref[...]
        n_ref[...] = n0_ref[...]
        m_ref[...] = m0_ref[...]

    gif = if_ref[0] + ifb_ref[...]
    lf = jnp.minimum(gif, 0.0) - jnp.log1p(jnp.exp(-jnp.abs(gif)))
    cum = _cumsum_rows(lf)
    src = gif - pltpu.roll(cum, V7X_LANES - M_HEADS, axis=1)
    src_t = src.T if lc % V7X_LANES == 0 else None
    lane = lax.broadcasted_iota(jnp.int32, (lc, V7X_LANES), 1)
    tpos = lax.broadcasted_iota(jnp.int32, (lc, lc), 0)
    spos = lax.broadcasted_iota(jnp.int32, (lc, lc), 1)
    causal = spos <= tpos
    m_prev_all = m_ref[0]
    mlane = lax.broadcasted_iota(jnp.int32, (1, M_HEADS), 1)
    m_new_all = m_prev_all

    heads = range(M_HEADS)
    sl = lambda h: slice(h * M_HD, (h + 1) * M_HD)
    q = {h: qk_ref[0, :, sl(h)] for h in heads}
    k = {h: qk_ref[0, :, M_HEADS * M_HD + h * M_HD:M_HEADS * M_HD + (h + 1) * M_HD] for h in heads}
    vb = {h: v_ref[0, :, sl(h)] for h in heads}
    bc = {h: cum[:, M_HEADS + h:M_HEADS + h + 1] for h in heads}
    ig = {h: gif[:, h:h + 1] for h in heads}
    m_prev = {h: m_prev_all[:, h:h + 1] for h in heads}
    c_prev = {h: c_ref[0, h] for h in heads}
    n_prev = {h: n_ref[0, h:h + 1, :] for h in heads}
    if src_t is not None:
        rowv = {h: src_t[h:h + 1, :] for h in heads}
    else:
        rowv = {h: _dot_nt(jnp.where(lane == h, 1.0, 0.0), src, HI) for h in heads}
    dmat = {h: jnp.where(causal, bc[h] + rowv[h], -jnp.inf) for h in heads}
    inter = {h: bc[h] + m_prev[h] for h in heads}
    m_t = {h: jnp.maximum(inter[h], jnp.max(dmat[h], axis=1, keepdims=True)) for h in heads}
    qk = {h: _dot_nt(q[h], k[h]) for h in heads}
    qc = {h: _dot_nt(q[h], c_prev[h].astype(BF16)) for h in heads}
    nrows = {h: jnp.broadcast_to(n_prev[h], (V7X_SUBLANES, M_HD)).astype(BF16) for h in heads}
    qn = {h: _dot_nt(q[h], nrows[h])[:, 0:1] for h in heads}
    w_inter = {h: jnp.exp(inter[h] - m_t[h]) for h in heads}
    s = {h: qk[h] * jnp.exp(dmat[h] - m_t[h]) for h in heads}
    num = {h: _dot(s[h].astype(BF16), vb[h]) + w_inter[h] * qc[h] for h in heads}
    den = {h: jnp.sum(s[h], axis=1, keepdims=True) + w_inter[h] * qn[h] for h in heads}
    hh = {h: num[h] / jnp.maximum(jnp.abs(den[h]), jnp.exp(-m_t[h])) for h in heads}
    mean = {h: jnp.mean(hh[h], axis=1, keepdims=True) for h in heads}
    d = {h: hh[h] - mean[h] for h in heads}
    var = {h: jnp.mean(d[h] * d[h], axis=1, keepdims=True) for h in heads}
    for h in heads:
        hn = d[h] * lax.rsqrt(var[h] + MH_EPS)
        om_ref[0, :, sl(h)] = (_sigmoid(o_ref[0, :, sl(h)].astype(F32)) * (hn * nw_ref[:, sl(h)])).astype(BF16)
    m_new = {h: m_t[h][lc - 1:lc, :] for h in heads}
    bl = {h: bc[h][lc - 1:lc, :] for h in heads}
    g_state = {h: jnp.exp(bl[h] + m_prev[h] - m_new[h]) for h in heads}
    g_src = {h: jnp.exp(bl[h] - bc[h] + ig[h] - m_new[h]) for h in heads}
    for h in heads:
        c_ref[0, h] = g_state[h] * c_prev[h] + _dot_tn((g_src[h] * vb[h].astype(F32)).astype(BF16), k[h])
        n_ref[0, h:h + 1, :] = g_state[h] * n_prev[h] + jnp.sum(g_src[h] * k[h].astype(F32), axis=0, keepdims=True)
        m_new_all = jnp.where(mlane == h, m_new[h], m_new_all)
    m_ref[0] = m_new_all


def _mlstm(qk, z, zif, c0, n0, m0, if_bias, norm_w):
    b, t, _ = z.shape
    lc = min(MLSTM_CHUNK, t)
    w = M_W
    return pl.pallas_call(
        functools.partial(_mlstm_kernel, lc=lc),
        grid=(b, t // lc),
        in_specs=[
            pl.BlockSpec((1, lc, 2 * w), lambda i, j: (i, j, 0)),
            pl.BlockSpec((1, lc, w), lambda i, j: (i, j, ZC_MV * V7X_LANES // w)),
            pl.BlockSpec((1, lc, w), lambda i, j: (i, j, ZC_MO * V7X_LANES // w)),
            pl.BlockSpec((1, lc, V7X_LANES), lambda i, j: (i, j, 0)),
            pl.BlockSpec((1, M_HEADS, M_HD, M_HD), lambda i, j: (i, 0, 0, 0)),
            pl.BlockSpec((1, M_HEADS, M_HD), lambda i, j: (i, 0, 0)),
            pl.BlockSpec((1, 1, M_HEADS), lambda i, j: (i, 0, 0)),
            pl.BlockSpec((1, V7X_LANES), lambda i, j: (0, 0)),
            pl.BlockSpec((1, w), lambda i, j: (0, 0)),
        ],
        out_specs=[
            pl.BlockSpec((1, lc, w), lambda i, j: (i, j, 0)),
            pl.BlockSpec((1, M_HEADS, M_HD, M_HD), lambda i, j: (i, 0, 0, 0)),
            pl.BlockSpec((1, M_HEADS, M_HD), lambda i, j: (i, 0, 0)),
            pl.BlockSpec((1, 1, M_HEADS), lambda i, j: (i, 0, 0)),
        ],
        out_shape=[
            jax.ShapeDtypeStruct((b, t, w), BF16),
            jax.ShapeDtypeStruct((b, M_HEADS, M_HD, M_HD), F32),
            jax.ShapeDtypeStruct((b, M_HEADS, M_HD), F32),
            jax.ShapeDtypeStruct((b, 1, M_HEADS), F32),
        ],
        compiler_params=_cparams(("parallel", "arbitrary")),
        name="mlstm",
    )(qk, z, z, zif, c0, n0, m0.reshape(b, 1, M_HEADS), if_bias, norm_w.reshape(1, w))


def _rwkv_mix_kernel(zr_ref, zk_ref, zv_ref, zwa_ref, zgd_ref, sh_ref, mu_ref, w0_ref, w2_ref, a0_ref, a2_ref, g2_ref,
                     kk_ref, ka_ref, s0_ref, lnw_ref, lnb_ref, rk_ref, e2_ref,
                     o_ref, s_ref, last_ref, y_ref, car_ref, *, npair, tt, lc, rg):
    @pl.when(pl.program_id(2) == 0)
    def _():
        s_ref[...] = s0_ref[...]
        car_ref[...] = sh_ref[0]

    def seg(x):
        return _dot(x.astype(BF16), e2_ref[...])

    first_row = lax.broadcasted_iota(jnp.int32, (tt, 1), 0) == 0
    pairs = range(npair)
    ps = lambda p: slice(p * V7X_LANES, (p + 1) * V7X_LANES)
    hs = lambda h: slice(h * R_HD, (h + 1) * R_HD)

    def shifted(zp, p, i):
        zp = zp.astype(F32)
        prev = jnp.where(first_row, car_ref[p, i:i + 1, :], pltpu.roll(zp, 1, axis=0))
        car_ref[p, i:i + 1, :] = zp[tt - 1:tt, :]
        return zp + (prev - zp) * mu_ref[p, i:i + 1, :]

    xwa = shifted(zwa_ref[0], 0, 3)
    xgd = shifted(zgd_ref[0], 0, 4)
    twa_b = jnp.tanh(xwa).astype(BF16)
    xwa_b = xwa.astype(BF16)
    sgd_b = _sigmoid(xgd).astype(BF16)
    r_all, k_all, v_all, lw_all, a_all, b_all, g_all = {}, {}, {}, {}, {}, {}, {}

    def prologue(p):
        r_all[p] = shifted(zr_ref[0, :, ps(p)], p, 0)
        k_in = shifted(zk_ref[0, :, ps(p)], p, 1)
        v_all[p] = shifted(zv_ref[0, :, ps(p)], p, 2)
        lw_all[p] = -RWKV_DECAY_SCALE * _sigmoid(w0_ref[p] + _dot(twa_b, w2_ref[:, ps(p)]))
        a_gate = _sigmoid(a0_ref[p] + _dot(xwa_b, a2_ref[:, ps(p)]))
        g_all[p] = _dot(sgd_b, g2_ref[:, ps(p)])
        kk = k_in * kk_ref[p]
        kk = kk * lax.rsqrt(jnp.maximum(seg(kk * kk), 1e-24))
        k_all[p] = k_in * (1.0 + (a_gate - 1.0) * ka_ref[p])
        a_all[p] = -kk
        b_all[p] = kk * a_gate

    nb = rg // lc
    urow = lax.broadcasted_iota(jnp.int32, (2 * rg, 2 * rg), 0)
    scol = lax.broadcasted_iota(jnp.int32, (2 * rg, 2 * rg), 1)
    incl = urow >= rg
    trow = jnp.where(incl, urow - rg, urow)
    scol = jnp.where(scol >= rg, scol - rg, scol)
    dist = trow - scol
    gmask = (dist >= jnp.where(incl, 0, 1)) & (dist <= (trow & (lc - 1)))
    lsh = lc.bit_length() - 1
    qrow = lax.broadcasted_iota(jnp.int32, (rg, rg), 0)
    qcol = lax.broadcasted_iota(jnp.int32, (rg, rg), 1)
    same = (qrow >> lsh) == (qcol >> lsh)
    blk = jnp.where(same, 1.0, 0.0).astype(BF16)
    tril = jnp.where(qcol <= qrow, blk, jnp.zeros((), BF16))
    xrow = lax.broadcasted_iota(jnp.int32, (rg, nb * R_HD), 0)
    xcol = lax.broadcasted_iota(jnp.int32, (rg, nb * R_HD), 1)
    xmask = (xrow >> (lc.bit_length() - 1)) == (xcol >> (R_HD.bit_length() - 1))
    n_sq = max(lc.bit_length() - 1, 0)

    groups = range(tt // rg)
    ew = {}

    def prepare(p):
        for g in groups:
            rows = slice(g * rg, (g + 1) * rg)
            lw = lw_all[p][rows]
            k = k_all[p][rows]
            b = b_all[p][rows]
            lw_hi = lw.astype(BF16)
            lw_lo = (lw - lw_hi.astype(F32)).astype(BF16)
            cum = _dot(tril, lw_hi) + _dot(tril, lw_lo)
            ends = [cum[c * lc + lc - 1:c * lc + lc, :] for c in range(nb)]
            cum_end = ends[0] if nb == 1 else jnp.concatenate(
                [jnp.broadcast_to(e, (lc, V7X_LANES)) for e in ends], axis=0)
            e_neg = jnp.exp(-cum)
            at = a_all[p][rows] * jnp.exp(cum - lw)
            rt = r_all[p][rows] * jnp.exp(cum)
            e_end = jnp.exp(cum_end - cum)
            ew[p, g] = dict(at=at, rt=rt, lhs=jnp.concatenate([at, rt], axis=0).astype(BF16),
                            rhs=jnp.concatenate([b * e_neg, k * e_neg], axis=0).astype(BF16),
                            be=(b * e_end).astype(BF16), ke=(k * e_end).astype(BF16),
                            vb=v_all[p][rows].astype(BF16),
                            w_end=[jnp.exp(e) for e in ends])

    def expand(x):
        if nb == 1:
            return x
        return jnp.where(xmask, jnp.concatenate([x] * nb, axis=1), jnp.zeros((), x.dtype))

    piece = lambda c, name: ew[c[0], c[1]][name][:, hs(c[2])]
    pq2, xe, vk = {}, {}, {}

    def solve(chains):
        gm = {c: jnp.where(gmask, _dot_nt(piece(c, "lhs"), piece(c, "rhs")), 0.0) for c in chains}
        gv = {c: _dot(gm[c][:, rg:].astype(BF16), piece(c, "vb")) for c in chains}
        gr = {c: gm[c][rg:] for c in chains}
        gvr = {c: gv[c][rg:] for c in chains}
        x = {c: jnp.concatenate([piece(c, "at"), gv[c][:rg]], axis=1) for c in chains}
        pw = {c: gm[c][:rg, :rg] for c in chains}
        for i in range(n_sq):
            pwb = {c: pw[c].astype(BF16) for c in chains}
            x = {c: x[c] + _dot(pwb[c], x[c].astype(BF16)) for c in chains}
            if i + 1 < n_sq:
                pw = {c: _dot(pwb[c], pwb[c]) for c in chains}
        xb = {c: x[c].astype(BF16) for c in chains}
        for c in chains:
            pq2[c] = jnp.concatenate([piece(c, "rt"), gvr[c]], axis=1) + _dot(gr[c][:, :rg].astype(BF16), xb[c])
            xe[c] = _dot_tn(xb[c], expand(piece(c, "be")))
            vk[c] = _dot_tn(piece(c, "vb"), expand(piece(c, "ke")))

    for p in pairs:
        prologue(p)
    for p in pairs:
        prepare(p)
    solve([(p, g, h) for p in pairs for g in groups for h in range(2)])

    bonus = {p: seg((r_all[p] * k_all[p]) * rk_ref[p]) * v_all[p] for p in pairs}

    s = {(p, h): s_ref[0, 2 * p + h] for p in pairs for h in range(2)}
    for g in groups:
        for c in range(nb):
            rows = slice(c * lc, (c + 1) * lc)
            cols = slice(c * R_HD, (c + 1) * R_HD)
            sb = {ph: s[ph].astype(BF16) for ph in s}
            for p in pairs:
                ys = [_dot_nt(pq2[p, g, h][rows, :R_HD].astype(BF16), sb[p, h]) + pq2[p, g, h][rows, R_HD:]
                      for h in range(2)]
                y_ref[g * rg + c * lc:g * rg + (c + 1) * lc, ps(p)] = jnp.concatenate(ys, axis=1)
            s = {(p, h): (s[p, h] * ew[p, g]["w_end"][c][:, hs(h)]
                          + _dot(sb[p, h], xe[p, g, h][:R_HD, cols].astype(BF16))
                          + (xe[p, g, h][R_HD:, cols] + vk[p, g, h][:, cols])) for (p, h) in s}
    for (p, h) in s:
        s_ref[0, 2 * p + h] = s[p, h]

    y = {p: y_ref[:, ps(p)] for p in pairs}
    d = {p: y[p] - seg(y[p]) * (1.0 / R_HD) for p in pairs}
    var = {p: seg(d[p] * d[p]) * (1.0 / R_HD) for p in pairs}
    for p in pairs:
        yn = (d[p] * lax.rsqrt(var[p] + RWKV_GN_EPS)) * lnw_ref[p] + lnb_ref[p]
        o_ref[0, :, ps(p)] = ((yn + bonus[p]) * g_all[p]).astype(BF16)
    last_ref[0] = car_ref[...]


def _rwkv_mix(z, shift_prev, s0, mu, w0, w2p, a0, a2p, g2, k_k, k_a, ln_w, ln_b, r_k, e2):
    bsz, t, _ = z.shape
    tt = min(RWKV_TT, t)
    lc = min(RWKV_CHUNK, t)
    rg = min(RWKV_GROUP, t)
    npair = RWKV_PAIRS_PER_STEP
    wide = npair * V7X_LANES
    zpair = lambda blk: pl.BlockSpec((1, tt, wide), lambda i, p, j: (i, j, blk // npair + p))
    zshared = lambda blk: pl.BlockSpec((1, tt, V7X_LANES), lambda i, p, j: (i, j, blk))
    par_spec = pl.BlockSpec((npair, 1, V7X_LANES), lambda i, p, j: (p, 0, 0))
    lora_spec = pl.BlockSpec((V7X_LANES, wide), lambda i, p, j: (0, p))
    st_spec = pl.BlockSpec((1, 2 * npair, R_HD, R_HD), lambda i, p, j: (i, p, 0, 0))
    pair = lambda x: x.reshape(R_PAIRS, 1, V7X_LANES)

    def pieces(x):
        lead = x.shape[:-1]
        rkv = x[..., :3 * R_W].reshape(lead + (3, R_PAIRS, V7X_LANES))
        rkv = jnp.moveaxis(rkv, -3, -2)
        lora = jnp.broadcast_to(x[..., 3 * R_W:].reshape(lead + (1, 2, V7X_LANES)), lead + (R_PAIRS, 2, V7X_LANES))
        return jnp.concatenate([rkv, lora], axis=-2)

    out, s_new, last = pl.pallas_call(
        functools.partial(_rwkv_mix_kernel, npair=npair, tt=tt, lc=lc, rg=rg),
        grid=(bsz, R_PAIRS // npair, t // tt),
        in_specs=[
            zpair(ZC_RW), zpair(ZC_RW + 8), zpair(ZC_RW + 16), zshared(ZC_RW + 24), zshared(ZC_RW + 25),
            pl.BlockSpec((1, npair, 5, V7X_LANES), lambda i, p, j: (i, p, 0, 0)),
            pl.BlockSpec((npair, 5, V7X_LANES), lambda i, p, j: (p, 0, 0)),
            par_spec, lora_spec, par_spec, lora_spec, lora_spec, par_spec, par_spec,
            st_spec, par_spec, par_spec, par_spec,
            pl.BlockSpec((V7X_LANES, V7X_LANES), lambda i, p, j: (0, 0)),
        ],
        out_specs=[
            pl.BlockSpec((1, tt, wide), lambda i, p, j: (i, j, p)),
            st_spec,
            pl.BlockSpec((1, npair, 5, V7X_LANES), lambda i, p, j: (i, p, 0, 0)),
        ],
        out_shape=[
            jax.ShapeDtypeStruct((bsz, t, R_W), BF16),
            jax.ShapeDtypeStruct((bsz, R_HEADS, R_HD, R_HD), F32),
            jax.ShapeDtypeStruct((bsz, R_PAIRS, 5, V7X_LANES), F32),
        ],
        scratch_shapes=[pltpu.VMEM((tt, wide), F32), pltpu.VMEM((npair, 5, V7X_LANES), F32)],
        compiler_params=_cparams(("parallel", "parallel", "arbitrary")),
        name="rwkv_mix",
    )(z, z, z, z, z, pieces(shift_prev.reshape(bsz, R_IN)), pieces(mu), pair(w0), w2p, pair(a0), a2p, g2,
      pair(k_k), pair(k_a), s0, pair(ln_w), pair(ln_b), pair(r_k), e2)
    rkv = jnp.moveaxis(last[:, :, :3, :], 2, 1).reshape(bsz, 3 * R_W)
    shift_new = jnp.concatenate([rkv, last[:, 0, 3:, :].reshape(bsz, 2 * V7X_LANES)], axis=-1)
    return out, s_new, shift_new.reshape(bsz, 1, R_IN)


def _merge_kernel(x_ref, zg_ref, hl_ref, om_ref, or_ref, wb_ref, wo_ref, o_ref):
    acc = None
    for gidx, br in enumerate((hl_ref, om_ref, or_ref)):
        sl = slice(gidx * D_MODEL, (gidx + 1) * D_MODEL)
        term = _sigmoid(zg_ref[:, sl].astype(F32)) * _dot(br[...], wb_ref[gidx])
        acc = term if acc is None else acc + term
    o_ref[...] = x_ref[...] + _dot(acc.astype(BF16), wo_ref[...])


def _merge(x, z2d, hl, om, orw, wb, wo):
    m = x.shape[0]
    tm = min(MERGE_TILE, m)
    tok = pl.BlockSpec((tm, D_MODEL), lambda i: (i, 0))
    return pl.pallas_call(
        _merge_kernel,
        grid=(m // tm,),
        in_specs=[
            tok,
            pl.BlockSpec((tm, 3 * D_MODEL), lambda i: (i, ZC_GATE * V7X_LANES // (3 * D_MODEL))),
            tok, tok, tok,
            pl.BlockSpec((3, D_MODEL, D_MODEL), lambda i: (0, 0, 0)),
            pl.BlockSpec((D_MODEL, D_MODEL), lambda i: (0, 0)),
        ],
        out_specs=tok,
        out_shape=jax.ShapeDtypeStruct((m, D_MODEL), F32),
        compiler_params=_cparams(("parallel",)),
        name="merge",
    )(x, z2d, hl, om, orw, wb, wo)


def _prep_layer_weights(p, l):
    w_in = p["w_in"][l]
    c0 = CONV_CH
    c1 = c0 + M_W
    c2 = c1 + M_W
    c3 = c2 + 2 * M_HEADS
    c4 = c3 + 3 * R_W + 2 * R_LORA + V7X_LANES
    w_if = jnp.pad(w_in[:, c2:c3], ((0, 0), (0, V7X_LANES - 2 * M_HEADS))).astype(BF16)
    w_all = jnp.concatenate([w_in[:, :c0], w_in[:, c4:], w_in[:, c0:c1], w_in[:, c1:c2], w_in[:, c3:c4]],
                            axis=1).astype(BF16)
    def pairs(w):
        z = jnp.zeros((LRU_BD, LRU_BD), F32)
        return jnp.stack([jnp.block([[w[2 * i], z], [z, w[2 * i + 1]]]) for i in range(LRU_BLOCKS // 2)])
    w2 = jnp.concatenate([pairs(p["lru_wa"][l]), pairs(p["lru_wx"][l])], axis=2).astype(BF16)
    bax = jnp.stack([p["lru_ba"][l], p["lru_bx"][l]])
    if_bias = jnp.pad(p["mlstm_if_bias"][l], (0, V7X_LANES - 2 * M_HEADS)).reshape(1, V7X_LANES)
    zpad = jnp.zeros((R_LORA, R_W), F32)
    w2p = jnp.concatenate([p["rwkv_w2"][l], zpad], axis=0).astype(BF16)
    a2p = jnp.concatenate([zpad, p["rwkv_a2"][l]], axis=0).astype(BF16)
    return dict(
        w_all=w_all, w_if=w_if, w2=w2, bax=bax, if_bias=if_bias, w2p=w2p, a2p=a2p,
        g2=p["rwkv_g2"][l].astype(BF16),
        wb=p["w_branch"][l].astype(BF16), wo=p["w_out"][l].astype(BF16),
    )


def _segment_matrix():
    half = jnp.arange(V7X_LANES) // R_HD
    return (half[:, None] == half[None, :]).astype(BF16)


def _run_group(x, states, p, lw_list, ffn_w, final_norm):
    bsz, t, d_model = x.shape
    m = bsz * t
    assert d_model == D_MODEL and t >= CONV_W - 1
    assert all(t % min(tile, t) == 0 for tile in (CONV_TT, MLSTM_CHUNK, RWKV_TT, RWKV_GROUP, RWKV_CHUNK))
    assert all(m % min(tile, m) == 0 for tile in (TOKEN_TILE, MERGE_TILE))
    e2 = _segment_matrix()
    xf = x.reshape(m, D_MODEL)
    new_states = []
    depth = p["w_in"].shape[0]
    for l in range(depth):
        lw = lw_list[l]
        conv_prev, lru_h, m_c, m_n, m_m, shift_prev, rwkv_s = states[l]
        xf = _ffn(xf, p["ffn1_norm"][l], *ffn_w[0], layer=l)
        z2d, zif = _inproj(xf, p["mix_norm"][l], lw["w_all"], lw["w_if"])
        z = z2d.reshape(bsz, t, Z_W)
        hl, qk, h_last = _conv_lru(z, conv_prev, lru_h, p["conv_w"][l], p["conv_b"][l], lw["w2"], lw["bax"],
                                   p["lru_lambda"][l])
        om, c_new, n_new, m_new = _mlstm(qk, z, zif.reshape(bsz, t, V7X_LANES), m_c, m_n, m_m, lw["if_bias"],
                                         p["mlstm_norm"][l])
        orw, s_new, shift_new = _rwkv_mix(z, shift_prev, rwkv_s, p["rwkv_mu"][l], p["rwkv_w0"][l], lw["w2p"], p["rwkv_a0"][l],
                               lw["a2p"], lw["g2"], p["rwkv_k_k"][l], p["rwkv_k_a"][l], p["rwkv_ln_w"][l],
                               p["rwkv_ln_b"][l], p["rwkv_r_k"][l].reshape(-1), e2)
        xf = _merge(xf, z2d, hl.reshape(m, D_MODEL), om.reshape(m, D_MODEL), orw.reshape(m, D_MODEL),
                    lw["wb"], lw["wo"])
        xf = _ffn(xf, p["ffn2_norm"][l], *ffn_w[1], layer=l, final_g=final_norm if l == depth - 1 else None)
        conv_new = z[:, t - (CONV_W - 1):, ZC_CONV * V7X_LANES:ZC_CONV * V7X_LANES + CONV_CH].astype(F32)
        new_states.append((conv_new, h_last.reshape(bsz, LRU_W), c_new, n_new, m_new.reshape(bsz, M_HEADS),
                           shift_new, s_new))
    stacked = tuple(jnp.stack([st[i] for st in new_states]) for i in range(7))
    return xf.reshape(bsz, t, D_MODEL), stacked


def _zero_states(bsz):
    return (jnp.zeros((bsz, CONV_W - 1, CONV_CH), F32), jnp.zeros((bsz, LRU_W), F32),
            jnp.zeros((bsz, M_HEADS, M_HD, M_HD), F32), jnp.zeros((bsz, M_HEADS, M_HD), F32),
            jnp.zeros((bsz, M_HEADS), F32), jnp.zeros((bsz, 1, R_IN), F32),
            jnp.zeros((bsz, R_HEADS, R_HD, R_HD), F32))


def kernel(x_prompt, x_sample, state_conv, state_lru_h, state_mlstm_C, state_mlstm_n, state_mlstm_m, state_rwkv_shift, state_rwkv_S, ffn1_norm, ffn1_w_gate, ffn1_w_up, ffn1_w_down, mix_norm, w_in, conv_w, conv_b, lru_wa, lru_ba, lru_wx, lru_bx, lru_lambda, mlstm_if_bias, mlstm_norm, rwkv_mu, rwkv_w0, rwkv_w2, rwkv_a0, rwkv_a2, rwkv_g2, rwkv_k_k, rwkv_k_a, rwkv_r_k, rwkv_ln_w, rwkv_ln_b, w_branch, w_out, ffn2_norm, ffn2_w_gate, ffn2_w_up, ffn2_w_down, final_norm):
    p = dict(ffn1_norm=ffn1_norm, ffn1_w_gate=ffn1_w_gate, ffn1_w_up=ffn1_w_up, ffn1_w_down=ffn1_w_down,
             mix_norm=mix_norm, w_in=w_in, conv_w=conv_w, conv_b=conv_b,
             lru_wa=lru_wa, lru_ba=lru_ba, lru_wx=lru_wx, lru_bx=lru_bx, lru_lambda=lru_lambda,
             mlstm_if_bias=mlstm_if_bias, mlstm_norm=mlstm_norm,
             rwkv_mu=rwkv_mu, rwkv_w0=rwkv_w0, rwkv_w2=rwkv_w2, rwkv_a0=rwkv_a0, rwkv_a2=rwkv_a2,
             rwkv_g2=rwkv_g2, rwkv_k_k=rwkv_k_k, rwkv_k_a=rwkv_k_a, rwkv_r_k=rwkv_r_k,
             rwkv_ln_w=rwkv_ln_w, rwkv_ln_b=rwkv_ln_b, w_branch=w_branch, w_out=w_out,
             ffn2_norm=ffn2_norm, ffn2_w_gate=ffn2_w_gate, ffn2_w_up=ffn2_w_up, ffn2_w_down=ffn2_w_down)
    depth = w_in.shape[0]
    lw_list = [_prep_layer_weights(p, l) for l in range(depth)]
    ffn_w = [tuple(p[f"ffn{n}_w_{part}"].astype(BF16) for part in ("gate", "up", "down")) for n in (1, 2)]
    states_p = [_zero_states(x_prompt.shape[0]) for _ in range(depth)]
    y_p, st_p = _run_group(x_prompt, states_p, p, lw_list, ffn_w, final_norm)
    states_s = [(state_conv[l], state_lru_h[l], state_mlstm_C[l], state_mlstm_n[l], state_mlstm_m[l],
                 state_rwkv_shift[l], state_rwkv_S[l]) for l in range(depth)]
    y_s, st_s = _run_group(x_sample, states_s, p, lw_list, ffn_w, final_norm)
    return (y_p, y_s) + st_p + st_s
```
